```python
import math
import jax, jax.numpy as jnp
from jax import lax
import numpy as np

D_MODEL = 1024
BATCH = 8
SEQ = 4096
DEPTH = 1
DEC_BATCH = 32
DEC_SEQ = 4
PAST_LEN = 16384
PAGE_SIZE = 128

SSD_HEADS = 8
SSD_HEAD_DIM = 64
SSD_INNER = SSD_HEADS * SSD_HEAD_DIM
SSD_GROUPS = 2
SSD_STATE = 128
SSD_CONV = 4
SSD_CHUNK = 128
CONV_DIM = SSD_INNER + 2 * SSD_GROUPS * SSD_STATE

ATT_HEADS = 8
ATT_HEAD_DIM = 64
ATT_INNER = ATT_HEADS * ATT_HEAD_DIM
IDX_HEADS = 8
IDX_DIM = 64
TOPK_MAX = 256
Q_BLOCK = 128

N_BUCKETS = 32
MAX_DISTANCE = 128

MEM_TOKENS = 256
MEM_HEADS = 4
MEM_HEAD_DIM = 128
MEM_INNER = MEM_HEADS * MEM_HEAD_DIM

PEER_HEADS = 8
PEER_KEYS = 128
PEER_EXPERTS = PEER_KEYS * PEER_KEYS
PEER_HALF = 128
PEER_QDIM = 2 * PEER_HALF
PEER_TOPK = 16
PEER_BLOCK = 128

MIX_WIDTH = SSD_INNER + ATT_INNER
IN_SPLITS = (SSD_INNER, CONV_DIM, SSD_HEADS, ATT_INNER, ATT_INNER, ATT_INNER, IDX_HEADS * IDX_DIM, IDX_DIM, IDX_HEADS)
IN_COLS = sum(IN_SPLITS)
ALPHA = (2.0 * DEPTH) ** 0.25
BETA = (8.0 * DEPTH) ** -0.25
LN_EPS = 1e-5

kernel_name = "hymba_ssd_dsa_peer_decode_step"


def layer_norm(x, g, b):
    xf = x.astype(jnp.float32)
    mu = jnp.mean(xf, axis=-1, keepdims=True)
    var = jnp.mean(jnp.square(xf - mu), axis=-1, keepdims=True)
    return ((xf - mu) * lax.rsqrt(var + LN_EPS)).astype(x.dtype) * g + b


def gated_rmsnorm(y, z, w):
    h = (y * jax.nn.silu(z)).astype(jnp.float32)
    h = h * lax.rsqrt(jnp.mean(h * h, axis=-1, keepdims=True) + LN_EPS)
    return h.astype(y.dtype) * w


def project_in(x, w_in):
    b, L, _ = x.shape
    points = [int(p) for p in np.cumsum(IN_SPLITS)[:-1]]
    z, xbc, dt_raw, q, k, v, q_idx, k_idx, w_idx = jnp.split(x @ w_in, points, axis=-1)
    q = q.reshape(b, L, ATT_HEADS, ATT_HEAD_DIM)
    k = k.reshape(b, L, ATT_HEADS, ATT_HEAD_DIM)
    v = v.reshape(b, L, ATT_HEADS, ATT_HEAD_DIM)
    q_idx = q_idx.reshape(b, L, IDX_HEADS, IDX_DIM) * (IDX_DIM ** -0.5)
    w_idx = w_idx * (IDX_HEADS ** -0.5)
    return z, xbc, dt_raw, q, k, v, q_idx, k_idx, w_idx


def causal_conv(xbc, buf, w, bias):
    L = xbc.shape[1]
    full = jnp.concatenate([buf, xbc], axis=1)
    out = full[:, 0:L] * w[0]
    for tap in range(1, SSD_CONV):
        out = out + full[:, tap:tap + L] * w[tap]
    return jax.nn.silu(out + bias), full[:, -(SSD_CONV - 1):]


def ssd_scan(x, dt, A, B, C, h0, chunk):
    b, L, H, P = x.shape
    nc = L // chunk
    rep = H // SSD_GROUPS
    B = jnp.repeat(B, rep, axis=2)
    C = jnp.repeat(C, rep, axis=2)
    xd = x * dt[..., None]
    a = (dt * A).astype(jnp.float32)
    resh = lambda t: t.reshape((b, nc, chunk) + t.shape[2:])
    xd, a, B, C = resh(xd), resh(a), resh(B), resh(C)
    a_cum = jnp.cumsum(a, axis=2)
    seg = a_cum[:, :, :, None, :] - a_cum[:, :, None, :, :]
    causal = jnp.tril(jnp.ones((chunk, chunk), dtype=bool))[None, None, :, :, None]
    decay = jnp.exp(jnp.where(causal, seg, -jnp.inf)).astype(x.dtype)
    scores = jnp.einsum('bcihn,bcjhn->bcijh', C, B)
    y_diag = jnp.einsum('bcijh,bcjhp->bcihp', scores * decay, xd)
    decay_to_end = jnp.exp(a_cum[:, :, -1:, :] - a_cum).astype(x.dtype)
    chunk_states = jnp.einsum('bcjhn,bcjh,bcjhp->bchpn', B, decay_to_end, xd)
    chunk_decay = jnp.exp(a_cum[:, :, -1, :]).astype(x.dtype)

    def step(h, inp):
        s, d = inp
        return h * d[:, :, None, None] + s, h

    h_final, h_starts = lax.scan(step, h0, (jnp.moveaxis(chunk_states, 1, 0), jnp.moveaxis(chunk_decay, 1, 0)))
    h_starts = jnp.moveaxis(h_starts, 0, 1)
    y_off = jnp.einsum('bcihn,bchpn,bcih->bcihp', C, h_starts, jnp.exp(a_cum).astype(x.dtype))
    return (y_diag + y_off).reshape(b, L, H, P), h_final


def ssd_mixer(z, xbc, dt_raw, conv_buf, h0, chunk, conv_w, conv_b, dt_bias, a_log, d_skip, norm_w):
    xbc_c, new_buf = causal_conv(xbc, conv_buf, conv_w, conv_b)
    xs, Bm, Cm = jnp.split(xbc_c, [SSD_INNER, SSD_INNER + SSD_GROUPS * SSD_STATE], axis=-1)
    b, L, _ = xs.shape
    xs = xs.reshape(b, L, SSD_HEADS, SSD_HEAD_DIM)
    Bm = Bm.reshape(b, L, SSD_GROUPS, SSD_STATE)
    Cm = Cm.reshape(b, L, SSD_GROUPS, SSD_STATE)
    dt = jax.nn.softplus(dt_raw + dt_bias)
    A = -jnp.exp(a_log)
    y, h = ssd_scan(xs, dt, A, Bm, Cm, h0, chunk)
    y = y + xs * d_skip[:, None]
    return gated_rmsnorm(y.reshape(b, L, SSD_INNER), z, norm_w), h, new_buf


def rel_bucket(dist):
    n = jnp.maximum(dist, 0)
    max_exact = N_BUCKETS // 2
    nf = jnp.maximum(n, 1).astype(jnp.float32)
    large = max_exact + (jnp.log(nf / max_exact) / math.log(MAX_DISTANCE / max_exact) * (N_BUCKETS - max_exact)).astype(jnp.int32)
    large = jnp.minimum(large, N_BUCKETS - 1)
    return jnp.where(n < max_exact, n, large)


def indexer_scores(q_idx, w_idx, k_idx):
    s = jax.nn.relu(jnp.einsum('bthd,bsd->bths', q_idx, k_idx))
    return jnp.einsum('bths,bth->bts', s, w_idx).astype(jnp.float32)


def sparse_attend(q, k_sel, v_sel, valid, bias):
    logits = jnp.einsum('bthd,btkhd->bthk', q, k_sel).astype(jnp.float32) * (ATT_HEAD_DIM ** -0.5)
    logits = logits + jnp.moveaxis(bias, -1, 2).astype(jnp.float32)
    logits = jnp.where(valid[:, :, None, :], logits, -jnp.inf)
    p = jax.nn.softmax(logits, axis=-1).astype(v_sel.dtype)
    return jnp.einsum('bthk,btkhd->bthd', p, v_sel)


def dsa_prompt(q, k, v, q_idx, w_idx, k_idx, rel_bias):
    b, S = q.shape[:2]
    topk = min(TOPK_MAX, S // 4)
    key_pos = jnp.arange(S)
    bidx = jnp.arange(b)[:, None, None]

    def block(i):
        t0 = i * Q_BLOCK
        qb = lax.dynamic_slice_in_dim(q, t0, Q_BLOCK, axis=1)
        qib = lax.dynamic_slice_in_dim(q_idx, t0, Q_BLOCK, axis=1)
        wib = lax.dynamic_slice_in_dim(w_idx, t0, Q_BLOCK, axis=1)
        pos_q = t0 + jnp.arange(Q_BLOCK)
        sc = indexer_scores(qib, wib, k_idx)
        sc = jnp.where(key_pos[None, None, :] <= pos_q[None, :, None], sc, -jnp.inf)
        _, sel = lax.top_k(sc, topk)
        valid = sel <= pos_q[None, :, None]
        bias = rel_bias[rel_bucket(pos_q[None, :, None] - sel)]
        return sparse_attend(qb, k[bidx, sel], v[bidx, sel], valid, bias)

    out = lax.map(block, jnp.arange(S // Q_BLOCK))
    return jnp.moveaxis(out, 0, 1).reshape(b, S, ATT_INNER)


def dsa_sample(q, k_new, v_new, q_idx, w_idx, kidx_new, cache_k, cache_v, cache_kidx, page_table, rel_bias):
    b, T = q.shape[:2]
    past = page_table.shape[1] * PAGE_SIZE
    L = past + T
    topk = min(TOPK_MAX, L // 4)
    kidx_past = cache_kidx[page_table].reshape(b, past, IDX_DIM)
    kidx_all = jnp.concatenate([kidx_past, kidx_new], axis=1)
    pos_q = past + jnp.arange(T)
    key_pos = jnp.arange(L)
    sc = indexer_scores(q_idx, w_idx, kidx_all)
    sc = jnp.where(key_pos[None, None, :] <= pos_q[None, :, None], sc, -jnp.inf)
    _, sel = lax.top_k(sc, topk)
    valid = sel <= pos_q[None, :, None]
    bidx = jnp.arange(b)[:, None, None]
    in_past = (sel < past)[..., None, None]
    sp = jnp.minimum(sel, past - 1)
    phys = page_table[bidx, sp // PAGE_SIZE]
    off = sp % PAGE_SIZE
    sn = jnp.clip(sel - past, 0, T - 1)
    k_sel = jnp.where(in_past, cache_k[phys, off], k_new[bidx, sn])
    v_sel = jnp.where(in_past, cache_v[phys, off], v_new[bidx, sn])
    bias = rel_bias[rel_bucket(pos_q[None, :, None] - sel)]
    return sparse_attend(q, k_sel, v_sel, valid, bias).reshape(b, T, ATT_INNER)


def mem_kv(mem, w_ck, w_cv):
    b, M, _ = mem.shape
    return (mem @ w_ck).reshape(b, M, MEM_HEADS, MEM_HEAD_DIM), (mem @ w_cv).reshape(b, M, MEM_HEADS, MEM_HEAD_DIM)


def mem_attend(h, mk, mv, w_cq, w_co):
    b, L, _ = h.shape
    q = (h @ w_cq).reshape(b, L, MEM_HEADS, MEM_HEAD_DIM)
    logits = jnp.einsum('blhd,bmhd->bhlm', q, mk).astype(jnp.float32) * (MEM_HEAD_DIM ** -0.5)
    p = jax.nn.softmax(logits, axis=-1).astype(mv.dtype)
    o = jnp.einsum('bhlm,bmhd->blhd', p, mv).reshape(b, L, MEM_INNER)
    return o @ w_co


def peer(h, wq, subkeys, u, v):
    shp = h.shape
    xt = h.reshape(-1, D_MODEL)
    n = xt.shape[0]
    nb = -(-n // PEER_BLOCK)
    xt = jnp.pad(xt, ((0, nb * PEER_BLOCK - n), (0, 0))).reshape(nb, PEER_BLOCK, D_MODEL)

    def block(xb):
        q = (xb @ wq).reshape(PEER_BLOCK, PEER_HEADS, 2, PEER_HALF)
        s = jnp.einsum('thpd,hpkd->thpk', q, subkeys).astype(jnp.float32)
        s_top, i_top = lax.top_k(s, PEER_TOPK)
        cand = (s_top[:, :, 0, :, None] + s_top[:, :, 1, None, :]).reshape(PEER_BLOCK, PEER_HEADS, PEER_TOPK * PEER_TOPK)
        cand_idx = (i_top[:, :, 0, :, None] * PEER_KEYS + i_top[:, :, 1, None, :]).reshape(PEER_BLOCK, PEER_HEADS, PEER_TOPK * PEER_TOPK)
        g_s, g_i = lax.top_k(cand, PEER_TOPK)
        e = jnp.take_along_axis(cand_idx, g_i, axis=-1)
        g = jax.nn.softmax(g_s, axis=-1).astype(xb.dtype)
        act = jax.nn.gelu(jnp.einsum('thed,td->the', u[e], xb))
        return jnp.einsum('the,thed->td', g * act, v[e])

    out = lax.map(block, xt).reshape(-1, D_MODEL)[:n]
    return out.reshape(shp)


def finish_layer(x, mixed, mk, mv, w_out, ln1_g, ln1_b, w_cq, w_co, ln2_g, ln2_b, peer_wq, peer_subkeys, peer_u, peer_v, ln3_g, ln3_b):
    h = layer_norm(ALPHA * x + mixed @ w_out, ln1_g, ln1_b)
    h = layer_norm(ALPHA * h + mem_attend(h, mk, mv, w_cq, w_co), ln2_g, ln2_b)
    return layer_norm(ALPHA * h + peer(h, peer_wq, peer_subkeys, peer_u, peer_v), ln3_g, ln3_b)


def setup_inputs(seed: int = 0) -> dict:
    key = jax.random.key(seed)
    ks = jax.random.split(key, 40)
    f32 = jnp.float32
    nrm = lambda k, shape, s=1.0: s * jax.random.normal(k, shape, f32)
    n_pages = PAST_LEN // PAGE_SIZE
    n_used = DEC_BATCH * n_pages
    n_pool = (5 * n_used) // 4 + 1
    page_table = jax.random.permutation(ks[0], n_pool)[:n_used].reshape(DEC_BATCH, n_pages).astype(jnp.int32)
    dt0 = jnp.exp(jax.random.uniform(ks[1], (SSD_HEADS,), f32, math.log(1e-3), math.log(1e-1)))
    return {
        "x_prompt": nrm(ks[2], (BATCH, SEQ, D_MODEL)),
        "mem_prompt": nrm(ks[3], (BATCH, MEM_TOKENS, D_MODEL)),
        "x_sample": nrm(ks[4], (DEC_BATCH, DEC_SEQ, D_MODEL)),
        "cache_k": nrm(ks[5], (n_pool, PAGE_SIZE, ATT_HEADS, ATT_HEAD_DIM)),
        "cache_v": nrm(ks[6], (n_pool, PAGE_SIZE, ATT_HEADS, ATT_HEAD_DIM)),
        "cache_kidx": nrm(ks[7], (n_pool, PAGE_SIZE, IDX_DIM)),
        "cache_mem_k": nrm(ks[8], (DEC_BATCH, MEM_TOKENS, MEM_HEADS, MEM_HEAD_DIM)),
        "cache_mem_v": nrm(ks[9], (DEC_BATCH, MEM_TOKENS, MEM_HEADS, MEM_HEAD_DIM)),
        "state_ssm": nrm(ks[10], (DEC_BATCH, SSD_HEADS, SSD_HEAD_DIM, SSD_STATE), 0.1),
        "state_conv": nrm(ks[11], (DEC_BATCH, SSD_CONV - 1, CONV_DIM)),
        "page_table": page_table,
        "w_in": nrm(ks[12], (D_MODEL, IN_COLS), D_MODEL ** -0.5),
        "conv_w": nrm(ks[13], (SSD_CONV, CONV_DIM), SSD_CONV ** -0.5),
        "conv_b": nrm(ks[14], (CONV_DIM,), 0.01),
        "dt_bias": dt0 + jnp.log(-jnp.expm1(-dt0)),
        "a_log": jnp.log(jax.random.uniform(ks[15], (SSD_HEADS,), f32, 1.0, 16.0)),
        "d_skip": 1.0 + nrm(ks[16], (SSD_HEADS,), 0.01),
        "ssd_norm_w": 1.0 + nrm(ks[17], (SSD_INNER,), 0.01),
        "rel_bias": nrm(ks[18], (N_BUCKETS, ATT_HEADS), 0.5),
        "w_out": nrm(ks[19], (MIX_WIDTH, D_MODEL), BETA * MIX_WIDTH ** -0.5),
        "ln1_g": 1.0 + nrm(ks[20], (D_MODEL,), 0.01),
        "ln1_b": nrm(ks[21], (D_MODEL,), 0.01),
        "w_cq": nrm(ks[22], (D_MODEL, MEM_INNER), D_MODEL ** -0.5),
        "w_ck": nrm(ks[23], (D_MODEL, MEM_INNER), D_MODEL ** -0.5),
        "w_cv": nrm(ks[24], (D_MODEL, MEM_INNER), D_MODEL ** -0.5),
        "w_co": nrm(ks[25], (MEM_INNER, D_MODEL), BETA * MEM_INNER ** -0.5),
        "ln2_g": 1.0 + nrm(ks[26], (D_MODEL,), 0.01),
        "ln2_b": nrm(ks[27], (D_MODEL,), 0.01),
        "peer_wq": nrm(ks[28], (D_MODEL, PEER_HEADS * PEER_QDIM), D_MODEL ** -0.5),
        "peer_subkeys": nrm(ks[29], (PEER_HEADS, 2, PEER_KEYS, PEER_HALF), PEER_HALF ** -0.5),
        "peer_u": nrm(ks[30], (PEER_EXPERTS, D_MODEL), D_MODEL ** -0.5),
        "peer_v": nrm(ks[31], (PEER_EXPERTS, D_MODEL), BETA * PEER_HEADS ** -0.5),
        "ln3_g": 1.0 + nrm(ks[32], (D_MODEL,), 0.01),
        "ln3_b": nrm(ks[33], (D_MODEL,), 0.01),
    }


def reference(x_prompt, mem_prompt, x_sample, cache_k, cache_v, cache_kidx, cache_mem_k, cache_mem_v,
              state_ssm, state_conv, page_table, w_in, conv_w, conv_b, dt_bias, a_log, d_skip, ssd_norm_w,
              rel_bias, w_out, ln1_g, ln1_b, w_cq, w_ck, w_cv, w_co, ln2_g, ln2_b,
              peer_wq, peer_subkeys, peer_u, peer_v, ln3_g, ln3_b):
    b_p = x_prompt.shape[0]
    z, xbc, dt_raw, q, k_p, v_p, q_idx, kidx_p, w_idx = project_in(x_prompt, w_in)
    conv0 = jnp.zeros((b_p, SSD_CONV - 1, CONV_DIM), x_prompt.dtype)
    h0 = jnp.zeros((b_p, SSD_HEADS, SSD_HEAD_DIM, SSD_STATE), x_prompt.dtype)
    ssd_out, ssm_p, conv_p = ssd_mixer(z, xbc, dt_raw, conv0, h0, SSD_CHUNK, conv_w, conv_b, dt_bias, a_log, d_skip, ssd_norm_w)
    att_out = dsa_prompt(q, k_p, v_p, q_idx, w_idx, kidx_p, rel_bias)
    memk_p, memv_p = mem_kv(mem_prompt, w_ck, w_cv)
    y_prompt = finish_layer(x_prompt, jnp.concatenate([ssd_out, att_out], axis=-1), memk_p, memv_p,
                            w_out, ln1_g, ln1_b, w_cq, w_co, ln2_g, ln2_b, peer_wq, peer_subkeys, peer_u, peer_v, ln3_g, ln3_b)

    t_s = x_sample.shape[1]
    z, xbc, dt_raw, q, k_s, v_s, q_idx, kidx_s, w_idx = project_in(x_sample, w_in)
    ssd_out, ssm_s, conv_s = ssd_mixer(z, xbc, dt_raw, state_conv, state_ssm, t_s, conv_w, conv_b, dt_bias, a_log, d_skip, ssd_norm_w)
    att_out = dsa_sample(q, k_s, v_s, q_idx, w_idx, kidx_s, cache_k, cache_v, cache_kidx, page_table, rel_bias)
    y_sample = finish_layer(x_sample, jnp.concatenate([ssd_out, att_out], axis=-1), cache_mem_k, cache_mem_v,
                            w_out, ln1_g, ln1_b, w_cq, w_co, ln2_g, ln2_b, peer_wq, peer_subkeys, peer_u, peer_v, ln3_g, ln3_b)

    return (y_prompt, y_sample, k_p, v_p, kidx_p, memk_p, memv_p, ssm_p, conv_p, k_s, v_s, kidx_s, ssm_s, conv_s)
```

```python
import functools
import math

import jax
import jax.numpy as jnp
import numpy as np
from jax import lax
from jax.experimental import pallas as pl
from jax.experimental.pallas import tpu as pltpu

F32 = jnp.float32
BF16 = jnp.bfloat16
I32 = jnp.int32
MXU_DTYPE = BF16

D_MODEL = 1024
DEPTH = 1
ALPHA = (2.0 * DEPTH) ** 0.25
LN_EPS = 1e-5

SSD_HEADS = 8
SSD_HEAD_DIM = 64
SSD_INNER = 512
SSD_GROUPS = 2
SSD_STATE = 128
SSD_CONV = 4
SSD_CHUNK = 128
CONV_DIM = 1024

ATT_HEADS = 8
ATT_HEAD_DIM = 64
ATT_INNER = 512
IDX_HEADS = 8
IDX_DIM = 64
TOPK_MAX = 256
PAGE_SIZE = 128
N_BUCKETS = 32
MAX_DISTANCE = 128

MEM_HEADS = 4
MEM_HEAD_DIM = 128
MEM_INNER = 512

PEER_HEADS = 8
PEER_KEYS = 128
PEER_HALF = 128
PEER_TOPK = 16
PEER_PAIRS = PEER_HEADS * PEER_TOPK

LANES = 128
TOK_BLOCK = 128
DT_LANE = 64
WIDX_LANE = 72
NEG_BIG = -1e30
INT_MIN = -(2 ** 31)
VMEM_LIMIT = 56 * 1024 * 1024
SLAB_ROWS = 4
FEAT_CHUNKS = D_MODEL // LANES
TILE_STRIDE = 136


def _cparams(sem, vmem=VMEM_LIMIT):
    return pltpu.CompilerParams(dimension_semantics=sem, vmem_limit_bytes=vmem)


def _mm(a, b):
    return jnp.dot(a.astype(MXU_DTYPE), b.astype(MXU_DTYPE), preferred_element_type=F32)


def _mm_nt(a, b):
    return lax.dot_general(a.astype(MXU_DTYPE), b.astype(MXU_DTYPE), (((1,), (1,)), ((), ())),
                           preferred_element_type=F32)


def _mm_exact(a, b):
    return jnp.dot(a, b, preferred_element_type=F32, precision=lax.Precision.HIGHEST)


def _sigmoid(x):
    return 1.0 / (1.0 + jnp.exp(-x))


def _layer_norm(x, g, b):
    mu = jnp.mean(x, axis=-1, keepdims=True)
    xc = x - mu
    var = jnp.mean(xc * xc, axis=-1, keepdims=True)
    return xc * lax.rsqrt(var + LN_EPS) * g + b


def _bucket_thresholds():
    max_exact = N_BUCKETS // 2
    out = []
    for n in range(max_exact, MAX_DISTANCE + 1):
        v = max_exact + int(math.log(n / max_exact) / math.log(MAX_DISTANCE / max_exact) * (N_BUCKETS - max_exact))
        out.append(min(v, N_BUCKETS - 1))
    thr = []
    for bkt in range(max_exact + 1, N_BUCKETS):
        thr.append(max_exact + next(i for i, v in enumerate(out) if v >= bkt))
    return tuple(thr)


_BUCKET_THR = _bucket_thresholds()


def _rel_bucket(dist):
    n = jnp.maximum(dist, 0)
    max_exact = N_BUCKETS // 2
    large = jnp.full(n.shape, max_exact, I32)
    for t in _BUCKET_THR:
        large = large + (n >= t).astype(I32)
    return jnp.where(n < max_exact, n, large)


def _monotone_key(x):
    b = lax.bitcast_convert_type(x, I32)
    return b ^ ((b >> 31) & 0x7FFFFFFF)


def _proj_in_kernel(x_ref, wa_ref, wb_ref, wc_ref, wd_ref,
                    z_ref, xbc_ref, q_ref, k_ref, v_ref, kb_ref, vb_ref, qi_ref, kk_ref, sm_ref):
    xb = x_ref[...].astype(MXU_DTYPE)
    z_ref[...] = jnp.dot(xb, wa_ref[:, :SSD_INNER], preferred_element_type=F32)
    xbc_ref[...] = jnp.dot(xb, wa_ref[:, SSD_INNER:], preferred_element_type=F32)
    q = jnp.dot(xb, wb_ref[:, 0:512], preferred_element_type=F32)
    q_ref[...] = (q * (ATT_HEAD_DIM ** -0.5)).astype(q_ref.dtype)
    k = jnp.dot(xb, wb_ref[:, 512:1024], preferred_element_type=F32)
    k_ref[...] = k
    kb_ref[...] = k.astype(kb_ref.dtype)
    v = jnp.dot(xb, wb_ref[:, 1024:1536], preferred_element_type=F32)
    v_ref[...] = v
    vb_ref[...] = v.astype(vb_ref.dtype)
    qi = jnp.dot(xb, wb_ref[:, 1536:2048], preferred_element_type=F32)
    qi_ref[...] = (qi * (IDX_DIM ** -0.5)).astype(qi_ref.dtype)
    kk_ref[...] = jnp.dot(xb, wc_ref[...], preferred_element_type=F32).astype(kk_ref.dtype)
    sm = jnp.dot(xb, wd_ref[...], preferred_element_type=F32)
    lane = lax.broadcasted_iota(I32, sm.shape, 1)
    is_w = (lane >= WIDX_LANE) & (lane < WIDX_LANE + IDX_HEADS)
    sm_ref[...] = jnp.where(is_w, sm * (IDX_HEADS ** -0.5), sm)


def _project_in(x2d, wa, wb, wc, wd):
    n = x2d.shape[0]
    tm = min(256, n)
    row = lambda w: pl.BlockSpec((tm, w), lambda i: (i, 0))
    full = lambda a: pl.BlockSpec(a.shape, lambda i: (0, 0))
    out_shapes = (
        jax.ShapeDtypeStruct((n, SSD_INNER), F32),
        jax.ShapeDtypeStruct((n, CONV_DIM), F32),
        jax.ShapeDtypeStruct((n, ATT_INNER), MXU_DTYPE),
        jax.ShapeDtypeStruct((n, ATT_INNER), F32),
        jax.ShapeDtypeStruct((n, ATT_INNER), F32),
        jax.ShapeDtypeStruct((n, ATT_INNER), MXU_DTYPE),
        jax.ShapeDtypeStruct((n, ATT_INNER), MXU_DTYPE),
        jax.ShapeDtypeStruct((n, ATT_INNER), MXU_DTYPE),
        jax.ShapeDtypeStruct((n, LANES), MXU_DTYPE),
        jax.ShapeDtypeStruct((n, LANES), F32),
    )
    return pl.pallas_call(
        _proj_in_kernel,
        grid=(n // tm,),
        in_specs=[row(D_MODEL), full(wa), full(wb), full(wc), full(wd)],
        out_specs=tuple(row(s.shape[1]) for s in out_shapes),
        out_shape=out_shapes,
        compiler_params=_cparams(("parallel",)),
        name="project_in",
    )(x2d, wa, wb, wc, wd)


def _ssd_kernel(n_valid, z_ref, xbc_ref, sm_ref, conv0_ref, h0_ref, convw_ref, convb_ref, alog_ref, dtb_ref,
                dskip_ref, normw_ref, y_ref, hout_ref, tail_ref, cbuf, hst, ysc):
    q_len = z_ref.shape[1]
    c = pl.program_id(1)
    pad = 8

    @pl.when(c == 0)
    def _():
        cbuf[pad - 3:pad, :] = conv0_ref[0]
        hst[...] = h0_ref[0]

    xbc = xbc_ref[0]
    cbuf[pad:pad + q_len, :] = xbc
    w = convw_ref[...]
    conv = cbuf[pad - 3:pad - 3 + q_len, :] * w[0:1]
    conv = conv + cbuf[pad - 2:pad - 2 + q_len, :] * w[1:2]
    conv = conv + cbuf[pad - 1:pad - 1 + q_len, :] * w[2:3]
    conv = conv + xbc * w[3:4]
    conv = conv + convb_ref[...]
    xc = conv * _sigmoid(conv)
    tail = cbuf[pad + n_valid - 3:pad + n_valid, :]
    cbuf[pad - 3:pad, :] = tail

    row = lax.broadcasted_iota(I32, (q_len, LANES), 0)
    lane = lax.broadcasted_iota(I32, (q_len, LANES), 1)
    is_dt = (lane >= DT_LANE) & (lane < DT_LANE + SSD_HEADS) & (row < n_valid)
    raw = sm_ref[0] + dtb_ref[...]
    dt_full = jnp.maximum(raw, 0.0) + jnp.log1p(jnp.exp(-jnp.abs(raw)))
    dtm = jnp.where(is_dt, dt_full, 0.0)
    a_full = dtm * (-jnp.exp(alog_ref[...]))
    ri = lax.broadcasted_iota(I32, (q_len, q_len), 0)
    ci = lax.broadcasted_iota(I32, (q_len, q_len), 1)
    causal = ci <= ri
    acum = _mm_exact(causal.astype(F32), a_full)
    acum_t = acum.T

    xs = xc[:, :SSD_INNER]
    for g in range(SSD_GROUPS):
        bm = xc[:, SSD_INNER + g * SSD_STATE:SSD_INNER + (g + 1) * SSD_STATE]
        cm = xc[:, SSD_INNER + (SSD_GROUPS + g) * SSD_STATE:SSD_INNER + (SSD_GROUPS + g + 1) * SSD_STATE]
        scores = _mm_nt(cm, bm)
        bm_t = bm.T
        for hh in range(SSD_HEADS // SSD_GROUPS):
            h = g * (SSD_HEADS // SSD_GROUPS) + hh
            col = acum[:, DT_LANE + h:DT_LANE + h + 1]
            rowv = acum_t[DT_LANE + h:DT_LANE + h + 1, :]
            a_last = acum[q_len - 1:q_len, DT_LANE + h:DT_LANE + h + 1]
            decay = jnp.exp(jnp.where(causal, col - rowv, -jnp.inf))
            x_h = xs[:, h * SSD_HEAD_DIM:(h + 1) * SSD_HEAD_DIM]
            xd = x_h * dtm[:, DT_LANE + h:DT_LANE + h + 1]
            h_t = hst[h]
            y = _mm(scores * decay, xd) + _mm(cm, h_t) * jnp.exp(col)
            y = y + x_h * dskip_ref[:, h * SSD_HEAD_DIM:(h + 1) * SSD_HEAD_DIM]
            ysc[:, h * SSD_HEAD_DIM:(h + 1) * SSD_HEAD_DIM] = y
            hst[h] = h_t * jnp.exp(a_last) + _mm(bm_t, xd * jnp.exp(a_last - col))

    zz = z_ref[0]
    hg = ysc[...] * (zz * _sigmoid(zz))
    hg = hg * lax.rsqrt(jnp.mean(hg * hg, axis=-1, keepdims=True) + LN_EPS)
    y_ref[0] = hg * normw_ref[...]

    @pl.when(c == pl.num_programs(1) - 1)
    def _():
        hout_ref[0] = hst[...]
        tail_ref[0] = cbuf[pad - 3:pad, :]


def _ssd_mixer(z, xbc, small, conv0, h0_t, n_valid, conv_w, conv_b, alog_lane, dtb_lane, dskip_row, norm_w):
    b, length, _ = z.shape
    q_len = SSD_CHUNK
    nc = length // q_len
    blk = lambda w: pl.BlockSpec((1, q_len, w), lambda i, c: (i, c, 0))
    per_b3 = lambda s: pl.BlockSpec((1,) + s, lambda i, c: (i,) + (0,) * len(s))
    const2 = lambda a: pl.BlockSpec(a.shape, lambda i, c: (0, 0))
    st_shape = (SSD_HEADS, SSD_STATE, SSD_HEAD_DIM)
    return pl.pallas_call(
        functools.partial(_ssd_kernel, n_valid),
        grid=(b, nc),
        in_specs=[blk(SSD_INNER), blk(CONV_DIM), blk(LANES), per_b3((SSD_CONV - 1, CONV_DIM)), per_b3(st_shape),
                  const2(conv_w), const2(conv_b), const2(alog_lane), const2(dtb_lane), const2(dskip_row),
                  const2(norm_w)],
        out_specs=(blk(SSD_INNER), per_b3(st_shape), per_b3((SSD_CONV - 1, CONV_DIM))),
        out_shape=(jax.ShapeDtypeStruct((b, length, SSD_INNER), F32),
                   jax.ShapeDtypeStruct((b,) + st_shape, F32),
                   jax.ShapeDtypeStruct((b, SSD_CONV - 1, CONV_DIM), F32)),
        scratch_shapes=[pltpu.VMEM((q_len + 8, CONV_DIM), F32), pltpu.VMEM(st_shape, F32),
                        pltpu.VMEM((q_len, SSD_INNER), F32)],
        compiler_params=_cparams(("parallel", "arbitrary")),
        name="ssd_mixer",
    )(z, xbc, small, conv0, h0_t, conv_w, conv_b, alog_lane, dtb_lane, dskip_row, norm_w)


def _kth_largest_key(count_ge, k, rows):
    def body(i, tau):
        cand = tau + jnp.left_shift(jnp.int32(1), 31 - i)
        return jnp.where(count_ge(cand) >= k, cand, tau)
    return lax.fori_loop(0, 32, body, jnp.full((rows, 1), INT_MIN, I32))


def _bias_tiles(rb_ref, bias_sc, n_tiles):
    a = lax.broadcasted_iota(I32, (LANES, LANES), 0)
    bcol = lax.broadcasted_iota(I32, (LANES, LANES), 1)
    for d in range(n_tiles):
        bucket = _rel_bucket(d * LANES + a - bcol)
        for h in range(ATT_HEADS):
            tile = jnp.zeros((LANES, LANES), F32)
            for kb in range(N_BUCKETS):
                tile = jnp.where(bucket == kb, rb_ref[kb, h], tile)
            bias_sc[d, h] = tile


def _dsa_prompt_kernel(topk, q_ref, qi_ref, sm_ref, kk_ref, kb_ref, vb_ref, rb_ref, o_ref,
                       keys_sc, bias_sc, m_sc, l_sc, acc_sc):
    i = pl.program_id(1)
    tq = q_ref.shape[1]
    n_far = bias_sc.shape[0] - 1

    @pl.when(i == 0)
    def _():
        _bias_tiles(rb_ref, bias_sc, n_far + 1)

    lane = lax.broadcasted_iota(I32, (tq, LANES), 1)
    rowi = lax.broadcasted_iota(I32, (tq, LANES), 0)
    low_half = lane < ATT_HEAD_DIM
    sm = sm_ref[0]
    qi = qi_ref[0]
    qv = q_ref[0]

    def pair_masked(x, h):
        xp = x[:, (h // 2) * LANES:(h // 2 + 1) * LANES]
        keep = low_half if h % 2 == 0 else jnp.logical_not(low_half)
        return jnp.where(keep, xp, jnp.zeros_like(xp))

    qi_m = [pair_masked(qi, h) for h in range(IDX_HEADS)]
    w_cols = [sm[:, WIDX_LANE + h:WIDX_LANE + h + 1] for h in range(IDX_HEADS)]

    def score_blk(j, carry):
        kk = kk_ref[0, pl.ds(pl.multiple_of(j * LANES, LANES), LANES), :]
        acc = jnp.zeros((tq, LANES), F32)
        for h in range(IDX_HEADS):
            acc = acc + jnp.maximum(_mm_nt(qi_m[h], kk), 0.0) * w_cols[h]
        ok = (j * LANES + lane) <= (i * tq + rowi)
        keys_sc[j] = _monotone_key(jnp.where(ok, acc, -jnp.inf))
        return carry

    lax.fori_loop(0, i + 1, score_blk, 0)

    def count_ge(cand):
        def body(j, cnt):
            return cnt + jnp.where(keys_sc[j] >= cand, 1.0, 0.0)
        cnt = lax.fori_loop(0, i + 1, body, jnp.zeros((tq, LANES), F32))
        return jnp.sum(cnt, axis=1, keepdims=True)

    tau = _kth_largest_key(count_ge, float(topk), tq)

    q_m = [pair_masked(qv, h) for h in range(ATT_HEADS)]
    m_sc[...] = jnp.full(m_sc.shape, NEG_BIG, F32)
    l_sc[...] = jnp.zeros(l_sc.shape, F32)
    acc_sc[...] = jnp.zeros(acc_sc.shape, F32)

    def kv_blk(j, carry):
        start = pl.multiple_of(j * LANES, LANES)
        sel = (keys_sc[j] >= tau) & ((j * LANES + lane) <= (i * tq + rowi))
        d = jnp.minimum(i - j, n_far)
        for h in range(ATT_HEADS):
            g = h // 2
            kblk = kb_ref[0, pl.ds(start, LANES), g * LANES:(g + 1) * LANES]
            vblk = vb_ref[0, pl.ds(start, LANES), g * LANES:(g + 1) * LANES]
            s = _mm_nt(q_m[h], kblk) + bias_sc[d, h]
            s = jnp.where(sel, s, NEG_BIG)
            m_old = m_sc[h]
            m_new = jnp.maximum(m_old, jnp.max(s, axis=1, keepdims=True))
            alpha = jnp.exp(m_old - m_new)
            p = jnp.exp(s - m_new)
            l_sc[h] = alpha * l_sc[h] + jnp.sum(p, axis=1, keepdims=True)
            acc_sc[h] = alpha * acc_sc[h] + _mm(p, vblk)
            m_sc[h] = m_new
        return carry

    lax.fori_loop(0, i + 1, kv_blk, 0)

    for g in range(ATT_HEADS // 2):
        lo = acc_sc[2 * g] / l_sc[2 * g]
        hi = acc_sc[2 * g + 1] / l_sc[2 * g + 1]
        o_ref[0, :, g * LANES:(g + 1) * LANES] = jnp.where(low_half, lo, hi)


def _dsa_prompt(q_b, qi_b, small, kk_b, k_b, v_b, rel_bias):
    b, s, _ = q_b.shape
    tq = LANES
    nq = s // tq
    topk = min(TOPK_MAX, s // 4)
    n_bias_tiles = 3
    assert MAX_DISTANCE <= LANES + 1
    qblk = lambda w: pl.BlockSpec((1, tq, w), lambda i, j: (i, j, 0))
    per_b = lambda w: pl.BlockSpec((1, s, w), lambda i, j: (i, 0, 0))
    return pl.pallas_call(
        functools.partial(_dsa_prompt_kernel, topk),
        grid=(b, nq),
        in_specs=[qblk(ATT_INNER), qblk(ATT_INNER), qblk(LANES), per_b(LANES), per_b(ATT_INNER), per_b(ATT_INNER),
                  pl.BlockSpec(memory_space=pltpu.SMEM)],
        out_specs=qblk(ATT_INNER),
        out_shape=jax.ShapeDtypeStruct((b, s, ATT_INNER), F32),
        scratch_shapes=[pltpu.VMEM((nq, tq, LANES), I32),
                        pltpu.VMEM((n_bias_tiles, ATT_HEADS, LANES, LANES), F32),
                        pltpu.VMEM((ATT_HEADS, tq, 1), F32), pltpu.VMEM((ATT_HEADS, tq, 1), F32),
                        pltpu.VMEM((ATT_HEADS, tq, LANES), F32)],
        compiler_params=_cparams(("parallel", "arbitrary")),
        name="dsa_prompt",
    )(q_b, qi_b, small, kk_b, k_b, v_b, rel_bias)


def _dsa_sample_score_kernel(topk, t_new, pt_ref, qi_ref, w_ref, kidx_ref, kknew_ref, keys_ref, tau_ref, keys_sc):
    j = pl.program_id(1)
    n_pages = pl.num_programs(1) - 1
    rows = qi_ref.shape[1]
    lane = lax.broadcasted_iota(I32, (rows, LANES), 1)
    rowi = lax.broadcasted_iota(I32, (rows, LANES), 0)

    def head_sum(s):
        s = jnp.maximum(s, 0.0) * w_ref[0]
        return jnp.concatenate(
            [jnp.sum(s[t * IDX_HEADS:(t + 1) * IDX_HEADS], axis=0, keepdims=True) for t in range(t_new)], axis=0)

    @pl.when(j < n_pages)
    def _():
        sc = head_sum(_mm_nt(qi_ref[0], kidx_ref[0]))
        keys_sc[j] = _monotone_key(sc)

    @pl.when(j == n_pages)
    def _():
        sc = head_sum(_mm_nt(qi_ref[0], kknew_ref[0]))
        ok = lane[:t_new] <= rowi[:t_new]
        keys_sc[j] = _monotone_key(jnp.where(ok, sc, -jnp.inf))

        def count_ge(cand):
            def body(jj, cnt):
                return cnt + jnp.where(keys_sc[jj] >= cand, 1.0, 0.0)
            cnt = lax.fori_loop(0, n_pages + 1, body, jnp.zeros((t_new, LANES), F32))
            return jnp.sum(cnt, axis=1, keepdims=True)

        tau = _kth_largest_key(count_ge, float(topk), t_new)
        tau_ref[0] = jnp.broadcast_to(tau, (t_new, LANES))
        keys_ref[0] = keys_sc[...]


def _dsa_sample_attn_kernel(t_new, past, pt_ref, q_ref, keys_ref, tau_ref, kc_ref, vc_ref, knew_ref, vnew_ref,
                            rbrow_ref, o_ref, m_sc, l_sc, acc_sc):
    j = pl.program_id(1)
    n_pages = pl.num_programs(1) - 1
    rows = q_ref.shape[1]
    lane = lax.broadcasted_iota(I32, (rows, LANES), 1)
    rowi = lax.broadcasted_iota(I32, (rows, LANES), 0)
    tok = rowi // ATT_HEADS

    @pl.when(j == 0)
    def _():
        m_sc[...] = jnp.full(m_sc.shape, NEG_BIG, F32)
        l_sc[...] = jnp.zeros(l_sc.shape, F32)
        acc_sc[...] = jnp.zeros(acc_sc.shape, F32)

    keys = keys_ref[0, 0]
    sel_t = keys >= tau_ref[0]
    sel = jnp.concatenate([jnp.broadcast_to(sel_t[t:t + 1], (ATT_HEADS, LANES)) for t in range(t_new)], axis=0)

    def update(kblk, vblk, key_pos, sel_rows):
        s = _mm_nt(q_ref[0], kblk)
        bucket = _rel_bucket(past + tok - key_pos)
        bias = jnp.zeros((rows, LANES), F32)
        for kb in range(N_BUCKETS):
            bias = jnp.where(bucket == kb, rbrow_ref[:, kb:kb + 1], bias)
        s = jnp.where(sel_rows, s + bias, NEG_BIG)
        m_old = m_sc[...]
        m_new = jnp.maximum(m_old, jnp.max(s, axis=1, keepdims=True))
        alpha = jnp.exp(m_old - m_new)
        p = jnp.exp(s - m_new)
        l_sc[...] = alpha * l_sc[...] + jnp.sum(p, axis=1, keepdims=True)
        acc_sc[...] = alpha * acc_sc[...] + _mm(p, vblk)
        m_sc[...] = m_new

    @pl.when(j < n_pages)
    def _():
        update(kc_ref[0], vc_ref[0], j * PAGE_SIZE + lane, sel)

    @pl.when(j == n_pages)
    def _():
        key_pos = past + lane
        update(knew_ref[0], vnew_ref[0], key_pos, sel & (lane <= tok))
        out = acc_sc[...] / l_sc[...]
        col_head = lax.broadcasted_iota(I32, (ATT_HEADS, ATT_INNER), 1) // ATT_HEAD_DIM
        row_head = lax.broadcasted_iota(I32, (ATT_HEADS, ATT_INNER), 0)
        for t in range(t_new):
            blk = out[t * ATT_HEADS:(t + 1) * ATT_HEADS]
            o_ref[0, t:t + 1, :] = jnp.sum(jnp.where(col_head == row_head, blk, 0.0), axis=0, keepdims=True)


def _dsa_sample(q_b, qi_b, small, k_b, v_b, kk_b, cache_k, cache_v, cache_kidx, page_table, rel_bias):
    b, t_new, _ = q_b.shape
    n_pages = page_table.shape[1]
    past = n_pages * PAGE_SIZE
    topk = min(TOPK_MAX, (past + t_new) // 4)
    n_pool = cache_k.shape[0]
    rows = t_new * ATT_HEADS

    head_of_col = np.arange(ATT_INNER) // ATT_HEAD_DIM
    blockdiag = jnp.asarray(head_of_col[None, :] == np.arange(ATT_HEADS)[:, None])

    def block_diag_rows(x):
        xr = jnp.where(blockdiag[None, None], x[:, :, None, :], jnp.zeros((), x.dtype))
        return xr.reshape(b, rows, ATT_INNER)

    qi_rows = jnp.pad(qi_b.reshape(b, rows, IDX_DIM), ((0, 0), (0, 0), (0, LANES - IDX_DIM)))
    w_rows = small[:, :, WIDX_LANE:WIDX_LANE + IDX_HEADS].reshape(b, rows, 1)
    kidx_pages = jnp.pad(cache_kidx, ((0, 0), (0, 0), (0, LANES - IDX_DIM)))
    kk_new = jnp.pad(kk_b, ((0, 0), (0, LANES - t_new), (0, 0)))

    grid = (b, n_pages + 1)
    page = lambda i, j, pt: (pt[i, jnp.minimum(j, n_pages - 1)], 0, 0)
    per_b = lambda s: pl.BlockSpec((1,) + s, lambda i, j, pt: (i,) + (0,) * len(s))
    keys, tau = pl.pallas_call(
        functools.partial(_dsa_sample_score_kernel, topk, t_new),
        grid_spec=pltpu.PrefetchScalarGridSpec(
            num_scalar_prefetch=1, grid=grid,
            in_specs=[per_b((rows, LANES)), per_b((rows, 1)),
                      pl.BlockSpec((1, PAGE_SIZE, LANES), page), per_b((LANES, LANES))],
            out_specs=(per_b((n_pages + 1, t_new, LANES)), per_b((t_new, LANES))),
            scratch_shapes=[pltpu.VMEM((n_pages + 1, t_new, LANES), I32)]),
        out_shape=(jax.ShapeDtypeStruct((b, n_pages + 1, t_new, LANES), I32),
                   jax.ShapeDtypeStruct((b, t_new, LANES), I32)),
        compiler_params=_cparams(("parallel", "arbitrary")),
        name="dsa_sample_scores",
    )(page_table, qi_rows, w_rows, kidx_pages, kk_new)

    q_rows = block_diag_rows(q_b)
    rb_rows = jnp.tile(rel_bias.T, (t_new, 1))
    k_new = jnp.pad(k_b, ((0, 0), (0, LANES - t_new), (0, 0)))
    v_new = jnp.pad(v_b, ((0, 0), (0, LANES - t_new), (0, 0)))
    ck = cache_k.reshape(n_pool, PAGE_SIZE, ATT_INNER)
    cv = cache_v.reshape(n_pool, PAGE_SIZE, ATT_INNER)
    return pl.pallas_call(
        functools.partial(_dsa_sample_attn_kernel, t_new, past),
        grid_spec=pltpu.PrefetchScalarGridSpec(
            num_scalar_prefetch=1, grid=grid,
            in_specs=[per_b((rows, ATT_INNER)),
                      pl.BlockSpec((1, 1, t_new, LANES), lambda i, j, pt: (i, j, 0, 0)),
                      per_b((t_new, LANES)),
                      pl.BlockSpec((1, PAGE_SIZE, ATT_INNER), page), pl.BlockSpec((1, PAGE_SIZE, ATT_INNER), page),
                      per_b((LANES, ATT_INNER)), per_b((LANES, ATT_INNER)),
                      pl.BlockSpec((rows, N_BUCKETS), lambda i, j, pt: (0, 0))],
            out_specs=per_b((t_new, ATT_INNER)),
            scratch_shapes=[pltpu.VMEM((rows, 1), F32), pltpu.VMEM((rows, 1), F32),
                            pltpu.VMEM((rows, ATT_INNER), F32)]),
        out_shape=jax.ShapeDtypeStruct((b, t_new, ATT_INNER), F32),
        compiler_params=_cparams(("parallel", "arbitrary")),
        name="dsa_sample_attend",
    )(page_table, q_rows, keys, tau, ck, cv, k_new, v_new, rb_rows)


def _matmul_kernel(x_ref, w_ref, o_ref):
    o_ref[...] = _mm(x_ref[...], w_ref[...])


def _matmul(x2d, w):
    n, kdim = x2d.shape
    tm = min(256, n)
    return pl.pallas_call(
        _matmul_kernel,
        grid=(n // tm,),
        in_specs=[pl.BlockSpec((tm, kdim), lambda i: (i, 0)), pl.BlockSpec(w.shape, lambda i: (0, 0))],
        out_specs=pl.BlockSpec((tm, w.shape[1]), lambda i: (i, 0)),
        out_shape=jax.ShapeDtypeStruct((n, w.shape[1]), F32),
        compiler_params=_cparams(("parallel",)),
        name="mem_kv_proj",
    )(x2d, w)


def _out_proj_kernel(x_ref, ssd_ref, att_ref, wout_ref, g_ref, b_ref, wcq_ref, h1_ref, qm_ref):
    m = _mm(ssd_ref[...], wout_ref[:SSD_INNER, :]) + _mm(att_ref[...], wout_ref[SSD_INNER:, :])
    h1 = _layer_norm(ALPHA * x_ref[...] + m, g_ref[...], b_ref[...])
    h1_ref[...] = h1
    qm_ref[...] = _mm(h1, wcq_ref[...])


def _out_proj(x2d, ssd2d, att2d, w_out, g, b, w_cq):
    n = x2d.shape[0]
    tm = min(256, n)
    row = lambda w: pl.BlockSpec((tm, w), lambda i: (i, 0))
    full = lambda a: pl.BlockSpec(a.shape, lambda i: (0, 0))
    return pl.pallas_call(
        _out_proj_kernel,
        grid=(n // tm,),
        in_specs=[row(D_MODEL), row(SSD_INNER), row(ATT_INNER), full(w_out), full(g), full(b), full(w_cq)],
        out_specs=(row(D_MODEL), row(MEM_INNER)),
        out_shape=(jax.ShapeDtypeStruct((n, D_MODEL), F32), jax.ShapeDtypeStruct((n, MEM_INNER), F32)),
        compiler_params=_cparams(("parallel",)),
        name="out_proj_ln1",
    )(x2d, ssd2d, att2d, w_out, g, b, w_cq)


def _mem_attend_kernel(q_ref, mk_ref, mv_ref, o_ref):
    q = q_ref[0]
    for h in range(MEM_HEADS):
        sl = slice(h * MEM_HEAD_DIM, (h + 1) * MEM_HEAD_DIM)
        logits = _mm_nt(q[:, sl], mk_ref[0, :, sl]) * (MEM_HEAD_DIM ** -0.5)
        logits = logits - jnp.max(logits, axis=-1, keepdims=True)
        p = jnp.exp(logits)
        p = p / jnp.sum(p, axis=-1, keepdims=True)
        o_ref[0, :, sl] = _mm(p, mv_ref[0, :, sl])


def _mem_attend(qm, mk, mv):
    b, length, _ = qm.shape
    tq = min(512, length)
    m_tok = mk.shape[1]
    return pl.pallas_call(
        _mem_attend_kernel,
        grid=(b, length // tq),
        in_specs=[pl.BlockSpec((1, tq, MEM_INNER), lambda i, j: (i, j, 0)),
                  pl.BlockSpec((1, m_tok, MEM_INNER), lambda i, j: (i, 0, 0)),
                  pl.BlockSpec((1, m_tok, MEM_INNER), lambda i, j: (i, 0, 0))],
        out_specs=pl.BlockSpec((1, tq, MEM_INNER), lambda i, j: (i, j, 0)),
        out_shape=jax.ShapeDtypeStruct((b, length, MEM_INNER), F32),
        compiler_params=_cparams(("parallel", "parallel")),
        name="mem_attend",
    )(qm, mk, mv)


def _top_rows(s, n_top, with_index):
    rows = s.shape[0]
    ridx = lax.broadcasted_iota(I32, s.shape, 0)
    vals, idxs = [], []
    for _ in range(n_top):
        m = jnp.max(s, axis=0, keepdims=True)
        first = jnp.min(jnp.where(s == m, ridx, rows), axis=0, keepdims=True)
        s = jnp.where(ridx == first, -jnp.inf, s)
        vals.append(m)
        idxs.append(first)
    return jnp.concatenate(vals, axis=0), (jnp.concatenate(idxs, axis=0) if with_index else None)


def _peer_route_kernel(h1_ref, o_ref, wco_ref, g_ref, b_ref, wq_ref, sk_ref, h2_ref, e_ref, gate_ref):
    h2 = _layer_norm(ALPHA * h1_ref[...] + _mm(o_ref[...], wco_ref[...]), g_ref[...], b_ref[...])
    h2_ref[...] = h2
    h2b = h2.astype(MXU_DTYPE)
    for h in range(PEER_HEADS):
        tops = []
        for p in range(2):
            c = h * 2 + p
            qs = jnp.dot(h2b, wq_ref[:, c * PEER_HALF:(c + 1) * PEER_HALF], preferred_element_type=F32)
            s_t = _mm_nt(sk_ref[c], qs)
            tops.append(_top_rows(s_t, PEER_TOPK, True))
        (v0, i0), (v1, i1) = tops
        cand = jnp.concatenate([v0[a:a + 1] + v1 for a in range(PEER_TOPK)], axis=0)
        cidx = jnp.concatenate([i0[a:a + 1] * PEER_KEYS + i1 for a in range(PEER_TOPK)], axis=0)
        pos = lax.broadcasted_iota(I32, cand.shape, 0)
        gs, es = [], []
        for _ in range(PEER_TOPK):
            m = jnp.max(cand, axis=0, keepdims=True)
            first = jnp.min(jnp.where(cand == m, pos, cand.shape[0]), axis=0, keepdims=True)
            hit = pos == first
            es.append(jnp.max(jnp.where(hit, cidx, -1), axis=0, keepdims=True))
            cand = jnp.where(hit, -jnp.inf, cand)
            gs.append(m)
        g_s = jnp.concatenate(gs, axis=0)
        ex = jnp.exp(g_s - g_s[0:1])
        gate_ref[0, h * PEER_TOPK:(h + 1) * PEER_TOPK, :] = ex / jnp.sum(ex, axis=0, keepdims=True)
        e_ref[0, h * PEER_TOPK:(h + 1) * PEER_TOPK, :] = jnp.concatenate(es, axis=0)


def _peer_route(h1, o, w_co, g, b, wq, subkeys):
    n = h1.shape[0]
    tm = TOK_BLOCK
    nb = n // tm
    row = lambda w: pl.BlockSpec((tm, w), lambda i: (i, 0))
    full = lambda a: pl.BlockSpec(a.shape, lambda i: (0,) * a.ndim)
    pairs = pl.BlockSpec((1, PEER_PAIRS, tm), lambda i: (i, 0, 0))
    return pl.pallas_call(
        _peer_route_kernel,
        grid=(nb,),
        in_specs=[row(D_MODEL), row(MEM_INNER), full(w_co), full(g), full(b), full(wq), full(subkeys)],
        out_specs=(row(D_MODEL), pairs, pairs),
        out_shape=(jax.ShapeDtypeStruct((n, D_MODEL), F32),
                   jax.ShapeDtypeStruct((nb, PEER_PAIRS, tm), I32),
                   jax.ShapeDtypeStruct((nb, PEER_PAIRS, tm), F32)),
        compiler_params=_cparams(("parallel",)),
        name="ln2_peer_route",
    )(h1, o, w_co, g, b, wq, subkeys)


def _pack_table(t):
    tb = lax.bitcast_convert_type(t.astype(BF16), jnp.uint16).astype(jnp.uint32)
    half = t.shape[1] // 2
    words = tb[:, :half] | (tb[:, half:] << 16)
    return lax.bitcast_convert_type(words, I32).reshape(t.shape[0] * SLAB_ROWS, LANES)


def _unpack_words(w):
    lo = lax.bitcast_convert_type(w << 16, F32)
    hi = lax.bitcast_convert_type(w & jnp.int32(-65536), F32)
    return lo, hi


def _gather_token_rows(idx_smem, tab_ref, tile_ref, t):
    for p in range(PEER_PAIRS):
        base = pl.multiple_of(idx_smem[p, t] * SLAB_ROWS, SLAB_ROWS)
        tile_ref[pl.ds(p, SLAB_ROWS, stride=TILE_STRIDE), :] = tab_ref[pl.ds(base, SLAB_ROWS), :]


def _load_indices(e_hbm, idx_smem, sem):
    cp = pltpu.make_async_copy(e_hbm.at[pl.program_id(0)], idx_smem, sem)
    cp.start()
    cp.wait()


def _gelu_tanh(x):
    return 0.5 * x * (1.0 + jnp.tanh(math.sqrt(2.0 / math.pi) * (x + 0.044715 * (x * x * x))))


def _peer_in_kernel(e_hbm, tab_ref, h2_ref, gate_ref, w_ref, idx_smem, sem, tile_ref):
    _load_indices(e_hbm, idx_smem, sem)
    tm = h2_ref.shape[0]
    lane = lax.broadcasted_iota(I32, (PEER_PAIRS, tm), 1)

    def tok_body(t, act_t):
        _gather_token_rows(idx_smem, tab_ref, tile_ref, t)
        x_t = h2_ref[t]
        acc = jnp.zeros((PEER_PAIRS, LANES), F32)
        for j in range(SLAB_ROWS):
            lo, hi = _unpack_words(tile_ref[j * TILE_STRIDE:j * TILE_STRIDE + PEER_PAIRS, :])
            acc = acc + lo * x_t[j:j + 1, :] + hi * x_t[SLAB_ROWS + j:SLAB_ROWS + j + 1, :]
        col = jnp.sum(acc, axis=1, keepdims=True)
        return jnp.where(lane == t, col, act_t)

    act_t = lax.fori_loop(0, tm, tok_body, jnp.zeros((PEER_PAIRS, tm), F32))
    w_ref[0] = gate_ref[0] * _gelu_tanh(act_t)


def _peer_out_kernel(e_hbm, tab_ref, h2_ref, w_ref, g_ref, b_ref, y_ref, idx_smem, sem, tile_ref, out_sc):
    _load_indices(e_hbm, idx_smem, sem)
    tm = h2_ref.shape[0]
    lane = lax.broadcasted_iota(I32, (PEER_PAIRS, tm), 1)
    w_t = w_ref[0]

    def tok_body(t, carry):
        _gather_token_rows(idx_smem, tab_ref, tile_ref, t)
        w_col = jnp.sum(jnp.where(lane == t, w_t, 0.0), axis=1, keepdims=True)
        los, his = [], []
        for j in range(SLAB_ROWS):
            lo, hi = _unpack_words(tile_ref[j * TILE_STRIDE:j * TILE_STRIDE + PEER_PAIRS, :])
            los.append(jnp.sum(lo * w_col, axis=0, keepdims=True))
            his.append(jnp.sum(hi * w_col, axis=0, keepdims=True))
        out_sc[t] = jnp.concatenate(los + his, axis=0)
        return carry

    lax.fori_loop(0, tm, tok_body, 0)
    x = ALPHA * h2_ref[...] + out_sc[...]
    mean3 = lambda a: jnp.sum(jnp.sum(a, axis=2, keepdims=True), axis=1, keepdims=True) * (1.0 / D_MODEL)
    xc = x - mean3(x)
    y_ref[...] = xc * lax.rsqrt(mean3(xc * xc) + LN_EPS) * g_ref[...] + b_ref[...]


def _peer_scratch(tm):
    return [pltpu.SMEM((PEER_PAIRS, tm), I32), pltpu.SemaphoreType.DMA(()),
            pltpu.VMEM((SLAB_ROWS * TILE_STRIDE, LANES), I32)]


def _peer_in(e_t, u_tab, h2, gate_t):
    nb, _, tm = e_t.shape
    pairs = pl.BlockSpec((1, PEER_PAIRS, tm), lambda i: (i, 0, 0))
    return pl.pallas_call(
        _peer_in_kernel,
        grid=(nb,),
        in_specs=[pl.BlockSpec(memory_space=pl.ANY), pl.BlockSpec(memory_space=pltpu.VMEM),
                  pl.BlockSpec((tm, FEAT_CHUNKS, LANES), lambda i: (i, 0, 0)), pairs],
        out_specs=pairs,
        out_shape=jax.ShapeDtypeStruct((nb, PEER_PAIRS, tm), F32),
        scratch_shapes=_peer_scratch(tm),
        compiler_params=_cparams(("arbitrary",)),
        name="peer_expert_in",
    )(e_t, u_tab, h2, gate_t)


def _peer_out(e_t, v_tab, h2, w_t, g, b):
    nb, _, tm = e_t.shape
    pairs = pl.BlockSpec((1, PEER_PAIRS, tm), lambda i: (i, 0, 0))
    row = pl.BlockSpec((tm, FEAT_CHUNKS, LANES), lambda i: (i, 0, 0))
    vec = pl.BlockSpec((FEAT_CHUNKS, LANES), lambda i: (0, 0))
    return pl.pallas_call(
        _peer_out_kernel,
        grid=(nb,),
        in_specs=[pl.BlockSpec(memory_space=pl.ANY), pl.BlockSpec(memory_space=pltpu.VMEM), row, pairs, vec, vec],
        out_specs=row,
        out_shape=jax.ShapeDtypeStruct((nb * tm, FEAT_CHUNKS, LANES), F32),
        scratch_shapes=_peer_scratch(tm) + [pltpu.VMEM((tm, FEAT_CHUNKS, LANES), F32)],
        compiler_params=_cparams(("arbitrary",)),
        name="peer_expert_out_ln3",
    )(e_t, v_tab, h2, w_t, g, b)


def _lane_row(vals, offset):
    return jnp.zeros((1, LANES), F32).at[0, offset:offset + vals.shape[0]].set(vals)


def _finish(x2d, ssd2d, att2d, mem_attend_fn, p):
    h1, qm = _out_proj(x2d, ssd2d, att2d, p["w_out"], p["ln1_g"], p["ln1_b"], p["w_cq"])
    o = mem_attend_fn(qm)
    h2, e_t, gate_t = _peer_route(h1, o, p["w_co"], p["ln2_g"], p["ln2_b"], p["peer_wq"], p["subkeys"])
    h2_chunks = h2.reshape(-1, FEAT_CHUNKS, LANES)
    w_t = _peer_in(e_t, p["u_tab"], h2_chunks, gate_t)
    return _peer_out(e_t, p["v_tab"], h2_chunks, w_t, p["ln3_g"], p["ln3_b"]).reshape(-1, D_MODEL)


def kernel(x_prompt, mem_prompt, x_sample, cache_k, cache_v, cache_kidx, cache_mem_k, cache_mem_v, state_ssm, state_conv, page_table, w_in, conv_w, conv_b, dt_bias, a_log, d_skip, ssd_norm_w, rel_bias, w_out, ln1_g, ln1_b, w_cq, w_ck, w_cv, w_co, ln2_g, ln2_b, peer_wq, peer_subkeys, peer_u, peer_v, ln3_g, ln3_b):
    bp, seq, _ = x_prompt.shape
    bs, t_s, _ = x_sample.shape
    m_tok = mem_prompt.shape[1]

    c0 = SSD_INNER + CONV_DIM
    wa = w_in[:, :c0].astype(MXU_DTYPE)
    c1 = c0 + SSD_HEADS
    wb = w_in[:, c1:c1 + 4 * ATT_INNER].astype(MXU_DTYPE)
    c2 = c1 + 4 * ATT_INNER
    w_kidx = w_in[:, c2:c2 + IDX_DIM]
    wc = jnp.concatenate([w_kidx, w_kidx], axis=1).astype(MXU_DTYPE)
    wd = jnp.concatenate([w_kidx, w_in[:, c0:c1], w_in[:, c2 + IDX_DIM:],
                          jnp.zeros((D_MODEL, LANES - IDX_DIM - SSD_HEADS - IDX_HEADS), F32)], axis=1).astype(MXU_DTYPE)
    row = lambda v: v.reshape(1, -1)
    p = dict(
        w_out=w_out.astype(MXU_DTYPE), ln1_g=row(ln1_g), ln1_b=row(ln1_b), w_cq=w_cq.astype(MXU_DTYPE),
        w_co=w_co.astype(MXU_DTYPE), ln2_g=row(ln2_g), ln2_b=row(ln2_b), peer_wq=peer_wq.astype(MXU_DTYPE),
        subkeys=peer_subkeys.reshape(PEER_HEADS * 2, PEER_KEYS, PEER_HALF).astype(MXU_DTYPE),
        u_tab=_pack_table(peer_u), v_tab=_pack_table(peer_v),
        ln3_g=ln3_g.reshape(FEAT_CHUNKS, LANES), ln3_b=ln3_b.reshape(FEAT_CHUNKS, LANES))
    ssd_params = (conv_w, row(conv_b), _lane_row(a_log, DT_LANE), _lane_row(dt_bias, DT_LANE),
                  row(jnp.repeat(d_skip, SSD_HEAD_DIM)), row(ssd_norm_w))
    state_t = lambda s: jnp.swapaxes(s, -1, -2)

    n_p = bp * seq
    z, xbc, q_b, k_p, v_p, k_b, v_b, qi_b, kk_b, small = _project_in(x_prompt.reshape(n_p, D_MODEL), wa, wb, wc, wd)
    r3 = lambda a, b_: a.reshape(b_, -1, a.shape[-1])
    ssd_out, ssm_t, conv_p = _ssd_mixer(
        r3(z, bp), r3(xbc, bp), r3(small, bp), jnp.zeros((bp, SSD_CONV - 1, CONV_DIM), F32),
        jnp.zeros((bp, SSD_HEADS, SSD_STATE, SSD_HEAD_DIM), F32), SSD_CHUNK, *ssd_params)
    att_out = _dsa_prompt(r3(q_b, bp), r3(qi_b, bp), r3(small, bp), r3(kk_b, bp), r3(k_b, bp), r3(v_b, bp), rel_bias)
    kidx_p = r3(small, bp)[:, :, :IDX_DIM]
    mem_kv = _matmul(mem_prompt.reshape(bp * m_tok, D_MODEL),
                     jnp.concatenate([w_ck, w_cv], axis=1).astype(MXU_DTYPE))
    memk_p = mem_kv[:, :MEM_INNER].reshape(bp, m_tok, MEM_INNER)
    memv_p = mem_kv[:, MEM_INNER:].reshape(bp, m_tok, MEM_INNER)
    y_prompt = _finish(x_prompt.reshape(n_p, D_MODEL), ssd_out.reshape(n_p, SSD_INNER),
                       att_out.reshape(n_p, ATT_INNER),
                       lambda qm: _mem_attend(qm.reshape(bp, seq, MEM_INNER), memk_p, memv_p).reshape(n_p, MEM_INNER),
                       p)

    n_s = bs * t_s
    n_pad = -(-n_s // TOK_BLOCK) * TOK_BLOCK
    xs2d = jnp.pad(x_sample.reshape(n_s, D_MODEL), ((0, n_pad - n_s), (0, 0)))
    outs = _project_in(xs2d, wa, wb, wc, wd)
    z, xbc, q_b, k_s, v_s, k_b, v_b, qi_b, kk_b, small = [r3(a[:n_s], bs) for a in outs]
    pad_chunk = lambda a: jnp.pad(a, ((0, 0), (0, SSD_CHUNK - t_s), (0, 0)))
    ssd_s, ssm_s_t, conv_s = _ssd_mixer(pad_chunk(z), pad_chunk(xbc), pad_chunk(small), state_conv,
                                        state_t(state_ssm), t_s, *ssd_params)
    att_s = _dsa_sample(q_b, qi_b, small, k_b, v_b, kk_b, cache_k, cache_v, cache_kidx, page_table, rel_bias)
    pad_rows = lambda a: jnp.pad(a.reshape(n_s, -1), ((0, n_pad - n_s), (0, 0)))
    cmk = cache_mem_k.reshape(bs, m_tok, MEM_INNER)
    cmv = cache_mem_v.reshape(bs, m_tok, MEM_INNER)
    y_s = _finish(xs2d, pad_rows(ssd_s[:, :t_s]), pad_rows(att_s),
                  lambda qm: pad_rows(_mem_attend(qm[:n_s].reshape(bs, t_s, MEM_INNER), cmk, cmv)), p)
    y_sample = y_s[:n_s].reshape(bs, t_s, D_MODEL)

    heads = lambda a, b_: a.reshape(b_, -1, ATT_HEADS, ATT_HEAD_DIM)
    mem_heads = lambda a: a.reshape(bp, m_tok, MEM_HEADS, MEM_HEAD_DIM)
    return (y_prompt.reshape(bp, seq, D_MODEL), y_sample,
            heads(k_p, bp), heads(v_p, bp), kidx_p, mem_heads(memk_p), mem_heads(memv_p),
            state_t(ssm_t), conv_p,
            heads(k_s, bs), heads(v_s, bs), small[:, :, :IDX_DIM], state_t(ssm_s_t), conv_s)
```

```python
import functools
import math

import jax
import jax.numpy as jnp
import numpy as np
from jax import lax
from jax.experimental import pallas as pl
from jax.experimental.pallas import tpu as pltpu

F32 = jnp.float32
BF16 = jnp.bfloat16
I32 = jnp.int32
MXU_DTYPE = BF16

D_MODEL = 1024
DEPTH = 1
ALPHA = (2.0 * DEPTH) ** 0.25
LN_EPS = 1e-5

SSD_HEADS = 8
SSD_HEAD_DIM = 64
SSD_INNER = 512
SSD_GROUPS = 2
SSD_STATE = 128
SSD_CONV = 4
SSD_CHUNK = 128
CONV_DIM = 1024

ATT_HEADS = 8
ATT_HEAD_DIM = 64
ATT_INNER = 512
IDX_HEADS = 8
IDX_DIM = 64
TOPK_MAX = 256
PAGE_SIZE = 128
N_BUCKETS = 32
MAX_DISTANCE = 128

MEM_HEADS = 4
MEM_HEAD_DIM = 128
MEM_INNER = 512

PEER_HEADS = 8
PEER_KEYS = 128
PEER_HALF = 128
PEER_TOPK = 16
PEER_PAIRS = PEER_HEADS * PEER_TOPK

LANES = 128
CHUNK = 2 * LANES
TOK_BLOCK = 128
DT_LANE = 64
WIDX_LANE = 72
NEG_BIG = -1e30
INT_MIN = -(2 ** 31)
VMEM_LIMIT = 56 * 1024 * 1024
SLAB_ROWS = 4
FEAT_CHUNKS = D_MODEL // LANES
TILE_STRIDE = 136
GATHER_TOKENS = 4


def _cparams(sem, vmem=VMEM_LIMIT):
    return pltpu.CompilerParams(dimension_semantics=sem, vmem_limit_bytes=vmem)


def _mm(a, b):
    return jnp.dot(a.astype(MXU_DTYPE), b.astype(MXU_DTYPE), preferred_element_type=F32)


def _mm_nt(a, b):
    return lax.dot_general(a.astype(MXU_DTYPE), b.astype(MXU_DTYPE), (((1,), (1,)), ((), ())),
                           preferred_element_type=F32)


def _mm_exact(a, b):
    return jnp.dot(a, b, preferred_element_type=F32, precision=lax.Precision.HIGHEST)


def _sigmoid(x):
    return 1.0 / (1.0 + jnp.exp(-x))


def _layer_norm(x, g, b):
    mu = jnp.mean(x, axis=-1, keepdims=True)
    xc = x - mu
    var = jnp.mean(xc * xc, axis=-1, keepdims=True)
    return xc * lax.rsqrt(var + LN_EPS) * g + b


def _bucket_thresholds():
    max_exact = N_BUCKETS // 2
    out = []
    for n in range(max_exact, MAX_DISTANCE + 1):
        v = max_exact + int(math.log(n / max_exact) / math.log(MAX_DISTANCE / max_exact) * (N_BUCKETS - max_exact))
        out.append(min(v, N_BUCKETS - 1))
    thr = []
    for bkt in range(max_exact + 1, N_BUCKETS):
        thr.append(max_exact + next(i for i, v in enumerate(out) if v >= bkt))
    return tuple(thr)


_BUCKET_THR = _bucket_thresholds()


def _rel_bucket(dist):
    n = jnp.maximum(dist, 0)
    max_exact = N_BUCKETS // 2
    large = jnp.full(n.shape, max_exact, I32)
    for t in _BUCKET_THR:
        large = large + (n >= t).astype(I32)
    return jnp.where(n < max_exact, n, large)


def _monotone_key(x):
    b = lax.bitcast_convert_type(x, I32)
    return b ^ ((b >> 31) & 0x7FFFFFFF)


def _proj_in_kernel(x_ref, wa_ref, wb_ref, wc_ref, wd_ref,
                    z_ref, xbc_ref, q_ref, k_ref, v_ref, kb_ref, vb_ref, qi_ref, kk_ref, sm_ref):
    xb = x_ref[...].astype(MXU_DTYPE)
    z_ref[...] = jnp.dot(xb, wa_ref[:, :SSD_INNER], preferred_element_type=F32)
    xbc_ref[...] = jnp.dot(xb, wa_ref[:, SSD_INNER:], preferred_element_type=F32)
    q = jnp.dot(xb, wb_ref[:, 0:512], preferred_element_type=F32)
    q_ref[...] = (q * (ATT_HEAD_DIM ** -0.5)).astype(q_ref.dtype)
    k = jnp.dot(xb, wb_ref[:, 512:1024], preferred_element_type=F32)
    k_ref[...] = k
    kb_ref[...] = k.astype(kb_ref.dtype)
    v = jnp.dot(xb, wb_ref[:, 1024:1536], preferred_element_type=F32)
    v_ref[...] = v
    vb_ref[...] = v.astype(vb_ref.dtype)
    qi = jnp.dot(xb, wb_ref[:, 1536:2048], preferred_element_type=F32)
    qi_ref[...] = (qi * (IDX_DIM ** -0.5)).astype(qi_ref.dtype)
    kk_ref[...] = jnp.dot(xb, wc_ref[...], preferred_element_type=F32).astype(kk_ref.dtype)
    sm = jnp.dot(xb, wd_ref[...], preferred_element_type=F32)
    lane = lax.broadcasted_iota(I32, sm.shape, 1)
    is_w = (lane >= WIDX_LANE) & (lane < WIDX_LANE + IDX_HEADS)
    sm_ref[...] = jnp.where(is_w, sm * (IDX_HEADS ** -0.5), sm)


def _project_in(x2d, wa, wb, wc, wd):
    n = x2d.shape[0]
    tm = min(256, n)
    row = lambda w: pl.BlockSpec((tm, w), lambda i: (i, 0))
    full = lambda a: pl.BlockSpec(a.shape, lambda i: (0, 0))
    out_shapes = (
        jax.ShapeDtypeStruct((n, SSD_INNER), F32),
        jax.ShapeDtypeStruct((n, CONV_DIM), F32),
        jax.ShapeDtypeStruct((n, ATT_INNER), MXU_DTYPE),
        jax.ShapeDtypeStruct((n, ATT_INNER), F32),
        jax.ShapeDtypeStruct((n, ATT_INNER), F32),
        jax.ShapeDtypeStruct((n, ATT_INNER), MXU_DTYPE),
        jax.ShapeDtypeStruct((n, ATT_INNER), MXU_DTYPE),
        jax.ShapeDtypeStruct((n, ATT_INNER), MXU_DTYPE),
        jax.ShapeDtypeStruct((n, LANES), MXU_DTYPE),
        jax.ShapeDtypeStruct((n, LANES), F32),
    )
    return pl.pallas_call(
        _proj_in_kernel,
        grid=(n // tm,),
        in_specs=[row(D_MODEL), full(wa), full(wb), full(wc), full(wd)],
        out_specs=tuple(row(s.shape[1]) for s in out_shapes),
        out_shape=out_shapes,
        compiler_params=_cparams(("parallel",)),
        name="project_in",
    )(x2d, wa, wb, wc, wd)


def _ssd_kernel(n_valid, z_ref, xbc_ref, sm_ref, conv0_ref, h0_ref, convw_ref, convb_ref, alog_ref, dtb_ref,
                dskip_ref, normw_ref, y_ref, hout_ref, tail_ref, cbuf, hst, ysc):
    q_len = z_ref.shape[1]
    c = pl.program_id(1)
    pad = 8

    @pl.when(c == 0)
    def _():
        cbuf[pad - 3:pad, :] = conv0_ref[0]
        hst[...] = h0_ref[0]

    xbc = xbc_ref[0]
    cbuf[pad:pad + q_len, :] = xbc
    w = convw_ref[...]
    conv = cbuf[pad - 3:pad - 3 + q_len, :] * w[0:1]
    conv = conv + cbuf[pad - 2:pad - 2 + q_len, :] * w[1:2]
    conv = conv + cbuf[pad - 1:pad - 1 + q_len, :] * w[2:3]
    conv = conv + xbc * w[3:4]
    conv = conv + convb_ref[...]
    xc = conv * _sigmoid(conv)
    tail = cbuf[pad + n_valid - 3:pad + n_valid, :]
    cbuf[pad - 3:pad, :] = tail

    row = lax.broadcasted_iota(I32, (q_len, LANES), 0)
    lane = lax.broadcasted_iota(I32, (q_len, LANES), 1)
    is_dt = (lane >= DT_LANE) & (lane < DT_LANE + SSD_HEADS) & (row < n_valid)
    raw = sm_ref[0] + dtb_ref[...]
    dt_full = jnp.maximum(raw, 0.0) + jnp.log1p(jnp.exp(-jnp.abs(raw)))
    dtm = jnp.where(is_dt, dt_full, 0.0)
    a_full = dtm * (-jnp.exp(alog_ref[...]))
    ri = lax.broadcasted_iota(I32, (q_len, q_len), 0)
    ci = lax.broadcasted_iota(I32, (q_len, q_len), 1)
    causal = ci <= ri
    acum = _mm_exact(causal.astype(F32), a_full)
    acum_t = acum.T

    xs = xc[:, :SSD_INNER]
    for g in range(SSD_GROUPS):
        bm = xc[:, SSD_INNER + g * SSD_STATE:SSD_INNER + (g + 1) * SSD_STATE]
        cm = xc[:, SSD_INNER + (SSD_GROUPS + g) * SSD_STATE:SSD_INNER + (SSD_GROUPS + g + 1) * SSD_STATE]
        scores = _mm_nt(cm, bm)
        bm_t = bm.T
        for hh in range(SSD_HEADS // SSD_GROUPS):
            h = g * (SSD_HEADS // SSD_GROUPS) + hh
            col = acum[:, DT_LANE + h:DT_LANE + h + 1]
            rowv = acum_t[DT_LANE + h:DT_LANE + h + 1, :]
            a_last = acum[q_len - 1:q_len, DT_LANE + h:DT_LANE + h + 1]
            decay = jnp.exp(jnp.where(causal, col - rowv, -jnp.inf))
            x_h = xs[:, h * SSD_HEAD_DIM:(h + 1) * SSD_HEAD_DIM]
            xd = x_h * dtm[:, DT_LANE + h:DT_LANE + h + 1]
            h_t = hst[h]
            y = _mm(scores * decay, xd) + _mm(cm, h_t) * jnp.exp(col)
            y = y + x_h * dskip_ref[:, h * SSD_HEAD_DIM:(h + 1) * SSD_HEAD_DIM]
            ysc[:, h * SSD_HEAD_DIM:(h + 1) * SSD_HEAD_DIM] = y
            hst[h] = h_t * jnp.exp(a_last) + _mm(bm_t, xd * jnp.exp(a_last - col))

    zz = z_ref[0]
    hg = ysc[...] * (zz * _sigmoid(zz))
    hg = hg * lax.rsqrt(jnp.mean(hg * hg, axis=-1, keepdims=True) + LN_EPS)
    y_ref[0] = hg * normw_ref[...]

    @pl.when(c == pl.num_programs(1) - 1)
    def _():
        hout_ref[0] = hst[...]
        tail_ref[0] = cbuf[pad - 3:pad, :]


def _ssd_mixer(z, xbc, small, conv0, h0_t, n_valid, conv_w, conv_b, alog_lane, dtb_lane, dskip_row, norm_w):
    b, length, _ = z.shape
    q_len = SSD_CHUNK
    nc = length // q_len
    blk = lambda w: pl.BlockSpec((1, q_len, w), lambda i, c: (i, c, 0))
    per_b3 = lambda s: pl.BlockSpec((1,) + s, lambda i, c: (i,) + (0,) * len(s))
    const2 = lambda a: pl.BlockSpec(a.shape, lambda i, c: (0, 0))
    st_shape = (SSD_HEADS, SSD_STATE, SSD_HEAD_DIM)
    return pl.pallas_call(
        functools.partial(_ssd_kernel, n_valid),
        grid=(b, nc),
        in_specs=[blk(SSD_INNER), blk(CONV_DIM), blk(LANES), per_b3((SSD_CONV - 1, CONV_DIM)), per_b3(st_shape),
                  const2(conv_w), const2(conv_b), const2(alog_lane), const2(dtb_lane), const2(dskip_row),
                  const2(norm_w)],
        out_specs=(blk(SSD_INNER), per_b3(st_shape), per_b3((SSD_CONV - 1, CONV_DIM))),
        out_shape=(jax.ShapeDtypeStruct((b, length, SSD_INNER), F32),
                   jax.ShapeDtypeStruct((b,) + st_shape, F32),
                   jax.ShapeDtypeStruct((b, SSD_CONV - 1, CONV_DIM), F32)),
        scratch_shapes=[pltpu.VMEM((q_len + 8, CONV_DIM), F32), pltpu.VMEM(st_shape, F32),
                        pltpu.VMEM((q_len, SSD_INNER), F32)],
        compiler_params=_cparams(("parallel", "arbitrary")),
        name="ssd_mixer",
    )(z, xbc, small, conv0, h0_t, conv_w, conv_b, alog_lane, dtb_lane, dskip_row, norm_w)


def _kth_largest_key(count_ge, k, rows):
    def body(i, tau):
        cand = tau + jnp.left_shift(jnp.int32(1), 31 - i)
        return jnp.where(count_ge(cand) >= k, cand, tau)
    return lax.fori_loop(0, 32, body, jnp.full((rows, 1), INT_MIN, I32))


def _bias_tiles(rb_ref, bias_sc, n_tiles):
    a = lax.broadcasted_iota(I32, (LANES, LANES), 0)
    bcol = lax.broadcasted_iota(I32, (LANES, LANES), 1)
    for d in range(n_tiles):
        bucket = _rel_bucket(d * LANES + a - bcol)
        for h in range(ATT_HEADS):
            far = rb_ref[N_BUCKETS - 1, h]
            tile = jnp.zeros((LANES, LANES), F32)
            for kb in range(N_BUCKETS):
                tile = jnp.where(bucket == kb, rb_ref[kb, h] - far, tile)
            bias_sc[d, h] = tile


def _dsa_prompt_kernel(topk, q_ref, qi_ref, sm_ref, kk_ref, kb_ref, vb_ref, rb_ref, o_ref,
                       keys_sc, bias_sc, qm_sc, qim_sc, wb_sc, *head_sc):
    nh = ATT_HEADS
    cache, m_l, l_l, acc = head_sc[:nh], head_sc[nh:2 * nh], head_sc[2 * nh:3 * nh], head_sc[3 * nh:]
    i = pl.program_id(1)
    tq = q_ref.shape[1]
    n_far = bias_sc.shape[0] - 1
    n_chunks = (i + 2) // 2

    @pl.when(i == 0)
    def _():
        _bias_tiles(rb_ref, bias_sc, n_far + 1)

    lane = lax.broadcasted_iota(I32, (tq, LANES), 1)
    low_half = lane < ATT_HEAD_DIM
    lane2 = lax.broadcasted_iota(I32, (tq, CHUNK), 1)
    row2 = lax.broadcasted_iota(I32, (tq, CHUNK), 0)
    sm = sm_ref[0]

    for h in range(nh):
        keep = low_half if h % 2 == 0 else jnp.logical_not(low_half)
        pair = slice((h // 2) * LANES, (h // 2 + 1) * LANES)
        qp, qip = q_ref[0, :, pair], qi_ref[0, :, pair]
        qm_sc[h] = jnp.where(keep, qp, jnp.zeros_like(qp))
        qim_sc[h] = jnp.where(keep, qip, jnp.zeros_like(qip))
        wb_sc[h] = jnp.broadcast_to(sm[:, WIDX_LANE + h:WIDX_LANE + h + 1], (tq, LANES))
        m_l[h][...] = jnp.full((tq, LANES), NEG_BIG, F32)
        l_l[h][...] = jnp.zeros((tq, LANES), F32)
        acc[h][...] = jnp.zeros((tq, LANES), F32)

    def causal(c):
        return (c * CHUNK + lane2) <= (i * tq + row2)

    def chunk_logits(c, carry):
        start = pl.multiple_of(c * CHUNK, CHUNK)
        kk = kk_ref[0, pl.ds(start, CHUNK), :]
        sc = jnp.zeros((tq, CHUNK), F32)
        for h in range(IDX_HEADS):
            wb = wb_sc[h]
            sc = sc + jnp.maximum(_mm_nt(qim_sc[h], kk), 0.0) * jnp.concatenate([wb, wb], axis=1)
        keys_sc[c] = _monotone_key(jnp.where(causal(c), sc, -jnp.inf))
        d0 = jnp.clip(i - 2 * c, 0, n_far)
        d1 = jnp.clip(i - 2 * c - 1, 0, n_far)
        for h in range(nh):
            kblk = kb_ref[0, pl.ds(start, CHUNK), (h // 2) * LANES:(h // 2 + 1) * LANES]
            bias = jnp.concatenate([bias_sc[d0, h], bias_sc[d1, h]], axis=1)
            cache[h][c] = _mm_nt(qm_sc[h], kblk) + bias
        return carry

    lax.fori_loop(0, n_chunks, chunk_logits, 0)

    def count_ge(cand):
        def body(c, cnt):
            k = keys_sc[c]
            return cnt + jnp.where(k[:, :LANES] >= cand, 1.0, 0.0) + jnp.where(k[:, LANES:] >= cand, 1.0, 0.0)
        cnt = lax.fori_loop(0, n_chunks, body, jnp.zeros((tq, LANES), F32))
        return jnp.sum(cnt, axis=1, keepdims=True)

    tau = _kth_largest_key(count_ge, float(topk), tq)

    def chunk_max(c, carry):
        sel = (keys_sc[c] >= tau) & causal(c)
        for h in range(nh):
            s = jnp.where(sel, cache[h][c], NEG_BIG)
            cache[h][c] = s
            m_l[h][...] = jnp.maximum(m_l[h][...], jnp.maximum(s[:, :LANES], s[:, LANES:]))
        return carry

    lax.fori_loop(0, n_chunks, chunk_max, 0)
    for h in range(nh):
        m_l[h][...] = jnp.broadcast_to(jnp.max(m_l[h][...], axis=1, keepdims=True), (tq, LANES))

    def chunk_pv(c, carry):
        start = pl.multiple_of(c * CHUNK, CHUNK)
        for h in range(nh):
            mb = m_l[h][...]
            s = cache[h][c]
            p0 = jnp.exp(s[:, :LANES] - mb)
            p1 = jnp.exp(s[:, LANES:] - mb)
            l_l[h][...] = l_l[h][...] + (p0 + p1)
            vblk = vb_ref[0, pl.ds(start, CHUNK), (h // 2) * LANES:(h // 2 + 1) * LANES]
            acc[h][...] = acc[h][...] + _mm(jnp.concatenate([p0, p1], axis=1), vblk)
        return carry

    lax.fori_loop(0, n_chunks, chunk_pv, 0)

    for g in range(nh // 2):
        lo = acc[2 * g][...] / jnp.sum(l_l[2 * g][...], axis=1, keepdims=True)
        hi = acc[2 * g + 1][...] / jnp.sum(l_l[2 * g + 1][...], axis=1, keepdims=True)
        o_ref[0, :, g * LANES:(g + 1) * LANES] = jnp.where(low_half, lo, hi)


def _dsa_prompt(q_b, qi_b, small, kk_b, k_b, v_b, rel_bias):
    b, s, _ = q_b.shape
    tq = LANES
    nq = s // tq
    topk = min(TOPK_MAX, s // 4)
    n_bias_tiles = 3
    assert MAX_DISTANCE <= LANES + 1 and s % CHUNK == 0
    n_chunks = s // CHUNK
    qblk = lambda w: pl.BlockSpec((1, tq, w), lambda i, j: (i, j, 0))
    per_b = lambda w: pl.BlockSpec((1, s, w), lambda i, j: (i, 0, 0))
    per_head = lambda shape, dt: [pltpu.VMEM(shape, dt) for _ in range(ATT_HEADS)]
    return pl.pallas_call(
        functools.partial(_dsa_prompt_kernel, topk),
        grid=(b, nq),
        in_specs=[qblk(ATT_INNER), qblk(ATT_INNER), qblk(LANES), per_b(LANES), per_b(ATT_INNER), per_b(ATT_INNER),
                  pl.BlockSpec(memory_space=pltpu.SMEM)],
        out_specs=qblk(ATT_INNER),
        out_shape=jax.ShapeDtypeStruct((b, s, ATT_INNER), F32),
        scratch_shapes=[pltpu.VMEM((n_chunks, tq, CHUNK), I32),
                        pltpu.VMEM((n_bias_tiles, ATT_HEADS, LANES, LANES), F32),
                        pltpu.VMEM((ATT_HEADS, tq, LANES), q_b.dtype), pltpu.VMEM((ATT_HEADS, tq, LANES), qi_b.dtype),
                        pltpu.VMEM((ATT_HEADS, tq, LANES), F32)]
                       + per_head((n_chunks, tq, CHUNK), F32)
                       + per_head((tq, LANES), F32)
                       + per_head((tq, LANES), F32)
                       + per_head((tq, LANES), F32),
        compiler_params=_cparams(("parallel", "arbitrary")),
        name="dsa_prompt",
    )(q_b, qi_b, small, kk_b, k_b, v_b, rel_bias)


def _dsa_sample_score_kernel(topk, t_new, pt_ref, qi_ref, w_ref, *refs):
    kidx_refs, (kknew_ref, keys_ref, keysnew_ref, tau_ref, keys_sc) = refs[:-5], refs[-5:]
    pg = len(kidx_refs)
    j = pl.program_id(1)
    n_pages = pl.num_programs(1) * pg
    rows = qi_ref.shape[1]
    lane = lax.broadcasted_iota(I32, (t_new, LANES), 1)
    rowi = lax.broadcasted_iota(I32, (t_new, LANES), 0)

    def head_sum(s):
        s = jnp.maximum(s, 0.0) * w_ref[0]
        return jnp.concatenate(
            [jnp.sum(s[t * IDX_HEADS:(t + 1) * IDX_HEADS], axis=0, keepdims=True) for t in range(t_new)], axis=0)

    for r, kidx_ref in enumerate(kidx_refs):
        keys_sc[j * pg + r] = _monotone_key(head_sum(_mm_nt(qi_ref[0], kidx_ref[0])))

    @pl.when(j == pl.num_programs(1) - 1)
    def _():
        sc = head_sum(_mm_nt(qi_ref[0], kknew_ref[0]))
        keys_new = _monotone_key(jnp.where(lane <= rowi, sc, -jnp.inf))
        keys_sc[n_pages] = keys_new

        def count_ge(cand):
            def body(jj, cnt):
                return cnt + jnp.where(keys_sc[jj] >= cand, 1.0, 0.0)
            cnt = lax.fori_loop(0, n_pages + 1, body, jnp.zeros((t_new, LANES), F32))
            return jnp.sum(cnt, axis=1, keepdims=True)

        tau = _kth_largest_key(count_ge, float(topk), t_new)
        tau_ref[0] = jnp.broadcast_to(tau, (t_new, LANES))
        keys_ref[0] = keys_sc[0:n_pages]
        keysnew_ref[0] = keys_new


def _dsa_sample_attn_kernel(t_new, past, pg, pt_ref, q_ref, keys_ref, keysnew_ref, tau_ref, knew_ref, vnew_ref,
                            rbrow_ref, expand_ref, *refs):
    k_refs, v_refs = refs[:pg], refs[pg:2 * pg]
    o_ref, cache_sc, m_sc, l_sc, acc_sc = refs[2 * pg:]
    phase = pl.program_id(1)
    j = pl.program_id(2)
    n_steps = pl.num_programs(2)
    n_pages = n_steps * pg
    rows = q_ref.shape[1]
    wide = PAGE_SIZE * ATT_HEADS
    lane = lax.broadcasted_iota(I32, (rows, wide), 1)
    rowi = lax.broadcasted_iota(I32, (rows, wide), 0)
    tok = lax.rem(rowi, t_new)
    own_head = lax.rem(lane, ATT_HEADS) == rowi // t_new
    key_in_page = lane // ATT_HEADS
    q = q_ref[0]

    def rows_of(x_t):
        return jnp.concatenate([x_t] * ATT_HEADS, axis=0)

    tau = rows_of(tau_ref[0])

    def selected(key_blocks):
        bits = jnp.concatenate([jnp.where(rows_of(k) >= tau, 1.0, 0.0) for k in key_blocks], axis=0)
        wide_bits = _mm(bits, expand_ref[...])
        return [wide_bits[r * rows:(r + 1) * rows] > 0.5 for r in range(len(key_blocks))]

    def near_bias(key_pos):
        bucket = _rel_bucket(past + tok - key_pos)
        far = rbrow_ref[:, N_BUCKETS - 1:N_BUCKETS]
        bias = jnp.zeros((rows, wide), F32)
        for kb in range(N_BUCKETS - 1):
            bias = jnp.where(bucket == kb, rbrow_ref[:, kb:kb + 1] - far, bias)
        return bias

    @pl.when(phase == 0)
    def _():
        sel = selected([keys_ref[0, r] for r in range(pg)])
        for r in range(pg):
            cache_sc[j * pg + r] = jnp.where(sel[r] & own_head, _mm_nt(q, k_refs[r][0]), NEG_BIG)

        @pl.when(j == n_steps - 1)
        def _():
            last = n_pages - 1
            cache_sc[last] = cache_sc[last] + near_bias(last * PAGE_SIZE + key_in_page)
            s = _mm_nt(q, knew_ref[0]) + near_bias(past + key_in_page)
            ok = selected([keysnew_ref[0]])[0] & own_head & (key_in_page <= tok)
            cache_sc[n_pages] = jnp.where(ok, s, NEG_BIG)
            m = lax.fori_loop(0, n_pages + 1, lambda jj, m: jnp.maximum(m, cache_sc[jj]),
                              jnp.full((rows, wide), NEG_BIG, F32))
            m_sc[...] = jnp.broadcast_to(jnp.max(m, axis=1, keepdims=True), (rows, wide))
            l_sc[...] = jnp.zeros(l_sc.shape, F32)
            acc_sc[...] = jnp.zeros(acc_sc.shape, F32)

    @pl.when(phase == 1)
    def _():
        mb = m_sc[...]
        acc = acc_sc[...]
        lsum = l_sc[...]
        for r in range(pg):
            p = jnp.exp(cache_sc[j * pg + r] - mb)
            lsum = lsum + p
            acc = acc + _mm(p, v_refs[r][0])
        l_sc[...] = lsum
        acc_sc[...] = acc

        @pl.when(j == n_steps - 1)
        def _():
            p_new = jnp.exp(cache_sc[n_pages] - mb)
            l = jnp.sum(lsum + p_new, axis=1, keepdims=True)
            a = (acc + _mm(p_new, vnew_ref[0])) / l
            o_ref[0] = jnp.concatenate([a[h * t_new:(h + 1) * t_new] for h in range(ATT_HEADS)], axis=1)


def _dsa_sample(q_b, qi_b, small, k_b, v_b, kk_b, cache_k, cache_v, cache_kidx, page_table, rel_bias):
    b, t_new, _ = q_b.shape
    n_pages = page_table.shape[1]
    past = n_pages * PAGE_SIZE
    topk = min(TOPK_MAX, (past + t_new) // 4)
    rows = t_new * ATT_HEADS
    assert PAGE_SIZE >= MAX_DISTANCE

    def pages_per_step(limit):
        return max(g for g in (1, 2, 4, 8, 16) if g <= limit and n_pages % g == 0)

    pgs = pages_per_step(16)
    qi_rows = qi_b.reshape(b, rows, IDX_DIM)
    w_rows = small[:, :, WIDX_LANE:WIDX_LANE + IDX_HEADS].reshape(b, rows, 1)
    kk_new = jnp.pad(kk_b[:, :, :IDX_DIM], ((0, 0), (0, LANES - t_new), (0, 0)))
    per_b = lambda s: pl.BlockSpec((1,) + s, lambda i, j, pt: (i,) + (0,) * len(s))
    kidx_page = lambda r: pl.BlockSpec((1, PAGE_SIZE, IDX_DIM), lambda i, j, pt: (pt[i, j * pgs + r], 0, 0))
    keys, keys_new, tau = pl.pallas_call(
        functools.partial(_dsa_sample_score_kernel, topk, t_new),
        grid_spec=pltpu.PrefetchScalarGridSpec(
            num_scalar_prefetch=1, grid=(b, n_pages // pgs),
            in_specs=[per_b((rows, IDX_DIM)), per_b((rows, 1))] + [kidx_page(r) for r in range(pgs)]
                     + [per_b((LANES, IDX_DIM))],
            out_specs=(per_b((n_pages, t_new, LANES)), per_b((t_new, LANES)), per_b((t_new, LANES))),
            scratch_shapes=[pltpu.VMEM((n_pages + 1, t_new, LANES), I32)]),
        out_shape=(jax.ShapeDtypeStruct((b, n_pages, t_new, LANES), I32),
                   jax.ShapeDtypeStruct((b, t_new, LANES), I32),
                   jax.ShapeDtypeStruct((b, t_new, LANES), I32)),
        compiler_params=_cparams(("parallel", "arbitrary")),
        name="dsa_sample_scores",
    )(page_table, qi_rows, w_rows, *([cache_kidx] * pgs), kk_new)

    pg = pages_per_step(8)
    n_steps = n_pages // pg
    q_rows = q_b.reshape(b, t_new, ATT_HEADS, ATT_HEAD_DIM).swapaxes(1, 2).reshape(b, rows, ATT_HEAD_DIM)
    rb_rows = jnp.repeat(rel_bias.T, t_new, axis=0)
    wide = PAGE_SIZE * ATT_HEADS
    as_rows = lambda c: c.reshape(c.shape[0], wide, ATT_HEAD_DIM)
    new_page = lambda x: jnp.pad(x.reshape(b, t_new, ATT_HEADS, ATT_HEAD_DIM),
                                 ((0, 0), (0, PAGE_SIZE - t_new), (0, 0), (0, 0))).reshape(b, wide, ATT_HEAD_DIM)
    k_new, v_new = new_page(k_b), new_page(v_b)
    expand = jnp.repeat(jnp.eye(PAGE_SIZE, dtype=MXU_DTYPE), ATT_HEADS, axis=1)
    per_b = lambda s: pl.BlockSpec((1,) + s, lambda i, ph, j, pt: (i,) + (0,) * len(s))
    page_blk = (1, wide, ATT_HEAD_DIM)
    k_page = lambda r: pl.BlockSpec(
        page_blk, lambda i, ph, j, pt: (pt[i, jnp.where(ph == 0, j, n_steps - 1) * pg + r], 0, 0))
    v_page = lambda r: pl.BlockSpec(
        page_blk, lambda i, ph, j, pt: (pt[i, jnp.where(ph == 1, j, 0) * pg + r], 0, 0))
    return pl.pallas_call(
        functools.partial(_dsa_sample_attn_kernel, t_new, past, pg),
        grid_spec=pltpu.PrefetchScalarGridSpec(
            num_scalar_prefetch=1, grid=(b, 2, n_steps),
            in_specs=[per_b((rows, ATT_HEAD_DIM)),
                      pl.BlockSpec((1, pg, t_new, LANES),
                                   lambda i, ph, j, pt: (i, jnp.where(ph == 0, j, n_steps - 1), 0, 0)),
                      per_b((t_new, LANES)), per_b((t_new, LANES)),
                      per_b((wide, ATT_HEAD_DIM)), per_b((wide, ATT_HEAD_DIM)),
                      pl.BlockSpec((rows, N_BUCKETS), lambda i, ph, j, pt: (0, 0)),
                      pl.BlockSpec((PAGE_SIZE, wide), lambda i, ph, j, pt: (0, 0))]
                     + [k_page(r) for r in range(pg)] + [v_page(r) for r in range(pg)],
            out_specs=per_b((t_new, ATT_INNER)),
            scratch_shapes=[pltpu.VMEM((n_pages + 1, rows, wide), F32), pltpu.VMEM((rows, wide), F32),
                            pltpu.VMEM((rows, wide), F32), pltpu.VMEM((rows, ATT_HEAD_DIM), F32)]),
        out_shape=jax.ShapeDtypeStruct((b, t_new, ATT_INNER), F32),
        compiler_params=_cparams(("parallel", "arbitrary", "arbitrary")),
        name="dsa_sample_attend",
    )(page_table, q_rows, keys, keys_new, tau, k_new, v_new, rb_rows, expand,
      *([as_rows(cache_k)] * pg), *([as_rows(cache_v)] * pg))


def _matmul_kernel(x_ref, w_ref, o_ref):
    o_ref[...] = _mm(x_ref[...], w_ref[...])


def _matmul(x2d, w):
    n, kdim = x2d.shape
    tm = min(256, n)
    return pl.pallas_call(
        _matmul_kernel,
        grid=(n // tm,),
        in_specs=[pl.BlockSpec((tm, kdim), lambda i: (i, 0)), pl.BlockSpec(w.shape, lambda i: (0, 0))],
        out_specs=pl.BlockSpec((tm, w.shape[1]), lambda i: (i, 0)),
        out_shape=jax.ShapeDtypeStruct((n, w.shape[1]), F32),
        compiler_params=_cparams(("parallel",)),
        name="mem_kv_proj",
    )(x2d, w)


def _out_proj_kernel(x_ref, ssd_ref, att_ref, wout_ref, g_ref, b_ref, wcq_ref, h1_ref, qm_ref):
    m = _mm(ssd_ref[...], wout_ref[:SSD_INNER, :]) + _mm(att_ref[...], wout_ref[SSD_INNER:, :])
    h1 = _layer_norm(ALPHA * x_ref[...] + m, g_ref[...], b_ref[...])
    h1_ref[...] = h1
    qm_ref[...] = _mm(h1, wcq_ref[...])


def _out_proj(x2d, ssd2d, att2d, w_out, g, b, w_cq):
    n = x2d.shape[0]
    tm = min(256, n)
    row = lambda w: pl.BlockSpec((tm, w), lambda i: (i, 0))
    full = lambda a: pl.BlockSpec(a.shape, lambda i: (0, 0))
    return pl.pallas_call(
        _out_proj_kernel,
        grid=(n // tm,),
        in_specs=[row(D_MODEL), row(SSD_INNER), row(ATT_INNER), full(w_out), full(g), full(b), full(w_cq)],
        out_specs=(row(D_MODEL), row(MEM_INNER)),
        out_shape=(jax.ShapeDtypeStruct((n, D_MODEL), F32), jax.ShapeDtypeStruct((n, MEM_INNER), F32)),
        compiler_params=_cparams(("parallel",)),
        name="out_proj_ln1",
    )(x2d, ssd2d, att2d, w_out, g, b, w_cq)


def _mem_attend_kernel(q_ref, mk_ref, mv_ref, o_ref):
    q = q_ref[0]
    for h in range(MEM_HEADS):
        sl = slice(h * MEM_HEAD_DIM, (h + 1) * MEM_HEAD_DIM)
        logits = _mm_nt(q[:, sl], mk_ref[0, :, sl]) * (MEM_HEAD_DIM ** -0.5)
        logits = logits - jnp.max(logits, axis=-1, keepdims=True)
        p = jnp.exp(logits)
        p = p / jnp.sum(p, axis=-1, keepdims=True)
        o_ref[0, :, sl] = _mm(p, mv_ref[0, :, sl])


def _mem_attend(qm, mk, mv):
    b, length, _ = qm.shape
    tq = min(512, length)
    m_tok = mk.shape[1]
    return pl.pallas_call(
        _mem_attend_kernel,
        grid=(b, length // tq),
        in_specs=[pl.BlockSpec((1, tq, MEM_INNER), lambda i, j: (i, j, 0)),
                  pl.BlockSpec((1, m_tok, MEM_INNER), lambda i, j: (i, 0, 0)),
                  pl.BlockSpec((1, m_tok, MEM_INNER), lambda i, j: (i, 0, 0))],
        out_specs=pl.BlockSpec((1, tq, MEM_INNER), lambda i, j: (i, j, 0)),
        out_shape=jax.ShapeDtypeStruct((b, length, MEM_INNER), F32),
        compiler_params=_cparams(("parallel", "parallel")),
        name="mem_attend",
    )(qm, mk, mv)


def _top_rows(s, n_top, with_index):
    rows = s.shape[0]
    ridx = lax.broadcasted_iota(I32, s.shape, 0)
    vals, idxs = [], []
    for _ in range(n_top):
        m = jnp.max(s, axis=0, keepdims=True)
        first = jnp.min(jnp.where(s == m, ridx, rows), axis=0, keepdims=True)
        s = jnp.where(ridx == first, -jnp.inf, s)
        vals.append(m)
        idxs.append(first)
    return jnp.concatenate(vals, axis=0), (jnp.concatenate(idxs, axis=0) if with_index else None)


def _peer_route_kernel(h1_ref, o_ref, wco_ref, g_ref, b_ref, wq_ref, sk_ref, h2_ref, e_ref, gate_ref):
    h2 = _layer_norm(ALPHA * h1_ref[...] + _mm(o_ref[...], wco_ref[...]), g_ref[...], b_ref[...])
    h2_ref[...] = h2
    h2b = h2.astype(MXU_DTYPE)
    for h in range(PEER_HEADS):
        tops = []
        for p in range(2):
            c = h * 2 + p
            qs = jnp.dot(h2b, wq_ref[:, c * PEER_HALF:(c + 1) * PEER_HALF], preferred_element_type=F32)
            s_t = _mm_nt(sk_ref[c], qs)
            tops.append(_top_rows(s_t, PEER_TOPK, True))
        (v0, i0), (v1, i1) = tops
        cand = jnp.concatenate([v0[a:a + 1] + v1 for a in range(PEER_TOPK)], axis=0)
        cidx = jnp.concatenate([i0[a:a + 1] * PEER_KEYS + i1 for a in range(PEER_TOPK)], axis=0)
        pos = lax.broadcasted_iota(I32, cand.shape, 0)
        gs, es = [], []
        for _ in range(PEER_TOPK):
            m = jnp.max(cand, axis=0, keepdims=True)
            first = jnp.min(jnp.where(cand == m, pos, cand.shape[0]), axis=0, keepdims=True)
            hit = pos == first
            es.append(jnp.max(jnp.where(hit, cidx, -1), axis=0, keepdims=True))
            cand = jnp.where(hit, -jnp.inf, cand)
            gs.append(m)
        g_s = jnp.concatenate(gs, axis=0)
        ex = jnp.exp(g_s - g_s[0:1])
        gate_ref[0, h * PEER_TOPK:(h + 1) * PEER_TOPK, :] = ex / jnp.sum(ex, axis=0, keepdims=True)
        e_ref[0, h * PEER_TOPK:(h + 1) * PEER_TOPK, :] = jnp.concatenate(es, axis=0) * SLAB_ROWS


def _peer_route(h1, o, w_co, g, b, wq, subkeys):
    n = h1.shape[0]
    tm = TOK_BLOCK
    nb = n // tm
    row = lambda w: pl.BlockSpec((tm, w), lambda i: (i, 0))
    full = lambda a: pl.BlockSpec(a.shape, lambda i: (0,) * a.ndim)
    pairs = pl.BlockSpec((1, PEER_PAIRS, tm), lambda i: (i, 0, 0))
    return pl.pallas_call(
        _peer_route_kernel,
        grid=(nb,),
        in_specs=[row(D_MODEL), row(MEM_INNER), full(w_co), full(g), full(b), full(wq), full(subkeys)],
        out_specs=(row(D_MODEL), pairs, pairs),
        out_shape=(jax.ShapeDtypeStruct((n, D_MODEL), F32),
                   jax.ShapeDtypeStruct((nb, PEER_PAIRS, tm), I32),
                   jax.ShapeDtypeStruct((nb, PEER_PAIRS, tm), F32)),
        compiler_params=_cparams(("parallel",)),
        name="ln2_peer_route",
    )(h1, o, w_co, g, b, wq, subkeys)


def _pack_table(t):
    tb = lax.bitcast_convert_type(t.astype(BF16), jnp.uint16).astype(jnp.uint32)
    half = t.shape[1] // 2
    words = tb[:, :half] | (tb[:, half:] << 16)
    return lax.bitcast_convert_type(words, I32).reshape(t.shape[0] * SLAB_ROWS, LANES)


def _unpack_words(w):
    lo = lax.bitcast_convert_type(w << 16, F32)
    hi = lax.bitcast_convert_type(w & jnp.int32(-65536), F32)
    return lo, hi


def _gather_token_rows(idx_smem, tab_ref, tile_ref, t):
    for p in range(PEER_PAIRS):
        row = pl.multiple_of(idx_smem[t, p], SLAB_ROWS)
        tile_ref[pl.ds(p, SLAB_ROWS, stride=TILE_STRIDE), :] = tab_ref[pl.ds(row, SLAB_ROWS), :]


def _load_indices(e_hbm, idx_smem, sem):
    cp = pltpu.make_async_copy(e_hbm.at[pl.program_id(0)], idx_smem, sem)
    cp.start()
    cp.wait()


def _gelu_tanh(x):
    return 0.5 * x * (1.0 + jnp.tanh(math.sqrt(2.0 / math.pi) * (x + 0.044715 * (x * x * x))))


def _peer_in_kernel(e_hbm, tab_ref, h2_ref, gate_ref, w_ref, idx_smem, sem, *tiles):
    _load_indices(e_hbm, idx_smem, sem)
    tm = h2_ref.shape[0]
    lane = lax.broadcasted_iota(I32, (PEER_PAIRS, tm), 1)

    def tok_group(g, act_t):
        cols = []
        for u, tile_ref in enumerate(tiles):
            t = g * len(tiles) + u
            _gather_token_rows(idx_smem, tab_ref, tile_ref, t)
            x_t = h2_ref[t]
            acc = jnp.zeros((PEER_PAIRS, LANES), F32)
            for j in range(SLAB_ROWS):
                lo, hi = _unpack_words(tile_ref[j * TILE_STRIDE:j * TILE_STRIDE + PEER_PAIRS, :])
                acc = acc + lo * x_t[j:j + 1, :] + hi * x_t[SLAB_ROWS + j:SLAB_ROWS + j + 1, :]
            cols.append(jnp.sum(acc, axis=1, keepdims=True))
        for u, col in enumerate(cols):
            act_t = jnp.where(lane == g * len(tiles) + u, col, act_t)
        return act_t

    act_t = lax.fori_loop(0, tm // len(tiles), tok_group, jnp.zeros((PEER_PAIRS, tm), F32))
    w_ref[0] = gate_ref[0] * _gelu_tanh(act_t)


def _peer_out_kernel(e_hbm, tab_ref, h2_ref, w_ref, g_ref, b_ref, y_ref, idx_smem, sem, out_sc, *tiles):
    _load_indices(e_hbm, idx_smem, sem)
    tm = h2_ref.shape[0]
    lane = lax.broadcasted_iota(I32, (PEER_PAIRS, tm), 1)

    def tok_group(g, carry):
        for u, tile_ref in enumerate(tiles):
            t = g * len(tiles) + u
            _gather_token_rows(idx_smem, tab_ref, tile_ref, t)
            w_col = jnp.sum(jnp.where(lane == t, w_ref[0], 0.0), axis=1, keepdims=True)
            los, his = [], []
            for j in range(SLAB_ROWS):
                lo, hi = _unpack_words(tile_ref[j * TILE_STRIDE:j * TILE_STRIDE + PEER_PAIRS, :])
                los.append(jnp.sum(lo * w_col, axis=0, keepdims=True))
                his.append(jnp.sum(hi * w_col, axis=0, keepdims=True))
            out_sc[t] = jnp.concatenate(los + his, axis=0)
        return carry

    lax.fori_loop(0, tm // len(tiles), tok_group, 0)
    x = ALPHA * h2_ref[...] + out_sc[...]
    mean3 = lambda a: jnp.sum(jnp.sum(a, axis=2, keepdims=True), axis=1, keepdims=True) * (1.0 / D_MODEL)
    xc = x - mean3(x)
    y_ref[...] = xc * lax.rsqrt(mean3(xc * xc) + LN_EPS) * g_ref[...] + b_ref[...]


def _peer_scratch(tm):
    return [pltpu.SMEM((tm, PEER_PAIRS), I32), pltpu.SemaphoreType.DMA(())]


def _peer_tiles():
    return [pltpu.VMEM((SLAB_ROWS * TILE_STRIDE, LANES), I32) for _ in range(GATHER_TOKENS)]


def _peer_in(e_rows, u_tab, h2, gate_t):
    nb, _, tm = gate_t.shape
    pairs = pl.BlockSpec((1, PEER_PAIRS, tm), lambda i: (i, 0, 0))
    return pl.pallas_call(
        _peer_in_kernel,
        grid=(nb,),
        in_specs=[pl.BlockSpec(memory_space=pl.ANY), pl.BlockSpec(memory_space=pltpu.VMEM),
                  pl.BlockSpec((tm, FEAT_CHUNKS, LANES), lambda i: (i, 0, 0)), pairs],
        out_specs=pairs,
        out_shape=jax.ShapeDtypeStruct((nb, PEER_PAIRS, tm), F32),
        scratch_shapes=_peer_scratch(tm) + _peer_tiles(),
        compiler_params=_cparams(("arbitrary",)),
        name="peer_expert_in",
    )(e_rows, u_tab, h2, gate_t)


def _peer_out(e_rows, v_tab, h2, w_t, g, b):
    nb, _, tm = w_t.shape
    pairs = pl.BlockSpec((1, PEER_PAIRS, tm), lambda i: (i, 0, 0))
    row = pl.BlockSpec((tm, FEAT_CHUNKS, LANES), lambda i: (i, 0, 0))
    vec = pl.BlockSpec((FEAT_CHUNKS, LANES), lambda i: (0, 0))
    return pl.pallas_call(
        _peer_out_kernel,
        grid=(nb,),
        in_specs=[pl.BlockSpec(memory_space=pl.ANY), pl.BlockSpec(memory_space=pltpu.VMEM), row, pairs, vec, vec],
        out_specs=row,
        out_shape=jax.ShapeDtypeStruct((nb * tm, FEAT_CHUNKS, LANES), F32),
        scratch_shapes=_peer_scratch(tm) + [pltpu.VMEM((tm, FEAT_CHUNKS, LANES), F32)] + _peer_tiles(),
        compiler_params=_cparams(("arbitrary",)),
        name="peer_expert_out_ln3",
    )(e_rows, v_tab, h2, w_t, g, b)


def _lane_row(vals, offset):
    return jnp.zeros((1, LANES), F32).at[0, offset:offset + vals.shape[0]].set(vals)


def _finish(x2d, ssd2d, att2d, mem_attend_fn, p):
    h1, qm = _out_proj(x2d, ssd2d, att2d, p["w_out"], p["ln1_g"], p["ln1_b"], p["w_cq"])
    o = mem_attend_fn(qm)
    h2, e_t, gate_t = _peer_route(h1, o, p["w_co"], p["ln2_g"], p["ln2_b"], p["peer_wq"], p["subkeys"])
    h2_chunks = h2.reshape(-1, FEAT_CHUNKS, LANES)
    e_rows = jnp.swapaxes(e_t, 1, 2)
    w_t = _peer_in(e_rows, p["u_tab"], h2_chunks, gate_t)
    return _peer_out(e_rows, p["v_tab"], h2_chunks, w_t, p["ln3_g"], p["ln3_b"]).reshape(-1, D_MODEL)


def kernel(x_prompt, mem_prompt, x_sample, cache_k, cache_v, cache_kidx, cache_mem_k, cache_mem_v, state_ssm, state_conv, page_table, w_in, conv_w, conv_b, dt_bias, a_log, d_skip, ssd_norm_w, rel_bias, w_out, ln1_g, ln1_b, w_cq, w_ck, w_cv, w_co, ln2_g, ln2_b, peer_wq, peer_subkeys, peer_u, peer_v, ln3_g, ln3_b):
    bp, seq, _ = x_prompt.shape
    bs, t_s, _ = x_sample.shape
    m_tok = mem_prompt.shape[1]

    c0 = SSD_INNER + CONV_DIM
    wa = w_in[:, :c0].astype(MXU_DTYPE)
    c1 = c0 + SSD_HEADS
    wb = w_in[:, c1:c1 + 4 * ATT_INNER].astype(MXU_DTYPE)
    c2 = c1 + 4 * ATT_INNER
    w_kidx = w_in[:, c2:c2 + IDX_DIM]
    wc = jnp.concatenate([w_kidx, w_kidx], axis=1).astype(MXU_DTYPE)
    wd = jnp.concatenate([w_kidx, w_in[:, c0:c1], w_in[:, c2 + IDX_DIM:],
                          jnp.zeros((D_MODEL, LANES - IDX_DIM - SSD_HEADS - IDX_HEADS), F32)], axis=1).astype(MXU_DTYPE)
    row = lambda v: v.reshape(1, -1)
    p = dict(
        w_out=w_out.astype(MXU_DTYPE), ln1_g=row(ln1_g), ln1_b=row(ln1_b), w_cq=w_cq.astype(MXU_DTYPE),
        w_co=w_co.astype(MXU_DTYPE), ln2_g=row(ln2_g), ln2_b=row(ln2_b), peer_wq=peer_wq.astype(MXU_DTYPE),
        subkeys=peer_subkeys.reshape(PEER_HEADS * 2, PEER_KEYS, PEER_HALF).astype(MXU_DTYPE),
        u_tab=_pack_table(peer_u), v_tab=_pack_table(peer_v),
        ln3_g=ln3_g.reshape(FEAT_CHUNKS, LANES), ln3_b=ln3_b.reshape(FEAT_CHUNKS, LANES))
    ssd_params = (conv_w, row(conv_b), _lane_row(a_log, DT_LANE), _lane_row(dt_bias, DT_LANE),
                  row(jnp.repeat(d_skip, SSD_HEAD_DIM)), row(ssd_norm_w))
    state_t = lambda s: jnp.swapaxes(s, -1, -2)

    n_p = bp * seq
    z, xbc, q_b, k_p, v_p, k_b, v_b, qi_b, kk_b, small = _project_in(x_prompt.reshape(n_p, D_MODEL), wa, wb, wc, wd)
    r3 = lambda a, b_: a.reshape(b_, -1, a.shape[-1])
    ssd_out, ssm_t, conv_p = _ssd_mixer(
        r3(z, bp), r3(xbc, bp), r3(small, bp), jnp.zeros((bp, SSD_CONV - 1, CONV_DIM), F32),
        jnp.zeros((bp, SSD_HEADS, SSD_STATE, SSD_HEAD_DIM), F32), SSD_CHUNK, *ssd_params)
    att_out = _dsa_prompt(r3(q_b, bp), r3(qi_b, bp), r3(small, bp), r3(kk_b, bp), r3(k_b, bp), r3(v_b, bp), rel_bias)
    kidx_p = r3(small, bp)[:, :, :IDX_DIM]
    mem_kv = _matmul(mem_prompt.reshape(bp * m_tok, D_MODEL),
                     jnp.concatenate([w_ck, w_cv], axis=1).astype(MXU_DTYPE))
    memk_p = mem_kv[:, :MEM_INNER].reshape(bp, m_tok, MEM_INNER)
    memv_p = mem_kv[:, MEM_INNER:].reshape(bp, m_tok, MEM_INNER)
    y_prompt = _finish(x_prompt.reshape(n_p, D_MODEL), ssd_out.reshape(n_p, SSD_INNER),
                       att_out.reshape(n_p, ATT_INNER),
                       lambda qm: _mem_attend(qm.reshape(bp, seq, MEM_INNER), memk_p, memv_p).reshape(n_p, MEM_INNER),
                       p)

    n_s = bs * t_s
    n_pad = -(-n_s // TOK_BLOCK) * TOK_BLOCK
    xs2d = jnp.pad(x_sample.reshape(n_s, D_MODEL), ((0, n_pad - n_s), (0, 0)))
    outs = _project_in(xs2d, wa, wb, wc, wd)
    z, xbc, q_b, k_s, v_s, k_b, v_b, qi_b, kk_b, small = [r3(a[:n_s], bs) for a in outs]
    pad_chunk = lambda a: jnp.pad(a, ((0, 0), (0, SSD_CHUNK - t_s), (0, 0)))
    ssd_s, ssm_s_t, conv_s = _ssd_mixer(pad_chunk(z), pad_chunk(xbc), pad_chunk(small), state_conv,
                                        state_t(state_ssm), t_s, *ssd_params)
    att_s = _dsa_sample(q_b, qi_b, small, k_b, v_b, kk_b, cache_k, cache_v, cache_kidx, page_table, rel_bias)
    pad_rows = lambda a: jnp.pad(a.reshape(n_s, -1), ((0, n_pad - n_s), (0, 0)))
    cmk = cache_mem_k.reshape(bs, m_tok, MEM_INNER)
    cmv = cache_mem_v.reshape(bs, m_tok, MEM_INNER)
    y_s = _finish(xs2d, pad_rows(ssd_s[:, :t_s]), pad_rows(att_s),
                  lambda qm: pad_rows(_mem_attend(qm[:n_s].reshape(bs, t_s, MEM_INNER), cmk, cmv)), p)
    y_sample = y_s[:n_s].reshape(bs, t_s, D_MODEL)

    heads = lambda a, b_: a.reshape(b_, -1, ATT_HEADS, ATT_HEAD_DIM)
    mem_heads = lambda a: a.reshape(bp, m_tok, MEM_HEADS, MEM_HEAD_DIM)
    return (y_prompt.reshape(bp, seq, D_MODEL), y_sample,
            heads(k_p, bp), heads(v_p, bp), kidx_p, mem_heads(memk_p), mem_heads(memv_p),
            state_t(ssm_t), conv_p,
            heads(k_s, bs), heads(v_s, bs), small[:, :, :IDX_DIM], state_t(ssm_s_t), conv_s)
```

```python
import functools
import math

import jax
import jax.numpy as jnp
import numpy as np
from jax import lax
from jax.experimental import pallas as pl
from jax.experimental.pallas import tpu as pltpu

F32 = jnp.float32
BF16 = jnp.bfloat16
I32 = jnp.int32
MXU_DTYPE = BF16

D_MODEL = 1024
DEPTH = 1
ALPHA = (2.0 * DEPTH) ** 0.25
LN_EPS = 1e-5

SSD_HEADS = 8
SSD_HEAD_DIM = 64
SSD_INNER = 512
SSD_GROUPS = 2
SSD_STATE = 128
SSD_CONV = 4
SSD_CHUNK = 128
CONV_DIM = 1024

ATT_HEADS = 8
ATT_HEAD_DIM = 64
ATT_INNER = 512
IDX_HEADS = 8
IDX_DIM = 64
TOPK_MAX = 256
PAGE_SIZE = 128
N_BUCKETS = 32
MAX_DISTANCE = 128

MEM_HEADS = 4
MEM_HEAD_DIM = 128
MEM_INNER = 512

PEER_HEADS = 8
PEER_KEYS = 128
PEER_HALF = 128
PEER_TOPK = 16
PEER_PAIRS = PEER_HEADS * PEER_TOPK

LANES = 128
CHUNK = 2 * LANES
TOK_BLOCK = 128
DT_LANE = 64
WIDX_LANE = 72
NEG_BIG = -1e30
INT_MIN = -(2 ** 31)
VMEM_LIMIT = 56 * 1024 * 1024
SLAB_ROWS = 4
FEAT_CHUNKS = D_MODEL // LANES
TILE_STRIDE = 136
GATHER_TOKENS = 8


def _cparams(sem, vmem=VMEM_LIMIT):
    return pltpu.CompilerParams(dimension_semantics=sem, vmem_limit_bytes=vmem)


def _mm(a, b):
    return jnp.dot(a.astype(MXU_DTYPE), b.astype(MXU_DTYPE), preferred_element_type=F32)


def _mm_nt(a, b):
    return lax.dot_general(a.astype(MXU_DTYPE), b.astype(MXU_DTYPE), (((1,), (1,)), ((), ())),
                           preferred_element_type=F32)


def _mm_exact(a, b):
    return jnp.dot(a, b, preferred_element_type=F32, precision=lax.Precision.HIGHEST)


def _sigmoid(x):
    return 1.0 / (1.0 + jnp.exp(-x))


def _layer_norm(x, g, b):
    mu = jnp.mean(x, axis=-1, keepdims=True)
    xc = x - mu
    var = jnp.mean(xc * xc, axis=-1, keepdims=True)
    return xc * lax.rsqrt(var + LN_EPS) * g + b


def _bucket_thresholds():
    max_exact = N_BUCKETS // 2
    out = []
    for n in range(max_exact, MAX_DISTANCE + 1):
        v = max_exact + int(math.log(n / max_exact) / math.log(MAX_DISTANCE / max_exact) * (N_BUCKETS - max_exact))
        out.append(min(v, N_BUCKETS - 1))
    thr = []
    for bkt in range(max_exact + 1, N_BUCKETS):
        thr.append(max_exact + next(i for i, v in enumerate(out) if v >= bkt))
    return tuple(thr)


_BUCKET_THR = _bucket_thresholds()


def _rel_bucket(dist):
    n = jnp.maximum(dist, 0)
    max_exact = N_BUCKETS // 2
    large = jnp.full(n.shape, max_exact, I32)
    for t in _BUCKET_THR:
        large = large + (n >= t).astype(I32)
    return jnp.where(n < max_exact, n, large)


def _monotone_key(x):
    b = lax.bitcast_convert_type(x, I32)
    return b ^ ((b >> 31) & 0x7FFFFFFF)


def _proj_in_kernel(x_ref, wa_ref, wb_ref, wc_ref, wd_ref,
                    z_ref, xbc_ref, q_ref, k_ref, v_ref, kb_ref, vb_ref, qi_ref, kk_ref, sm_ref):
    xb = x_ref[...].astype(MXU_DTYPE)
    z_ref[...] = jnp.dot(xb, wa_ref[:, :SSD_INNER], preferred_element_type=F32)
    xbc_ref[...] = jnp.dot(xb, wa_ref[:, SSD_INNER:], preferred_element_type=F32)
    q = jnp.dot(xb, wb_ref[:, 0:512], preferred_element_type=F32)
    q_ref[...] = (q * (ATT_HEAD_DIM ** -0.5)).astype(q_ref.dtype)
    k = jnp.dot(xb, wb_ref[:, 512:1024], preferred_element_type=F32)
    k_ref[...] = k
    kb_ref[...] = k.astype(kb_ref.dtype)
    v = jnp.dot(xb, wb_ref[:, 1024:1536], preferred_element_type=F32)
    v_ref[...] = v
    vb_ref[...] = v.astype(vb_ref.dtype)
    qi = jnp.dot(xb, wb_ref[:, 1536:2048], preferred_element_type=F32)
    qi_ref[...] = (qi * (IDX_DIM ** -0.5)).astype(qi_ref.dtype)
    kk_ref[...] = jnp.dot(xb, wc_ref[...], preferred_element_type=F32).astype(kk_ref.dtype)
    sm = jnp.dot(xb, wd_ref[...], preferred_element_type=F32)
    lane = lax.broadcasted_iota(I32, sm.shape, 1)
    is_w = (lane >= WIDX_LANE) & (lane < WIDX_LANE + IDX_HEADS)
    sm_ref[...] = jnp.where(is_w, sm * (IDX_HEADS ** -0.5), sm)


def _project_in(x2d, wa, wb, wc, wd):
    n = x2d.shape[0]
    tm = min(256, n)
    row = lambda w: pl.BlockSpec((tm, w), lambda i: (i, 0))
    full = lambda a: pl.BlockSpec(a.shape, lambda i: (0, 0))
    out_shapes = (
        jax.ShapeDtypeStruct((n, SSD_INNER), F32),
        jax.ShapeDtypeStruct((n, CONV_DIM), F32),
        jax.ShapeDtypeStruct((n, ATT_INNER), MXU_DTYPE),
        jax.ShapeDtypeStruct((n, ATT_INNER), F32),
        jax.ShapeDtypeStruct((n, ATT_INNER), F32),
        jax.ShapeDtypeStruct((n, ATT_INNER), MXU_DTYPE),
        jax.ShapeDtypeStruct((n, ATT_INNER), MXU_DTYPE),
        jax.ShapeDtypeStruct((n, ATT_INNER), MXU_DTYPE),
        jax.ShapeDtypeStruct((n, LANES), MXU_DTYPE),
        jax.ShapeDtypeStruct((n, LANES), F32),
    )
    return pl.pallas_call(
        _proj_in_kernel,
        grid=(n // tm,),
        in_specs=[row(D_MODEL), full(wa), full(wb), full(wc), full(wd)],
        out_specs=tuple(row(s.shape[1]) for s in out_shapes),
        out_shape=out_shapes,
        compiler_params=_cparams(("parallel",)),
        name="project_in",
    )(x2d, wa, wb, wc, wd)


def _ssd_kernel(n_valid, z_ref, xbc_ref, sm_ref, conv0_ref, h0_ref, convw_ref, convb_ref, alog_ref, dtb_ref,
                dskip_ref, normw_ref, y_ref, hout_ref, tail_ref, cbuf, hst, ysc):
    q_len = z_ref.shape[1]
    c = pl.program_id(1)
    pad = 8

    @pl.when(c == 0)
    def _():
        cbuf[pad - 3:pad, :] = conv0_ref[0]
        hst[...] = h0_ref[0]

    xbc = xbc_ref[0]
    cbuf[pad:pad + q_len, :] = xbc
    w = convw_ref[...]
    conv = cbuf[pad - 3:pad - 3 + q_len, :] * w[0:1]
    conv = conv + cbuf[pad - 2:pad - 2 + q_len, :] * w[1:2]
    conv = conv + cbuf[pad - 1:pad - 1 + q_len, :] * w[2:3]
    conv = conv + xbc * w[3:4]
    conv = conv + convb_ref[...]
    xc = conv * _sigmoid(conv)
    tail = cbuf[pad + n_valid - 3:pad + n_valid, :]
    cbuf[pad - 3:pad, :] = tail

    row = lax.broadcasted_iota(I32, (q_len, LANES), 0)
    lane = lax.broadcasted_iota(I32, (q_len, LANES), 1)
    is_dt = (lane >= DT_LANE) & (lane < DT_LANE + SSD_HEADS) & (row < n_valid)
    raw = sm_ref[0] + dtb_ref[...]
    dt_full = jnp.maximum(raw, 0.0) + jnp.log1p(jnp.exp(-jnp.abs(raw)))
    dtm = jnp.where(is_dt, dt_full, 0.0)
    a_full = dtm * (-jnp.exp(alog_ref[...]))
    ri = lax.broadcasted_iota(I32, (q_len, q_len), 0)
    ci = lax.broadcasted_iota(I32, (q_len, q_len), 1)
    causal = ci <= ri
    acum = _mm_exact(causal.astype(F32), a_full)
    acum_t = acum.T

    xs = xc[:, :SSD_INNER]
    for g in range(SSD_GROUPS):
        bm = xc[:, SSD_INNER + g * SSD_STATE:SSD_INNER + (g + 1) * SSD_STATE]
        cm = xc[:, SSD_INNER + (SSD_GROUPS + g) * SSD_STATE:SSD_INNER + (SSD_GROUPS + g + 1) * SSD_STATE]
        scores = _mm_nt(cm, bm)
        bm_t = bm.T
        for hh in range(SSD_HEADS // SSD_GROUPS):
            h = g * (SSD_HEADS // SSD_GROUPS) + hh
            col = acum[:, DT_LANE + h:DT_LANE + h + 1]
            rowv = acum_t[DT_LANE + h:DT_LANE + h + 1, :]
            a_last = acum[q_len - 1:q_len, DT_LANE + h:DT_LANE + h + 1]
            decay = jnp.exp(jnp.where(causal, col - rowv, -jnp.inf))
            x_h = xs[:, h * SSD_HEAD_DIM:(h + 1) * SSD_HEAD_DIM]
            xd = x_h * dtm[:, DT_LANE + h:DT_LANE + h + 1]
            h_t = hst[h]
            y = _mm(scores * decay, xd) + _mm(cm, h_t) * jnp.exp(col)
            y = y + x_h * dskip_ref[:, h * SSD_HEAD_DIM:(h + 1) * SSD_HEAD_DIM]
            ysc[:, h * SSD_HEAD_DIM:(h + 1) * SSD_HEAD_DIM] = y
            hst[h] = h_t * jnp.exp(a_last) + _mm(bm_t, xd * jnp.exp(a_last - col))

    zz = z_ref[0]
    hg = ysc[...] * (zz * _sigmoid(zz))
    hg = hg * lax.rsqrt(jnp.mean(hg * hg, axis=-1, keepdims=True) + LN_EPS)
    y_ref[0] = hg * normw_ref[...]

    @pl.when(c == pl.num_programs(1) - 1)
    def _():
        hout_ref[0] = hst[...]
        tail_ref[0] = cbuf[pad - 3:pad, :]


def _ssd_mixer(z, xbc, small, conv0, h0_t, n_valid, conv_w, conv_b, alog_lane, dtb_lane, dskip_row, norm_w):
    b, length, _ = z.shape
    q_len = SSD_CHUNK
    nc = length // q_len
    blk = lambda w: pl.BlockSpec((1, q_len, w), lambda i, c: (i, c, 0))
    per_b3 = lambda s: pl.BlockSpec((1,) + s, lambda i, c: (i,) + (0,) * len(s))
    const2 = lambda a: pl.BlockSpec(a.shape, lambda i, c: (0, 0))
    st_shape = (SSD_HEADS, SSD_STATE, SSD_HEAD_DIM)
    return pl.pallas_call(
        functools.partial(_ssd_kernel, n_valid),
        grid=(b, nc),
        in_specs=[blk(SSD_INNER), blk(CONV_DIM), blk(LANES), per_b3((SSD_CONV - 1, CONV_DIM)), per_b3(st_shape),
                  const2(conv_w), const2(conv_b), const2(alog_lane), const2(dtb_lane), const2(dskip_row),
                  const2(norm_w)],
        out_specs=(blk(SSD_INNER), per_b3(st_shape), per_b3((SSD_CONV - 1, CONV_DIM))),
        out_shape=(jax.ShapeDtypeStruct((b, length, SSD_INNER), F32),
                   jax.ShapeDtypeStruct((b,) + st_shape, F32),
                   jax.ShapeDtypeStruct((b, SSD_CONV - 1, CONV_DIM), F32)),
        scratch_shapes=[pltpu.VMEM((q_len + 8, CONV_DIM), F32), pltpu.VMEM(st_shape, F32),
                        pltpu.VMEM((q_len, SSD_INNER), F32)],
        compiler_params=_cparams(("parallel", "arbitrary")),
        name="ssd_mixer",
    )(z, xbc, small, conv0, h0_t, conv_w, conv_b, alog_lane, dtb_lane, dskip_row, norm_w)


def _kth_largest_key(count_ge, k, rows):
    def body(i, tau):
        cand = tau + jnp.left_shift(jnp.int32(1), 31 - i)
        return jnp.where(count_ge(cand) >= k, cand, tau)
    return lax.fori_loop(0, 32, body, jnp.full((rows, 1), INT_MIN, I32))


def _bias_tiles(rb_ref, bias_sc, n_tiles):
    a = lax.broadcasted_iota(I32, (LANES, LANES), 0)
    bcol = lax.broadcasted_iota(I32, (LANES, LANES), 1)
    for d in range(n_tiles):
        bucket = _rel_bucket(d * LANES + a - bcol)
        for h in range(ATT_HEADS):
            far = rb_ref[N_BUCKETS - 1, h]
            tile = jnp.zeros((LANES, LANES), F32)
            for kb in range(N_BUCKETS):
                tile = jnp.where(bucket == kb, rb_ref[kb, h] - far, tile)
            bias_sc[d, h] = tile


def _dsa_prompt_kernel(topk, q_ref, qi_ref, sm_ref, kk_ref, kb_ref, vb_ref, rb_ref, o_ref,
                       keys_sc, bias_sc, qm_sc, qim_sc, wb_sc, *head_sc):
    nh = ATT_HEADS
    cache, m_l, l_l, acc = head_sc[:nh], head_sc[nh:2 * nh], head_sc[2 * nh:3 * nh], head_sc[3 * nh:]
    i = pl.program_id(1)
    tq = q_ref.shape[1]
    n_far = bias_sc.shape[0] - 1
    n_chunks = (i + 2) // 2

    @pl.when(i == 0)
    def _():
        _bias_tiles(rb_ref, bias_sc, n_far + 1)

    lane = lax.broadcasted_iota(I32, (tq, LANES), 1)
    low_half = lane < ATT_HEAD_DIM
    lane2 = lax.broadcasted_iota(I32, (tq, CHUNK), 1)
    row2 = lax.broadcasted_iota(I32, (tq, CHUNK), 0)
    sm = sm_ref[0]

    for h in range(nh):
        keep = low_half if h % 2 == 0 else jnp.logical_not(low_half)
        pair = slice((h // 2) * LANES, (h // 2 + 1) * LANES)
        qp, qip = q_ref[0, :, pair], qi_ref[0, :, pair]
        qm_sc[h] = jnp.where(keep, qp, jnp.zeros_like(qp))
        qim_sc[h] = jnp.where(keep, qip, jnp.zeros_like(qip))
        wb_sc[h] = jnp.broadcast_to(sm[:, WIDX_LANE + h:WIDX_LANE + h + 1], (tq, LANES))
        m_l[h][...] = jnp.full((tq, LANES), NEG_BIG, F32)
        l_l[h][...] = jnp.zeros((tq, LANES), F32)
        acc[h][...] = jnp.zeros((tq, LANES), F32)

    def causal(c):
        return (c * CHUNK + lane2) <= (i * tq + row2)

    def chunk_logits(c, carry):
        start = pl.multiple_of(c * CHUNK, CHUNK)
        kk = kk_ref[0, pl.ds(start, CHUNK), :]
        sc = jnp.zeros((tq, CHUNK), F32)
        for h in range(IDX_HEADS):
            wb = wb_sc[h]
            sc = sc + jnp.maximum(_mm_nt(qim_sc[h], kk), 0.0) * jnp.concatenate([wb, wb], axis=1)
        keys_sc[c] = _monotone_key(jnp.where(causal(c), sc, -jnp.inf))
        d0 = jnp.clip(i - 2 * c, 0, n_far)
        d1 = jnp.clip(i - 2 * c - 1, 0, n_far)
        for h in range(nh):
            kblk = kb_ref[0, pl.ds(start, CHUNK), (h // 2) * LANES:(h // 2 + 1) * LANES]
            bias = jnp.concatenate([bias_sc[d0, h], bias_sc[d1, h]], axis=1)
            cache[h][c] = _mm_nt(qm_sc[h], kblk) + bias
        return carry

    lax.fori_loop(0, n_chunks, chunk_logits, 0)

    def count_ge(cand):
        def body(c, cnt):
            k = keys_sc[c]
            return cnt + jnp.where(k[:, :LANES] >= cand, 1.0, 0.0) + jnp.where(k[:, LANES:] >= cand, 1.0, 0.0)
        cnt = lax.fori_loop(0, n_chunks, body, jnp.zeros((tq, LANES), F32))
        return jnp.sum(cnt, axis=1, keepdims=True)

    tau = _kth_largest_key(count_ge, float(topk), tq)

    def chunk_max(c, carry):
        sel = (keys_sc[c] >= tau) & causal(c)
        for h in range(nh):
            s = jnp.where(sel, cache[h][c], NEG_BIG)
            cache[h][c] = s
            m_l[h][...] = jnp.maximum(m_l[h][...], jnp.maximum(s[:, :LANES], s[:, LANES:]))
        return carry

    lax.fori_loop(0, n_chunks, chunk_max, 0)
    for h in range(nh):
        m_l[h][...] = jnp.broadcast_to(jnp.max(m_l[h][...], axis=1, keepdims=True), (tq, LANES))

    def chunk_pv(c, carry):
        start = pl.multiple_of(c * CHUNK, CHUNK)
        for h in range(nh):
            mb = m_l[h][...]
            s = cache[h][c]
            p0 = jnp.exp(s[:, :LANES] - mb)
            p1 = jnp.exp(s[:, LANES:] - mb)
            l_l[h][...] = l_l[h][...] + (p0 + p1)
            vblk = vb_ref[0, pl.ds(start, CHUNK), (h // 2) * LANES:(h // 2 + 1) * LANES]
            acc[h][...] = acc[h][...] + _mm(jnp.concatenate([p0, p1], axis=1), vblk)
        return carry

    lax.fori_loop(0, n_chunks, chunk_pv, 0)

    for g in range(nh // 2):
        lo = acc[2 * g][...] / jnp.sum(l_l[2 * g][...], axis=1, keepdims=True)
        hi = acc[2 * g + 1][...] / jnp.sum(l_l[2 * g + 1][...], axis=1, keepdims=True)
        o_ref[0, :, g * LANES:(g + 1) * LANES] = jnp.where(low_half, lo, hi)


def _dsa_prompt(q_b, qi_b, small, kk_b, k_b, v_b, rel_bias):
    b, s, _ = q_b.shape
    tq = LANES
    nq = s // tq
    topk = min(TOPK_MAX, s // 4)
    n_bias_tiles = 3
    assert MAX_DISTANCE <= LANES + 1 and s % CHUNK == 0
    n_chunks = s // CHUNK
    qblk = lambda w: pl.BlockSpec((1, tq, w), lambda i, j: (i, j, 0))
    per_b = lambda w: pl.BlockSpec((1, s, w), lambda i, j: (i, 0, 0))
    per_head = lambda shape, dt: [pltpu.VMEM(shape, dt) for _ in range(ATT_HEADS)]
    return pl.pallas_call(
        functools.partial(_dsa_prompt_kernel, topk),
        grid=(b, nq),
        in_specs=[qblk(ATT_INNER), qblk(ATT_INNER), qblk(LANES), per_b(LANES), per_b(ATT_INNER), per_b(ATT_INNER),
                  pl.BlockSpec(memory_space=pltpu.SMEM)],
        out_specs=qblk(ATT_INNER),
        out_shape=jax.ShapeDtypeStruct((b, s, ATT_INNER), F32),
        scratch_shapes=[pltpu.VMEM((n_chunks, tq, CHUNK), I32),
                        pltpu.VMEM((n_bias_tiles, ATT_HEADS, LANES, LANES), F32),
                        pltpu.VMEM((ATT_HEADS, tq, LANES), q_b.dtype), pltpu.VMEM((ATT_HEADS, tq, LANES), qi_b.dtype),
                        pltpu.VMEM((ATT_HEADS, tq, LANES), F32)]
                       + per_head((n_chunks, tq, CHUNK), F32)
                       + per_head((tq, LANES), F32)
                       + per_head((tq, LANES), F32)
                       + per_head((tq, LANES), F32),
        compiler_params=_cparams(("parallel", "arbitrary")),
        name="dsa_prompt",
    )(q_b, qi_b, small, kk_b, k_b, v_b, rel_bias)


def _dsa_sample_score_kernel(topk, t_new, pt_ref, qi_ref, w_ref, *refs):
    kidx_refs, (kknew_ref, keys_ref, keysnew_ref, tau_ref, keys_sc) = refs[:-5], refs[-5:]
    pg = len(kidx_refs)
    j = pl.program_id(1)
    n_pages = pl.num_programs(1) * pg
    rows = qi_ref.shape[1]
    lane = lax.broadcasted_iota(I32, (t_new, LANES), 1)
    rowi = lax.broadcasted_iota(I32, (t_new, LANES), 0)

    def head_sum(s):
        s = jnp.maximum(s, 0.0) * w_ref[0]
        return jnp.concatenate(
            [jnp.sum(s[t * IDX_HEADS:(t + 1) * IDX_HEADS], axis=0, keepdims=True) for t in range(t_new)], axis=0)

    for r, kidx_ref in enumerate(kidx_refs):
        keys_sc[j * pg + r] = _monotone_key(head_sum(_mm_nt(qi_ref[0], kidx_ref[0])))

    @pl.when(j == pl.num_programs(1) - 1)
    def _():
        sc = head_sum(_mm_nt(qi_ref[0], kknew_ref[0]))
        keys_new = _monotone_key(jnp.where(lane <= rowi, sc, -jnp.inf))
        keys_sc[n_pages] = keys_new

        def count_ge(cand):
            def body(jj, cnt):
                return cnt + jnp.where(keys_sc[jj] >= cand, 1.0, 0.0)
            cnt = lax.fori_loop(0, n_pages + 1, body, jnp.zeros((t_new, LANES), F32))
            return jnp.sum(cnt, axis=1, keepdims=True)

        tau = _kth_largest_key(count_ge, float(topk), t_new)
        tau_ref[0] = jnp.broadcast_to(tau, (t_new, LANES))
        keys_ref[0] = keys_sc[0:n_pages]
        keysnew_ref[0] = keys_new


def _dsa_sample_attn_kernel(t_new, past, pg, pt_ref, q_ref, keys_ref, keysnew_ref, tau_ref, knew_ref, vnew_ref,
                            rbrow_ref, expand_ref, *refs):
    k_refs, v_refs = refs[:pg], refs[pg:2 * pg]
    o_ref, cache_sc, m_sc, l_sc, acc_sc = refs[2 * pg:]
    phase = pl.program_id(1)
    j = pl.program_id(2)
    n_steps = pl.num_programs(2)
    n_pages = n_steps * pg
    rows = q_ref.shape[1]
    wide = PAGE_SIZE * ATT_HEADS
    lane = lax.broadcasted_iota(I32, (rows, wide), 1)
    rowi = lax.broadcasted_iota(I32, (rows, wide), 0)
    tok = lax.rem(rowi, t_new)
    own_head = lax.rem(lane, ATT_HEADS) == rowi // t_new
    key_in_page = lane // ATT_HEADS
    q = q_ref[0]

    def rows_of(x_t):
        return jnp.concatenate([x_t] * ATT_HEADS, axis=0)

    tau = rows_of(tau_ref[0])

    def page_rows(ref):
        return ref[0].reshape(wide, ATT_HEAD_DIM)

    def selected(key_blocks):
        bits = jnp.concatenate([jnp.where(rows_of(k) >= tau, 1.0, 0.0) for k in key_blocks], axis=0)
        wide_bits = _mm(bits, expand_ref[...])
        return [wide_bits[r * rows:(r + 1) * rows] > 0.5 for r in range(len(key_blocks))]

    def near_bias(key_pos):
        bucket = _rel_bucket(past + tok - key_pos)
        far = rbrow_ref[:, N_BUCKETS - 1:N_BUCKETS]
        bias = jnp.zeros((rows, wide), F32)
        for kb in range(N_BUCKETS - 1):
            bias = jnp.where(bucket == kb, rbrow_ref[:, kb:kb + 1] - far, bias)
        return bias

    @pl.when(phase == 0)
    def _():
        sel = selected([keys_ref[0, r] for r in range(pg)])
        for r in range(pg):
            cache_sc[j * pg + r] = jnp.where(sel[r] & own_head, _mm_nt(q, page_rows(k_refs[r])), NEG_BIG)

        @pl.when(j == n_steps - 1)
        def _():
            last = n_pages - 1
            cache_sc[last] = cache_sc[last] + near_bias(last * PAGE_SIZE + key_in_page)
            s = _mm_nt(q, knew_ref[0]) + near_bias(past + key_in_page)
            ok = selected([keysnew_ref[0]])[0] & own_head & (key_in_page <= tok)
            cache_sc[n_pages] = jnp.where(ok, s, NEG_BIG)
            m = lax.fori_loop(0, n_pages + 1, lambda jj, m: jnp.maximum(m, cache_sc[jj]),
                              jnp.full((rows, wide), NEG_BIG, F32))
            m_sc[...] = jnp.broadcast_to(jnp.max(m, axis=1, keepdims=True), (rows, wide))
            l_sc[...] = jnp.zeros(l_sc.shape, F32)
            acc_sc[...] = jnp.zeros(acc_sc.shape, F32)

    @pl.when(phase == 1)
    def _():
        mb = m_sc[...]
        acc = acc_sc[...]
        lsum = l_sc[...]
        for r in range(pg):
            p = jnp.exp(cache_sc[j * pg + r] - mb)
            lsum = lsum + p
            acc = acc + _mm(p, page_rows(v_refs[r]))
        l_sc[...] = lsum
        acc_sc[...] = acc

        @pl.when(j == n_steps - 1)
        def _():
            p_new = jnp.exp(cache_sc[n_pages] - mb)
            l = jnp.sum(lsum + p_new, axis=1, keepdims=True)
            a = (acc + _mm(p_new, vnew_ref[0])) / l
            o_ref[0] = jnp.concatenate([a[h * t_new:(h + 1) * t_new] for h in range(ATT_HEADS)], axis=1)


def _dsa_sample(q_b, qi_b, small, k_b, v_b, kk_b, cache_k, cache_v, cache_kidx, page_table, rel_bias):
    b, t_new, _ = q_b.shape
    n_pages = page_table.shape[1]
    past = n_pages * PAGE_SIZE
    topk = min(TOPK_MAX, (past + t_new) // 4)
    rows = t_new * ATT_HEADS
    assert PAGE_SIZE >= MAX_DISTANCE

    def pages_per_step(limit):
        return max(g for g in (1, 2, 4, 8, 16, 32) if g <= limit and n_pages % g == 0)

    pgs = pages_per_step(32)
    qi_rows = qi_b.reshape(b, rows, IDX_DIM)
    w_rows = small[:, :, WIDX_LANE:WIDX_LANE + IDX_HEADS].reshape(b, rows, 1)
    kk_new = jnp.pad(kk_b[:, :, :IDX_DIM], ((0, 0), (0, LANES - t_new), (0, 0)))
    per_b = lambda s: pl.BlockSpec((1,) + s, lambda i, j, pt: (i,) + (0,) * len(s))
    kidx_page = lambda r: pl.BlockSpec((1, PAGE_SIZE, IDX_DIM), lambda i, j, pt: (pt[i, j * pgs + r], 0, 0))
    keys, keys_new, tau = pl.pallas_call(
        functools.partial(_dsa_sample_score_kernel, topk, t_new),
        grid_spec=pltpu.PrefetchScalarGridSpec(
            num_scalar_prefetch=1, grid=(b, n_pages // pgs),
            in_specs=[per_b((rows, IDX_DIM)), per_b((rows, 1))] + [kidx_page(r) for r in range(pgs)]
                     + [per_b((LANES, IDX_DIM))],
            out_specs=(per_b((n_pages, t_new, LANES)), per_b((t_new, LANES)), per_b((t_new, LANES))),
            scratch_shapes=[pltpu.VMEM((n_pages + 1, t_new, LANES), I32)]),
        out_shape=(jax.ShapeDtypeStruct((b, n_pages, t_new, LANES), I32),
                   jax.ShapeDtypeStruct((b, t_new, LANES), I32),
                   jax.ShapeDtypeStruct((b, t_new, LANES), I32)),
        compiler_params=_cparams(("parallel", "arbitrary")),
        name="dsa_sample_scores",
    )(page_table, qi_rows, w_rows, *([cache_kidx] * pgs), kk_new)

    pg = pages_per_step(8)
    n_steps = n_pages // pg
    q_rows = q_b.reshape(b, t_new, ATT_HEADS, ATT_HEAD_DIM).swapaxes(1, 2).reshape(b, rows, ATT_HEAD_DIM)
    rb_rows = jnp.repeat(rel_bias.T, t_new, axis=0)
    wide = PAGE_SIZE * ATT_HEADS
    new_page = lambda x: jnp.pad(x.reshape(b, t_new, ATT_HEADS, ATT_HEAD_DIM),
                                 ((0, 0), (0, PAGE_SIZE - t_new), (0, 0), (0, 0))).reshape(b, wide, ATT_HEAD_DIM)
    k_new, v_new = new_page(k_b), new_page(v_b)
    expand = jnp.repeat(jnp.eye(PAGE_SIZE, dtype=MXU_DTYPE), ATT_HEADS, axis=1)
    per_b = lambda s: pl.BlockSpec((1,) + s, lambda i, ph, j, pt: (i,) + (0,) * len(s))
    page_blk = (1, PAGE_SIZE, ATT_HEADS, ATT_HEAD_DIM)
    k_page = lambda r: pl.BlockSpec(
        page_blk, lambda i, ph, j, pt: (pt[i, jnp.where(ph == 0, j, n_steps - 1) * pg + r], 0, 0, 0))
    v_page = lambda r: pl.BlockSpec(
        page_blk, lambda i, ph, j, pt: (pt[i, jnp.where(ph == 1, j, 0) * pg + r], 0, 0, 0))
    return pl.pallas_call(
        functools.partial(_dsa_sample_attn_kernel, t_new, past, pg),
        grid_spec=pltpu.PrefetchScalarGridSpec(
            num_scalar_prefetch=1, grid=(b, 2, n_steps),
            in_specs=[per_b((rows, ATT_HEAD_DIM)),
                      pl.BlockSpec((1, pg, t_new, LANES),
                                   lambda i, ph, j, pt: (i, jnp.where(ph == 0, j, n_steps - 1), 0, 0)),
                      per_b((t_new, LANES)), per_b((t_new, LANES)),
                      per_b((wide, ATT_HEAD_DIM)), per_b((wide, ATT_HEAD_DIM)),
                      pl.BlockSpec((rows, N_BUCKETS), lambda i, ph, j, pt: (0, 0)),
                      pl.BlockSpec((PAGE_SIZE, wide), lambda i, ph, j, pt: (0, 0))]
                     + [k_page(r) for r in range(pg)] + [v_page(r) for r in range(pg)],
            out_specs=per_b((t_new, ATT_INNER)),
            scratch_shapes=[pltpu.VMEM((n_pages + 1, rows, wide), F32), pltpu.VMEM((rows, wide), F32),
                            pltpu.VMEM((rows, wide), F32), pltpu.VMEM((rows, ATT_HEAD_DIM), F32)]),
        out_shape=jax.ShapeDtypeStruct((b, t_new, ATT_INNER), F32),
        compiler_params=_cparams(("parallel", "arbitrary", "arbitrary")),
        name="dsa_sample_attend",
    )(page_table, q_rows, keys, keys_new, tau, k_new, v_new, rb_rows, expand,
      *([cache_k] * pg), *([cache_v] * pg))


def _matmul_kernel(x_ref, w_ref, o_ref):
    o_ref[...] = _mm(x_ref[...], w_ref[...])


def _matmul(x2d, w):
    n, kdim = x2d.shape
    tm = min(256, n)
    return pl.pallas_call(
        _matmul_kernel,
        grid=(n // tm,),
        in_specs=[pl.BlockSpec((tm, kdim), lambda i: (i, 0)), pl.BlockSpec(w.shape, lambda i: (0, 0))],
        out_specs=pl.BlockSpec((tm, w.shape[1]), lambda i: (i, 0)),
        out_shape=jax.ShapeDtypeStruct((n, w.shape[1]), F32),
        compiler_params=_cparams(("parallel",)),
        name="mem_kv_proj",
    )(x2d, w)


def _out_proj_kernel(x_ref, ssd_ref, att_ref, wout_ref, g_ref, b_ref, wcq_ref, h1_ref, qm_ref):
    m = _mm(ssd_ref[...], wout_ref[:SSD_INNER, :]) + _mm(att_ref[...], wout_ref[SSD_INNER:, :])
    h1 = _layer_norm(ALPHA * x_ref[...] + m, g_ref[...], b_ref[...])
    h1_ref[...] = h1
    qm_ref[...] = _mm(h1, wcq_ref[...])


def _out_proj(x2d, ssd2d, att2d, w_out, g, b, w_cq):
    n = x2d.shape[0]
    tm = min(256, n)
    row = lambda w: pl.BlockSpec((tm, w), lambda i: (i, 0))
    full = lambda a: pl.BlockSpec(a.shape, lambda i: (0, 0))
    return pl.pallas_call(
        _out_proj_kernel,
        grid=(n // tm,),
        in_specs=[row(D_MODEL), row(SSD_INNER), row(ATT_INNER), full(w_out), full(g), full(b), full(w_cq)],
        out_specs=(row(D_MODEL), row(MEM_INNER)),
        out_shape=(jax.ShapeDtypeStruct((n, D_MODEL), F32), jax.ShapeDtypeStruct((n, MEM_INNER), F32)),
        compiler_params=_cparams(("parallel",)),
        name="out_proj_ln1",
    )(x2d, ssd2d, att2d, w_out, g, b, w_cq)


def _mem_attend_kernel(q_ref, mk_ref, mv_ref, o_ref):
    q = q_ref[0]
    for h in range(MEM_HEADS):
        sl = slice(h * MEM_HEAD_DIM, (h + 1) * MEM_HEAD_DIM)
        logits = _mm_nt(q[:, sl], mk_ref[0, :, sl]) * (MEM_HEAD_DIM ** -0.5)
        logits = logits - jnp.max(logits, axis=-1, keepdims=True)
        p = jnp.exp(logits)
        p = p / jnp.sum(p, axis=-1, keepdims=True)
        o_ref[0, :, sl] = _mm(p, mv_ref[0, :, sl])


def _mem_attend(qm, mk, mv):
    b, length, _ = qm.shape
    tq = min(512, length)
    m_tok = mk.shape[1]
    return pl.pallas_call(
        _mem_attend_kernel,
        grid=(b, length // tq),
        in_specs=[pl.BlockSpec((1, tq, MEM_INNER), lambda i, j: (i, j, 0)),
                  pl.BlockSpec((1, m_tok, MEM_INNER), lambda i, j: (i, 0, 0)),
                  pl.BlockSpec((1, m_tok, MEM_INNER), lambda i, j: (i, 0, 0))],
        out_specs=pl.BlockSpec((1, tq, MEM_INNER), lambda i, j: (i, j, 0)),
        out_shape=jax.ShapeDtypeStruct((b, length, MEM_INNER), F32),
        compiler_params=_cparams(("parallel", "parallel")),
        name="mem_attend",
    )(qm, mk, mv)


def _top_rows(s, n_top, with_index):
    rows = s.shape[0]
    ridx = lax.broadcasted_iota(I32, s.shape, 0).astype(F32)
    vals, idxs = [], []
    for _ in range(n_top):
        m = jnp.max(s, axis=0, keepdims=True)
        first = jnp.min(jnp.where(s == m, ridx, float(rows)), axis=0, keepdims=True)
        s = jnp.where(ridx == first, -jnp.inf, s)
        vals.append(m)
        idxs.append(first)
    return jnp.concatenate(vals, axis=0), (jnp.concatenate(idxs, axis=0) if with_index else None)


_SUBLANES = 8
ROUTE_HEADS_PER_TRIP = 8


def _staircase_blocks(n):
    by_b, single_b = [], []
    for a in range(n):
        count = n // (a + 1)
        if count > 1:
            by_b += [(a, b) for b in range(0, count, _SUBLANES)]
        else:
            single_b.append(a)
    assert len(single_b) % _SUBLANES == 0 and single_b == list(range(n - len(single_b), n))
    return tuple(by_b), tuple(single_b[::_SUBLANES])


_STAIRCASE = _staircase_blocks(PEER_TOPK)


def _peer_route_kernel(h1_ref, o_ref, wco_ref, g_ref, b_ref, wq_ref, sk_ref, h2_ref, e_ref, gate_ref, h2b_sc):
    h2 = _layer_norm(ALPHA * h1_ref[...] + _mm(o_ref[...], wco_ref[...]), g_ref[...], b_ref[...])
    h2_ref[...] = h2
    h2b_sc[...] = h2.astype(MXU_DTYPE)

    def route_head(h, carry):
        tops = []
        for p in range(2):
            c = h * 2 + p
            qs = jnp.dot(h2b_sc[...], wq_ref[c], preferred_element_type=F32)
            s_t = _mm_nt(sk_ref[c], qs)
            tops.append(_top_rows(s_t, PEER_TOPK, True))
        (v0, i0), (v1, i1) = tops
        by_b, by_a = _STAIRCASE
        sub = _SUBLANES
        cand = jnp.concatenate([v0[a:a + 1] + v1[b:b + sub] for a, b in by_b]
                               + [v0[a:a + sub] + v1[0:1] for a in by_a], axis=0)
        cidx = jnp.concatenate([i0[a:a + 1] * PEER_KEYS + i1[b:b + sub] for a, b in by_b]
                               + [i0[a:a + sub] * PEER_KEYS + i1[0:1] for a in by_a], axis=0)
        pos = lax.broadcasted_iota(I32, cand.shape, 0).astype(F32)
        gs, es = [], []
        for _ in range(PEER_TOPK):
            m = jnp.max(cand, axis=0, keepdims=True)
            first = jnp.min(jnp.where(cand == m, pos, float(cand.shape[0])), axis=0, keepdims=True)
            hit = pos == first
            es.append(jnp.max(jnp.where(hit, cidx, -1.0), axis=0, keepdims=True))
            cand = jnp.where(hit, -jnp.inf, cand)
            gs.append(m)
        g_s = jnp.concatenate(gs, axis=0)
        ex = jnp.exp(g_s - g_s[0:1])
        rows = pl.ds(pl.multiple_of(h * PEER_TOPK, PEER_TOPK), PEER_TOPK)
        gate_ref[0, rows, :] = ex / jnp.sum(ex, axis=0, keepdims=True)
        e_ref[0, rows, :] = jnp.concatenate(es, axis=0).astype(I32) * SLAB_ROWS
        return carry

    def route_group(g, carry):
        for k in range(ROUTE_HEADS_PER_TRIP):
            route_head(g * ROUTE_HEADS_PER_TRIP + k, carry)
        return carry

    lax.fori_loop(0, PEER_HEADS // ROUTE_HEADS_PER_TRIP, route_group, 0)


def _peer_route(h1, o, w_co, g, b, wq, subkeys):
    n = h1.shape[0]
    tm = TOK_BLOCK
    nb = n // tm
    row = lambda w: pl.BlockSpec((tm, w), lambda i: (i, 0))
    full = lambda a: pl.BlockSpec(a.shape, lambda i: (0,) * a.ndim)
    pairs = pl.BlockSpec((1, PEER_PAIRS, tm), lambda i: (i, 0, 0))
    return pl.pallas_call(
        _peer_route_kernel,
        grid=(nb,),
        in_specs=[row(D_MODEL), row(MEM_INNER), full(w_co), full(g), full(b), full(wq), full(subkeys)],
        out_specs=(row(D_MODEL), pairs, pairs),
        out_shape=(jax.ShapeDtypeStruct((n, D_MODEL), F32),
                   jax.ShapeDtypeStruct((nb, PEER_PAIRS, tm), I32),
                   jax.ShapeDtypeStruct((nb, PEER_PAIRS, tm), F32)),
        scratch_shapes=[pltpu.VMEM((tm, D_MODEL), MXU_DTYPE)],
        compiler_params=_cparams(("parallel",)),
        name="ln2_peer_route",
    )(h1, o, w_co, g, b, wq, subkeys)


def _pack_table(t):
    tb = lax.bitcast_convert_type(t.astype(BF16), jnp.uint16).astype(jnp.uint32)
    half = t.shape[1] // 2
    words = tb[:, :half] | (tb[:, half:] << 16)
    return lax.bitcast_convert_type(words, I32).reshape(t.shape[0] * SLAB_ROWS, LANES)


def _unpack_words(w):
    lo = lax.bitcast_convert_type(w << 16, F32)
    hi = lax.bitcast_convert_type(w & jnp.int32(-65536), F32)
    return lo, hi


def _gather_token_rows(idx_smem, tab_ref, tile_ref, t):
    for p in range(PEER_PAIRS):
        row = pl.multiple_of(idx_smem[t, p], SLAB_ROWS)
        tile_ref[pl.ds(p, SLAB_ROWS, stride=TILE_STRIDE), :] = tab_ref[pl.ds(row, SLAB_ROWS), :]


def _load_indices(e_hbm, idx_smem, sem):
    cp = pltpu.make_async_copy(e_hbm.at[pl.program_id(0)], idx_smem, sem)
    cp.start()
    cp.wait()


def _gelu_tanh(x):
    return 0.5 * x * (1.0 + jnp.tanh(math.sqrt(2.0 / math.pi) * (x + 0.044715 * (x * x * x))))


def _peer_in_kernel(e_hbm, tab_ref, h2_ref, gate_ref, w_ref, idx_smem, sem, *tiles):
    _load_indices(e_hbm, idx_smem, sem)
    tm = h2_ref.shape[0]
    lane = lax.broadcasted_iota(I32, (PEER_PAIRS, tm), 1)

    def tok_group(g, act_t):
        cols = []
        for u, tile_ref in enumerate(tiles):
            t = g * len(tiles) + u
            _gather_token_rows(idx_smem, tab_ref, tile_ref, t)
            x_t = h2_ref[t]
            acc = jnp.zeros((PEER_PAIRS, LANES), F32)
            for j in range(SLAB_ROWS):
                lo, hi = _unpack_words(tile_ref[j * TILE_STRIDE:j * TILE_STRIDE + PEER_PAIRS, :])
                acc = acc + lo * x_t[j:j + 1, :] + hi * x_t[SLAB_ROWS + j:SLAB_ROWS + j + 1, :]
            cols.append(jnp.sum(acc, axis=1, keepdims=True))
        for u, col in enumerate(cols):
            act_t = jnp.where(lane == g * len(tiles) + u, col, act_t)
        return act_t

    act_t = lax.fori_loop(0, tm // len(tiles), tok_group, jnp.zeros((PEER_PAIRS, tm), F32))
    w_ref[0] = gate_ref[0] * _gelu_tanh(act_t)


def _peer_out_kernel(e_hbm, tab_ref, h2_ref, w_ref, g_ref, b_ref, y_ref, idx_smem, sem, out_sc, *tiles):
    _load_indices(e_hbm, idx_smem, sem)
    tm = h2_ref.shape[0]
    lane = lax.broadcasted_iota(I32, (PEER_PAIRS, tm), 1)

    def tok_group(g, carry):
        for u, tile_ref in enumerate(tiles):
            t = g * len(tiles) + u
            _gather_token_rows(idx_smem, tab_ref, tile_ref, t)
            w_col = jnp.sum(jnp.where(lane == t, w_ref[0], 0.0), axis=1, keepdims=True)
            los, his = [], []
            for j in range(SLAB_ROWS):
                lo, hi = _unpack_words(tile_ref[j * TILE_STRIDE:j * TILE_STRIDE + PEER_PAIRS, :])
                los.append(jnp.sum(lo * w_col, axis=0, keepdims=True))
                his.append(jnp.sum(hi * w_col, axis=0, keepdims=True))
            out_sc[t] = jnp.concatenate(los + his, axis=0)
        return carry

    lax.fori_loop(0, tm // len(tiles), tok_group, 0)
    x = ALPHA * h2_ref[...] + out_sc[...]
    mean3 = lambda a: jnp.sum(jnp.sum(a, axis=2, keepdims=True), axis=1, keepdims=True) * (1.0 / D_MODEL)
    xc = x - mean3(x)
    y_ref[...] = xc * lax.rsqrt(mean3(xc * xc) + LN_EPS) * g_ref[...] + b_ref[...]


def _peer_scratch(tm):
    return [pltpu.SMEM((tm, PEER_PAIRS), I32), pltpu.SemaphoreType.DMA(())]


def _peer_tiles():
    return [pltpu.VMEM((SLAB_ROWS * TILE_STRIDE, LANES), I32) for _ in range(GATHER_TOKENS)]


def _peer_in(e_rows, u_tab, h2, gate_t):
    nb, _, tm = gate_t.shape
    pairs = pl.BlockSpec((1, PEER_PAIRS, tm), lambda i: (i, 0, 0))
    return pl.pallas_call(
        _peer_in_kernel,
        grid=(nb,),
        in_specs=[pl.BlockSpec(memory_space=pl.ANY), pl.BlockSpec(memory_space=pltpu.VMEM),
                  pl.BlockSpec((tm, FEAT_CHUNKS, LANES), lambda i: (i, 0, 0)), pairs],
        out_specs=pairs,
        out_shape=jax.ShapeDtypeStruct((nb, PEER_PAIRS, tm), F32),
        scratch_shapes=_peer_scratch(tm) + _peer_tiles(),
        compiler_params=_cparams(("arbitrary",)),
        name="peer_expert_in",
    )(e_rows, u_tab, h2, gate_t)


def _peer_out(e_rows, v_tab, h2, w_t, g, b):
    nb, _, tm = w_t.shape
    pairs = pl.BlockSpec((1, PEER_PAIRS, tm), lambda i: (i, 0, 0))
    row = pl.BlockSpec((tm, FEAT_CHUNKS, LANES), lambda i: (i, 0, 0))
    vec = pl.BlockSpec((FEAT_CHUNKS, LANES), lambda i: (0, 0))
    return pl.pallas_call(
        _peer_out_kernel,
        grid=(nb,),
        in_specs=[pl.BlockSpec(memory_space=pl.ANY), pl.BlockSpec(memory_space=pltpu.VMEM), row, pairs, vec, vec],
        out_specs=row,
        out_shape=jax.ShapeDtypeStruct((nb * tm, FEAT_CHUNKS, LANES), F32),
        scratch_shapes=_peer_scratch(tm) + [pltpu.VMEM((tm, FEAT_CHUNKS, LANES), F32)] + _peer_tiles(),
        compiler_params=_cparams(("arbitrary",)),
        name="peer_expert_out_ln3",
    )(e_rows, v_tab, h2, w_t, g, b)


def _lane_row(vals, offset):
    return jnp.zeros((1, LANES), F32).at[0, offset:offset + vals.shape[0]].set(vals)


def _finish(x2d, ssd2d, att2d, mem_attend_fn, p):
    h1, qm = _out_proj(x2d, ssd2d, att2d, p["w_out"], p["ln1_g"], p["ln1_b"], p["w_cq"])
    o = mem_attend_fn(qm)
    h2, e_t, gate_t = _peer_route(h1, o, p["w_co"], p["ln2_g"], p["ln2_b"], p["peer_wq"], p["subkeys"])
    h2_chunks = h2.reshape(-1, FEAT_CHUNKS, LANES)
    e_rows = jnp.swapaxes(e_t, 1, 2)
    w_t = _peer_in(e_rows, p["u_tab"], h2_chunks, gate_t)
    return _peer_out(e_rows, p["v_tab"], h2_chunks, w_t, p["ln3_g"], p["ln3_b"]).reshape(-1, D_MODEL)


def kernel(x_prompt, mem_prompt, x_sample, cache_k, cache_v, cache_kidx, cache_mem_k, cache_mem_v, state_ssm, state_conv, page_table, w_in, conv_w, conv_b, dt_bias, a_log, d_skip, ssd_norm_w, rel_bias, w_out, ln1_g, ln1_b, w_cq, w_ck, w_cv, w_co, ln2_g, ln2_b, peer_wq, peer_subkeys, peer_u, peer_v, ln3_g, ln3_b):
    bp, seq, _ = x_prompt.shape
    bs, t_s, _ = x_sample.shape
    m_tok = mem_prompt.shape[1]

    c0 = SSD_INNER + CONV_DIM
    wa = w_in[:, :c0].astype(MXU_DTYPE)
    c1 = c0 + SSD_HEADS
    wb = w_in[:, c1:c1 + 4 * ATT_INNER].astype(MXU_DTYPE)
    c2 = c1 + 4 * ATT_INNER
    w_kidx = w_in[:, c2:c2 + IDX_DIM]
    wc = jnp.concatenate([w_kidx, w_kidx], axis=1).astype(MXU_DTYPE)
    wd = jnp.concatenate([w_kidx, w_in[:, c0:c1], w_in[:, c2 + IDX_DIM:],
                          jnp.zeros((D_MODEL, LANES - IDX_DIM - SSD_HEADS - IDX_HEADS), F32)], axis=1).astype(MXU_DTYPE)
    row = lambda v: v.reshape(1, -1)
    p = dict(
        w_out=w_out.astype(MXU_DTYPE), ln1_g=row(ln1_g), ln1_b=row(ln1_b), w_cq=w_cq.astype(MXU_DTYPE),
        w_co=w_co.astype(MXU_DTYPE), ln2_g=row(ln2_g), ln2_b=row(ln2_b),
        peer_wq=peer_wq.astype(MXU_DTYPE).reshape(D_MODEL, PEER_HEADS * 2, PEER_HALF).swapaxes(0, 1),
        subkeys=peer_subkeys.reshape(PEER_HEADS * 2, PEER_KEYS, PEER_HALF).astype(MXU_DTYPE),
        u_tab=_pack_table(peer_u), v_tab=_pack_table(peer_v),
        ln3_g=ln3_g.reshape(FEAT_CHUNKS, LANES), ln3_b=ln3_b.reshape(FEAT_CHUNKS, LANES))
    ssd_params = (conv_w, row(conv_b), _lane_row(a_log, DT_LANE), _lane_row(dt_bias, DT_LANE),
                  row(jnp.repeat(d_skip, SSD_HEAD_DIM)), row(ssd_norm_w))
    state_t = lambda s: jnp.swapaxes(s, -1, -2)

    n_p = bp * seq
    z, xbc, q_b, k_p, v_p, k_b, v_b, qi_b, kk_b, small = _project_in(x_prompt.reshape(n_p, D_MODEL), wa, wb, wc, wd)
    r3 = lambda a, b_: a.reshape(b_, -1, a.shape[-1])
    ssd_out, ssm_t, conv_p = _ssd_mixer(
        r3(z, bp), r3(xbc, bp), r3(small, bp), jnp.zeros((bp, SSD_CONV - 1, CONV_DIM), F32),
        jnp.zeros((bp, SSD_HEADS, SSD_STATE, SSD_HEAD_DIM), F32), SSD_CHUNK, *ssd_params)
    att_out = _dsa_prompt(r3(q_b, bp), r3(qi_b, bp), r3(small, bp), r3(kk_b, bp), r3(k_b, bp), r3(v_b, bp), rel_bias)
    kidx_p = r3(small, bp)[:, :, :IDX_DIM]
    mem_kv = _matmul(mem_prompt.reshape(bp * m_tok, D_MODEL),
                     jnp.concatenate([w_ck, w_cv], axis=1).astype(MXU_DTYPE))
    memk_p = mem_kv[:, :MEM_INNER].reshape(bp, m_tok, MEM_INNER)
    memv_p = mem_kv[:, MEM_INNER:].reshape(bp, m_tok, MEM_INNER)
    y_prompt = _finish(x_prompt.reshape(n_p, D_MODEL), ssd_out.reshape(n_p, SSD_INNER),
                       att_out.reshape(n_p, ATT_INNER),
                       lambda qm: _mem_attend(qm.reshape(bp, seq, MEM_INNER), memk_p, memv_p).reshape(n_p, MEM_INNER),
                       p)

    n_s = bs * t_s
    n_pad = -(-n_s // TOK_BLOCK) * TOK_BLOCK
    xs2d = jnp.pad(x_sample.reshape(n_s, D_MODEL), ((0, n_pad - n_s), (0, 0)))
    outs = _project_in(xs2d, wa, wb, wc, wd)
    z, xbc, q_b, k_s, v_s, k_b, v_b, qi_b, kk_b, small = [r3(a[:n_s], bs) for a in outs]
    pad_chunk = lambda a: jnp.pad(a, ((0, 0), (0, SSD_CHUNK - t_s), (0, 0)))
    ssd_s, ssm_s_t, conv_s = _ssd_mixer(pad_chunk(z), pad_chunk(xbc), pad_chunk(small), state_conv,
                                        state_t(state_ssm), t_s, *ssd_params)
    att_s = _dsa_sample(q_b, qi_b, small, k_b, v_b, kk_b, cache_k, cache_v, cache_kidx, page_table, rel_bias)
    pad_rows = lambda a: jnp.pad(a.reshape(n_s, -1), ((0, n_pad - n_s), (0, 0)))
    cmk = cache_mem_k.reshape(bs, m_tok, MEM_INNER)
    cmv = cache_mem_v.reshape(bs, m_tok, MEM_INNER)
    y_s = _finish(xs2d, pad_rows(ssd_s[:, :t_s]), pad_rows(att_s),
                  lambda qm: pad_rows(_mem_attend(qm[:n_s].reshape(bs, t_s, MEM_INNER), cmk, cmv)), p)
    y_sample = y_s[:n_s].reshape(bs, t_s, D_MODEL)

    heads = lambda a, b_: a.reshape(b_, -1, ATT_HEADS, ATT_HEAD_DIM)
    mem_heads = lambda a: a.reshape(bp, m_tok, MEM_HEADS, MEM_HEAD_DIM)
    return (y_prompt.reshape(bp, seq, D_MODEL), y_sample,
            heads(k_p, bp), heads(v_p, bp), kidx_p, mem_heads(memk_p), mem_heads(memv_p),
            state_t(ssm_t), conv_p,
            heads(k_s, bs), heads(v_s, bs), small[:, :, :IDX_DIM], state_t(ssm_s_t), conv_s)
```

```python
import functools
import math

import jax
import jax.numpy as jnp
import numpy as np
from jax import lax
from jax.experimental import pallas as pl
from jax.experimental.pallas import tpu as pltpu

F32 = jnp.float32
BF16 = jnp.bfloat16
I32 = jnp.int32
MXU_DTYPE = BF16

D_MODEL = 1024
DEPTH = 1
ALPHA = (2.0 * DEPTH) ** 0.25
LN_EPS = 1e-5

SSD_HEADS = 8
SSD_HEAD_DIM = 64
SSD_INNER = 512
SSD_GROUPS = 2
SSD_STATE = 128
SSD_CONV = 4
SSD_CHUNK = 128
CONV_DIM = 1024

ATT_HEADS = 8
ATT_HEAD_DIM = 64
ATT_INNER = 512
IDX_HEADS = 8
IDX_DIM = 64
TOPK_MAX = 256
PAGE_SIZE = 128
N_BUCKETS = 32
MAX_DISTANCE = 128

MEM_HEADS = 4
MEM_HEAD_DIM = 128
MEM_INNER = 512

PEER_HEADS = 8
PEER_KEYS = 128
PEER_HALF = 128
PEER_TOPK = 16
PEER_PAIRS = PEER_HEADS * PEER_TOPK

LANES = 128
CHUNK = 2 * LANES
TOK_BLOCK = 128
DT_LANE = 64
WIDX_LANE = 72
NEG_BIG = -1e30
INT_MIN = -(2 ** 31)
VMEM_LIMIT = 56 * 1024 * 1024
SLAB_ROWS = 4
FEAT_CHUNKS = D_MODEL // LANES
TILE_STRIDE = 136
GATHER_TOKENS = 8
HALF_EXPERTS = PEER_KEYS * PEER_KEYS // 2
PAIR_SLAB = 2 * SLAB_ROWS
FAST_PAIRS = 80


def _cparams(sem, vmem=VMEM_LIMIT):
    return pltpu.CompilerParams(dimension_semantics=sem, vmem_limit_bytes=vmem)


def _mm(a, b):
    return jnp.dot(a.astype(MXU_DTYPE), b.astype(MXU_DTYPE), preferred_element_type=F32)


def _mm_nt(a, b):
    return lax.dot_general(a.astype(MXU_DTYPE), b.astype(MXU_DTYPE), (((1,), (1,)), ((), ())),
                           preferred_element_type=F32)


def _mm_exact(a, b):
    return jnp.dot(a, b, preferred_element_type=F32, precision=lax.Precision.HIGHEST)


def _sigmoid(x):
    return 1.0 / (1.0 + jnp.exp(-x))


def _layer_norm(x, g, b):
    mu = jnp.mean(x, axis=-1, keepdims=True)
    xc = x - mu
    var = jnp.mean(xc * xc, axis=-1, keepdims=True)
    return xc * lax.rsqrt(var + LN_EPS) * g + b


def _bucket_thresholds():
    max_exact = N_BUCKETS // 2
    out = []
    for n in range(max_exact, MAX_DISTANCE + 1):
        v = max_exact + int(math.log(n / max_exact) / math.log(MAX_DISTANCE / max_exact) * (N_BUCKETS - max_exact))
        out.append(min(v, N_BUCKETS - 1))
    thr = []
    for bkt in range(max_exact + 1, N_BUCKETS):
        thr.append(max_exact + next(i for i, v in enumerate(out) if v >= bkt))
    return tuple(thr)


_BUCKET_THR = _bucket_thresholds()


def _rel_bucket(dist):
    n = jnp.maximum(dist, 0)
    max_exact = N_BUCKETS // 2
    large = jnp.full(n.shape, max_exact, I32)
    for t in _BUCKET_THR:
        large = large + (n >= t).astype(I32)
    return jnp.where(n < max_exact, n, large)


def _monotone_key(x):
    b = lax.bitcast_convert_type(x, I32)
    return b ^ ((b >> 31) & 0x7FFFFFFF)


def _proj_in_kernel(x_ref, wa_ref, wb_ref, wc_ref, wd_ref,
                    z_ref, xbc_ref, q_ref, k_ref, v_ref, kb_ref, vb_ref, qi_ref, kk_ref, sm_ref):
    xb = x_ref[...].astype(MXU_DTYPE)
    z_ref[...] = jnp.dot(xb, wa_ref[:, :SSD_INNER], preferred_element_type=F32)
    xbc_ref[...] = jnp.dot(xb, wa_ref[:, SSD_INNER:], preferred_element_type=F32)
    part = lambda n: wb_ref[:, n * ATT_INNER:(n + 1) * ATT_INNER]
    q = jnp.dot(xb, part(0), preferred_element_type=F32)
    q_ref[...] = (q * (ATT_HEAD_DIM ** -0.5)).astype(q_ref.dtype)
    k = jnp.dot(xb, part(1), preferred_element_type=F32)
    k_ref[...] = k
    kb_ref[...] = k.astype(kb_ref.dtype)
    v = jnp.dot(xb, part(2), preferred_element_type=F32)
    v_ref[...] = v
    vb_ref[...] = v.astype(vb_ref.dtype)
    qi = jnp.dot(xb, part(3), preferred_element_type=F32)
    qi_ref[...] = (qi * (IDX_DIM ** -0.5)).astype(qi_ref.dtype)
    kk_ref[...] = jnp.dot(xb, wc_ref[...], preferred_element_type=F32).astype(kk_ref.dtype)
    sm = jnp.dot(xb, wd_ref[...], preferred_element_type=F32)
    lane = lax.broadcasted_iota(I32, sm.shape, 1)
    is_w = (lane >= WIDX_LANE) & (lane < WIDX_LANE + IDX_HEADS)
    sm_ref[...] = jnp.where(is_w, sm * (IDX_HEADS ** -0.5), sm)


def _project_in(x2d, wa, wb, wc, wd):
    n = x2d.shape[0]
    tm = min(256, n)
    row = lambda w: pl.BlockSpec((tm, w), lambda i: (i, 0))
    full = lambda a: pl.BlockSpec(a.shape, lambda i: (0, 0))
    out_shapes = (
        jax.ShapeDtypeStruct((n, SSD_INNER), F32),
        jax.ShapeDtypeStruct((n, CONV_DIM), F32),
        jax.ShapeDtypeStruct((n, ATT_INNER), MXU_DTYPE),
        jax.ShapeDtypeStruct((n, ATT_INNER), F32),
        jax.ShapeDtypeStruct((n, ATT_INNER), F32),
        jax.ShapeDtypeStruct((n, ATT_INNER), MXU_DTYPE),
        jax.ShapeDtypeStruct((n, ATT_INNER), MXU_DTYPE),
        jax.ShapeDtypeStruct((n, ATT_INNER), MXU_DTYPE),
        jax.ShapeDtypeStruct((n, LANES), MXU_DTYPE),
        jax.ShapeDtypeStruct((n, LANES), F32),
    )
    return pl.pallas_call(
        _proj_in_kernel,
        grid=(n // tm,),
        in_specs=[row(D_MODEL), full(wa), full(wb), full(wc), full(wd)],
        out_specs=tuple(row(s.shape[1]) for s in out_shapes),
        out_shape=out_shapes,
        compiler_params=_cparams(("parallel",)),
        name="project_in",
    )(x2d, wa, wb, wc, wd)


def _ssd_kernel(n_valid, z_ref, xbc_ref, sm_ref, conv0_ref, h0_ref, convw_ref, convb_ref, alog_ref, dtb_ref,
                dskip_ref, normw_ref, y_ref, hout_ref, tail_ref, cbuf, hst, ysc):
    q_len = z_ref.shape[1]
    c = pl.program_id(1)
    pad = 8

    @pl.when(c == 0)
    def _():
        cbuf[pad - 3:pad, :] = conv0_ref[0]
        hst[...] = h0_ref[0]

    xbc = xbc_ref[0]
    cbuf[pad:pad + q_len, :] = xbc
    w = convw_ref[...]
    conv = cbuf[pad - 3:pad - 3 + q_len, :] * w[0:1]
    conv = conv + cbuf[pad - 2:pad - 2 + q_len, :] * w[1:2]
    conv = conv + cbuf[pad - 1:pad - 1 + q_len, :] * w[2:3]
    conv = conv + xbc * w[3:4]
    conv = conv + convb_ref[...]
    xc = conv * _sigmoid(conv)
    tail = cbuf[pad + n_valid - 3:pad + n_valid, :]
    cbuf[pad - 3:pad, :] = tail

    row = lax.broadcasted_iota(I32, (q_len, LANES), 0)
    lane = lax.broadcasted_iota(I32, (q_len, LANES), 1)
    is_dt = (lane >= DT_LANE) & (lane < DT_LANE + SSD_HEADS) & (row < n_valid)
    raw = sm_ref[0] + dtb_ref[...]
    dt_full = jnp.maximum(raw, 0.0) + jnp.log1p(jnp.exp(-jnp.abs(raw)))
    dtm = jnp.where(is_dt, dt_full, 0.0)
    a_full = dtm * (-jnp.exp(alog_ref[...]))
    ri = lax.broadcasted_iota(I32, (q_len, q_len), 0)
    ci = lax.broadcasted_iota(I32, (q_len, q_len), 1)
    causal = ci <= ri
    acum = _mm_exact(causal.astype(F32), a_full)
    acum_t = acum.T

    xs = xc[:, :SSD_INNER]
    for g in range(SSD_GROUPS):
        bm = xc[:, SSD_INNER + g * SSD_STATE:SSD_INNER + (g + 1) * SSD_STATE]
        cm = xc[:, SSD_INNER + (SSD_GROUPS + g) * SSD_STATE:SSD_INNER + (SSD_GROUPS + g + 1) * SSD_STATE]
        scores = _mm_nt(cm, bm)
        bm_t = bm.T
        for hh in range(SSD_HEADS // SSD_GROUPS):
            h = g * (SSD_HEADS // SSD_GROUPS) + hh
            col = acum[:, DT_LANE + h:DT_LANE + h + 1]
            rowv = acum_t[DT_LANE + h:DT_LANE + h + 1, :]
            a_last = acum[q_len - 1:q_len, DT_LANE + h:DT_LANE + h + 1]
            decay = jnp.exp(jnp.where(causal, col - rowv, -jnp.inf))
            x_h = xs[:, h * SSD_HEAD_DIM:(h + 1) * SSD_HEAD_DIM]
            xd = x_h * dtm[:, DT_LANE + h:DT_LANE + h + 1]
            h_t = hst[h]
            y = _mm(scores * decay, xd) + _mm(cm, h_t) * jnp.exp(col)
            y = y + x_h * dskip_ref[:, h * SSD_HEAD_DIM:(h + 1) * SSD_HEAD_DIM]
            ysc[:, h * SSD_HEAD_DIM:(h + 1) * SSD_HEAD_DIM] = y
            hst[h] = h_t * jnp.exp(a_last) + _mm(bm_t, xd * jnp.exp(a_last - col))

    zz = z_ref[0]
    hg = ysc[...] * (zz * _sigmoid(zz))
    hg = hg * lax.rsqrt(jnp.mean(hg * hg, axis=-1, keepdims=True) + LN_EPS)
    y_ref[0] = hg * normw_ref[...]

    @pl.when(c == pl.num_programs(1) - 1)
    def _():
        hout_ref[0] = hst[...]
        tail_ref[0] = cbuf[pad - 3:pad, :]


def _ssd_mixer(z, xbc, small, conv0, h0_t, n_valid, conv_w, conv_b, alog_lane, dtb_lane, dskip_row, norm_w):
    b, length, _ = z.shape
    q_len = SSD_CHUNK
    nc = length // q_len
    blk = lambda w: pl.BlockSpec((1, q_len, w), lambda i, c: (i, c, 0))
    per_b3 = lambda s: pl.BlockSpec((1,) + s, lambda i, c: (i,) + (0,) * len(s))
    const2 = lambda a: pl.BlockSpec(a.shape, lambda i, c: (0, 0))
    st_shape = (SSD_HEADS, SSD_STATE, SSD_HEAD_DIM)
    return pl.pallas_call(
        functools.partial(_ssd_kernel, n_valid),
        grid=(b, nc),
        in_specs=[blk(SSD_INNER), blk(CONV_DIM), blk(LANES), per_b3((SSD_CONV - 1, CONV_DIM)), per_b3(st_shape),
                  const2(conv_w), const2(conv_b), const2(alog_lane), const2(dtb_lane), const2(dskip_row),
                  const2(norm_w)],
        out_specs=(blk(SSD_INNER), per_b3(st_shape), per_b3((SSD_CONV - 1, CONV_DIM))),
        out_shape=(jax.ShapeDtypeStruct((b, length, SSD_INNER), F32),
                   jax.ShapeDtypeStruct((b,) + st_shape, F32),
                   jax.ShapeDtypeStruct((b, SSD_CONV - 1, CONV_DIM), F32)),
        scratch_shapes=[pltpu.VMEM((q_len + 8, CONV_DIM), F32), pltpu.VMEM(st_shape, F32),
                        pltpu.VMEM((q_len, SSD_INNER), F32)],
        compiler_params=_cparams(("parallel", "arbitrary")),
        name="ssd_mixer",
    )(z, xbc, small, conv0, h0_t, conv_w, conv_b, alog_lane, dtb_lane, dskip_row, norm_w)


def _kth_largest_key(count_ge, k, rows):
    def body(i, tau):
        cand = tau + jnp.left_shift(jnp.int32(1), 31 - i)
        return jnp.where(count_ge(cand) >= k, cand, tau)
    return lax.fori_loop(0, 32, body, jnp.full((rows, 1), INT_MIN, I32))


def _bias_tiles(rb_ref, bias_sc, n_tiles):
    a = lax.broadcasted_iota(I32, (LANES, LANES), 0)
    bcol = lax.broadcasted_iota(I32, (LANES, LANES), 1)
    for d in range(n_tiles):
        bucket = _rel_bucket(d * LANES + a - bcol)
        for h in range(ATT_HEADS):
            far = rb_ref[N_BUCKETS - 1, h]
            tile = jnp.zeros((LANES, LANES), F32)
            for kb in range(N_BUCKETS):
                tile = jnp.where(bucket == kb, rb_ref[kb, h] - far, tile)
            bias_sc[d, h] = tile


def _dsa_prompt_kernel(topk, q_ref, qi_ref, sm_ref, kk_ref, kb_ref, vb_ref, rb_ref, o_ref,
                       keys_sc, bias_sc, qm_sc, qim_sc, wb_sc, *head_sc):
    nh = ATT_HEADS
    cache, m_l, l_l, acc = head_sc[:nh], head_sc[nh:2 * nh], head_sc[2 * nh:3 * nh], head_sc[3 * nh:]
    i = pl.program_id(1)
    tq = q_ref.shape[1]
    n_far = bias_sc.shape[0] - 1
    n_chunks = (i + 2) // 2

    @pl.when(i == 0)
    def _():
        _bias_tiles(rb_ref, bias_sc, n_far + 1)

    lane = lax.broadcasted_iota(I32, (tq, LANES), 1)
    low_half = lane < ATT_HEAD_DIM
    lane2 = lax.broadcasted_iota(I32, (tq, CHUNK), 1)
    row2 = lax.broadcasted_iota(I32, (tq, CHUNK), 0)
    sm = sm_ref[0]

    for h in range(nh):
        keep = low_half if h % 2 == 0 else jnp.logical_not(low_half)
        pair = slice((h // 2) * LANES, (h // 2 + 1) * LANES)
        qp, qip = q_ref[0, :, pair], qi_ref[0, :, pair]
        qm_sc[h] = jnp.where(keep, qp, jnp.zeros_like(qp))
        qim_sc[h] = jnp.where(keep, qip, jnp.zeros_like(qip))
        wb_sc[h] = jnp.broadcast_to(sm[:, WIDX_LANE + h:WIDX_LANE + h + 1], (tq, LANES))
        m_l[h][...] = jnp.full((tq, LANES), NEG_BIG, F32)
        l_l[h][...] = jnp.zeros((tq, LANES), F32)
        acc[h][...] = jnp.zeros((tq, LANES), F32)

    def causal(c):
        return (c * CHUNK + lane2) <= (i * tq + row2)

    def chunk_logits(c, carry):
        start = pl.multiple_of(c * CHUNK, CHUNK)
        kk = kk_ref[0, pl.ds(start, CHUNK), :]
        sc = jnp.zeros((tq, CHUNK), F32)
        for h in range(IDX_HEADS):
            wb = wb_sc[h]
            sc = sc + jnp.maximum(_mm_nt(qim_sc[h], kk), 0.0) * jnp.concatenate([wb, wb], axis=1)
        keys_sc[c] = _monotone_key(jnp.where(causal(c), sc, -jnp.inf))
        d0 = jnp.clip(i - 2 * c, 0, n_far)
        d1 = jnp.clip(i - 2 * c - 1, 0, n_far)
        for h in range(nh):
            kblk = kb_ref[0, pl.ds(start, CHUNK), (h // 2) * LANES:(h // 2 + 1) * LANES]
            bias = jnp.concatenate([bias_sc[d0, h], bias_sc[d1, h]], axis=1)
            cache[h][c] = _mm_nt(qm_sc[h], kblk) + bias
        return carry

    lax.fori_loop(0, n_chunks, chunk_logits, 0)

    def count_ge(cand):
        def body(c, cnt):
            k = keys_sc[c]
            return cnt + jnp.where(k[:, :LANES] >= cand, 1.0, 0.0) + jnp.where(k[:, LANES:] >= cand, 1.0, 0.0)
        cnt = lax.fori_loop(0, n_chunks, body, jnp.zeros((tq, LANES), F32))
        return jnp.sum(cnt, axis=1, keepdims=True)

    tau = _kth_largest_key(count_ge, float(topk), tq)

    def chunk_max(c, carry):
        sel = (keys_sc[c] >= tau) & causal(c)
        for h in range(nh):
            s = jnp.where(sel, cache[h][c], NEG_BIG)
            cache[h][c] = s
            m_l[h][...] = jnp.maximum(m_l[h][...], jnp.maximum(s[:, :LANES], s[:, LANES:]))
        return carry

    lax.fori_loop(0, n_chunks, chunk_max, 0)
    for h in range(nh):
        m_l[h][...] = jnp.broadcast_to(jnp.max(m_l[h][...], axis=1, keepdims=True), (tq, LANES))

    def chunk_pv(c, carry):
        start = pl.multiple_of(c * CHUNK, CHUNK)
        for h in range(nh):
            mb = m_l[h][...]
            s = cache[h][c]
            p0 = jnp.exp(s[:, :LANES] - mb)
            p1 = jnp.exp(s[:, LANES:] - mb)
            l_l[h][...] = l_l[h][...] + (p0 + p1)
            vblk = vb_ref[0, pl.ds(start, CHUNK), (h // 2) * LANES:(h // 2 + 1) * LANES]
            acc[h][...] = acc[h][...] + _mm(jnp.concatenate([p0, p1], axis=1), vblk)
        return carry

    lax.fori_loop(0, n_chunks, chunk_pv, 0)

    for g in range(nh // 2):
        lo = acc[2 * g][...] / jnp.sum(l_l[2 * g][...], axis=1, keepdims=True)
        hi = acc[2 * g + 1][...] / jnp.sum(l_l[2 * g + 1][...], axis=1, keepdims=True)
        o_ref[0, :, g * LANES:(g + 1) * LANES] = jnp.where(low_half, lo, hi)


def _dsa_prompt(q_b, qi_b, small, kk_b, k_b, v_b, rel_bias):
    b, s, _ = q_b.shape
    tq = LANES
    nq = s // tq
    topk = min(TOPK_MAX, s // 4)
    n_bias_tiles = 3
    assert MAX_DISTANCE <= LANES + 1 and s % CHUNK == 0
    n_chunks = s // CHUNK
    qblk = lambda w: pl.BlockSpec((1, tq, w), lambda i, j: (i, j, 0))
    per_b = lambda w: pl.BlockSpec((1, s, w), lambda i, j: (i, 0, 0))
    per_head = lambda shape, dt: [pltpu.VMEM(shape, dt) for _ in range(ATT_HEADS)]
    return pl.pallas_call(
        functools.partial(_dsa_prompt_kernel, topk),
        grid=(b, nq),
        in_specs=[qblk(ATT_INNER), qblk(ATT_INNER), qblk(LANES), per_b(LANES), per_b(ATT_INNER), per_b(ATT_INNER),
                  pl.BlockSpec(memory_space=pltpu.SMEM)],
        out_specs=qblk(ATT_INNER),
        out_shape=jax.ShapeDtypeStruct((b, s, ATT_INNER), F32),
        scratch_shapes=[pltpu.VMEM((n_chunks, tq, CHUNK), I32),
                        pltpu.VMEM((n_bias_tiles, ATT_HEADS, LANES, LANES), F32),
                        pltpu.VMEM((ATT_HEADS, tq, LANES), q_b.dtype), pltpu.VMEM((ATT_HEADS, tq, LANES), qi_b.dtype),
                        pltpu.VMEM((ATT_HEADS, tq, LANES), F32)]
                       + per_head((n_chunks, tq, CHUNK), F32)
                       + per_head((tq, LANES), F32)
                       + per_head((tq, LANES), F32)
                       + per_head((tq, LANES), F32),
        compiler_params=_cparams(("parallel", "arbitrary")),
        name="dsa_prompt",
    )(q_b, qi_b, small, kk_b, k_b, v_b, rel_bias)


def _dsa_sample_score_kernel(topk, t_new, pt_ref, qi_ref, w_ref, *refs):
    kidx_refs, (kknew_ref, keys_ref, keysnew_ref, tau_ref, keys_sc) = refs[:-5], refs[-5:]
    pg = len(kidx_refs)
    j = pl.program_id(1)
    n_pages = pl.num_programs(1) * pg
    rows = qi_ref.shape[1]
    lane = lax.broadcasted_iota(I32, (t_new, LANES), 1)
    rowi = lax.broadcasted_iota(I32, (t_new, LANES), 0)

    def head_sum(s):
        s = jnp.maximum(s, 0.0) * w_ref[0]
        return jnp.concatenate(
            [jnp.sum(s[t * IDX_HEADS:(t + 1) * IDX_HEADS], axis=0, keepdims=True) for t in range(t_new)], axis=0)

    for r, kidx_ref in enumerate(kidx_refs):
        keys_sc[j * pg + r] = _monotone_key(head_sum(_mm(qi_ref[0], kidx_ref[0])))

    @pl.when(j == pl.num_programs(1) - 1)
    def _():
        sc = head_sum(_mm(qi_ref[0], kknew_ref[0]))
        keys_new = _monotone_key(jnp.where(lane <= rowi, sc, -jnp.inf))
        keys_sc[n_pages] = keys_new

        def count_ge(cand):
            def body(jj, cnt):
                return cnt + jnp.where(keys_sc[jj] >= cand, 1.0, 0.0)
            cnt = lax.fori_loop(0, n_pages + 1, body, jnp.zeros((t_new, LANES), F32))
            return jnp.sum(cnt, axis=1, keepdims=True)

        tau = _kth_largest_key(count_ge, float(topk), t_new)
        tau_ref[0] = jnp.broadcast_to(tau, (t_new, LANES))
        keys_ref[0] = keys_sc[0:n_pages]
        keysnew_ref[0] = keys_new


def _dsa_sample_attn_kernel(t_new, past, pg, pt_ref, q_ref, keys_ref, keysnew_ref, tau_ref, knew_ref, vnew_ref,
                            rbrow_ref, *refs):
    k_refs, v_refs = refs[:pg], refs[pg:2 * pg]
    o_ref, cache_sc, m_sc, l_sc, acc_sc = refs[2 * pg:]
    phase = pl.program_id(1)
    j = pl.program_id(2)
    n_steps = pl.num_programs(2)
    n_pages = n_steps * pg
    rows = q_ref.shape[1]
    lane = lax.broadcasted_iota(I32, (rows, LANES), 1)
    rowi = lax.broadcasted_iota(I32, (rows, LANES), 0)
    tok = lax.rem(rowi, t_new)
    q = q_ref[0]

    def rows_of(x_t):
        return jnp.concatenate([x_t] * ATT_HEADS, axis=0)

    tau = rows_of(tau_ref[0])

    def page_rows(ref):
        return ref[0].reshape(ATT_INNER, PAGE_SIZE)

    def near_bias(key_pos):
        bucket = _rel_bucket(past + tok - key_pos)
        far = rbrow_ref[:, N_BUCKETS - 1:N_BUCKETS]
        bias = jnp.zeros((rows, LANES), F32)
        for kb in range(N_BUCKETS - 1):
            bias = jnp.where(bucket == kb, rbrow_ref[:, kb:kb + 1] - far, bias)
        return bias

    @pl.when(phase == 0)
    def _():
        for r in range(pg):
            s = _mm(q, page_rows(k_refs[r]))
            cache_sc[j * pg + r] = jnp.where(rows_of(keys_ref[0, r]) >= tau, s, NEG_BIG)

        @pl.when(j == n_steps - 1)
        def _():
            last = n_pages - 1
            cache_sc[last] = cache_sc[last] + near_bias(last * PAGE_SIZE + lane)
            s = _mm(q, knew_ref[0]) + near_bias(past + lane)
            ok = (rows_of(keysnew_ref[0]) >= tau) & (lane <= tok)
            cache_sc[n_pages] = jnp.where(ok, s, NEG_BIG)
            m = lax.fori_loop(0, n_pages + 1, lambda jj, m: jnp.maximum(m, cache_sc[jj]),
                              jnp.full((rows, LANES), NEG_BIG, F32))
            m_sc[...] = jnp.broadcast_to(jnp.max(m, axis=1, keepdims=True), (rows, LANES))
            l_sc[...] = jnp.zeros(l_sc.shape, F32)
            acc_sc[...] = jnp.zeros(acc_sc.shape, F32)

    @pl.when(phase == 1)
    def _():
        mb = m_sc[...]
        acc = acc_sc[...]
        lsum = l_sc[...]
        for r in range(pg):
            p = jnp.exp(cache_sc[j * pg + r] - mb)
            lsum = lsum + p
            acc = acc + _mm_nt(p, page_rows(v_refs[r]))
        l_sc[...] = lsum
        acc_sc[...] = acc

        @pl.when(j == n_steps - 1)
        def _():
            p_new = jnp.exp(cache_sc[n_pages] - mb)
            l = jnp.sum(lsum + p_new, axis=1, keepdims=True)
            a = (acc + _mm_nt(p_new, vnew_ref[0])) / l
            col_head = lax.broadcasted_iota(I32, (t_new, ATT_INNER), 1) // ATT_HEAD_DIM
            out = jnp.zeros((t_new, ATT_INNER), F32)
            for h in range(ATT_HEADS):
                out = jnp.where(col_head == h, a[h * t_new:(h + 1) * t_new], out)
            o_ref[0] = out


def _dsa_sample(q_b, qi_b, small, k_b, v_b, kk_b, cache_k, cache_v, cache_kidx, page_table, rel_bias):
    b, t_new, _ = q_b.shape
    n_pages = page_table.shape[1]
    past = n_pages * PAGE_SIZE
    topk = min(TOPK_MAX, (past + t_new) // 4)
    rows = t_new * ATT_HEADS
    assert PAGE_SIZE >= MAX_DISTANCE

    def pages_per_step(limit):
        return max(g for g in (1, 2, 4, 8, 16, 32) if g <= limit and n_pages % g == 0)

    kidx_t = jnp.transpose(cache_kidx, (0, 2, 1))
    k_t = jnp.transpose(cache_k, (0, 2, 3, 1))
    v_t = jnp.transpose(cache_v, (0, 2, 3, 1))
    new_cols = lambda x: jnp.pad(jnp.swapaxes(x, 1, 2), ((0, 0), (0, 0), (0, PAGE_SIZE - t_new)))

    pgs = pages_per_step(32)
    qi_rows = qi_b.reshape(b, rows, IDX_DIM)
    w_rows = small[:, :, WIDX_LANE:WIDX_LANE + IDX_HEADS].reshape(b, rows, 1)
    kk_new = new_cols(kk_b[:, :, :IDX_DIM])
    per_b = lambda s: pl.BlockSpec((1,) + s, lambda i, j, pt: (i,) + (0,) * len(s))
    kidx_page = lambda r: pl.BlockSpec((1, IDX_DIM, PAGE_SIZE), lambda i, j, pt: (pt[i, j * pgs + r], 0, 0))
    keys, keys_new, tau = pl.pallas_call(
        functools.partial(_dsa_sample_score_kernel, topk, t_new),
        grid_spec=pltpu.PrefetchScalarGridSpec(
            num_scalar_prefetch=1, grid=(b, n_pages // pgs),
            in_specs=[per_b((rows, IDX_DIM)), per_b((rows, 1))] + [kidx_page(r) for r in range(pgs)]
                     + [per_b((IDX_DIM, LANES))],
            out_specs=(per_b((n_pages, t_new, LANES)), per_b((t_new, LANES)), per_b((t_new, LANES))),
            scratch_shapes=[pltpu.VMEM((n_pages + 1, t_new, LANES), I32)]),
        out_shape=(jax.ShapeDtypeStruct((b, n_pages, t_new, LANES), I32),
                   jax.ShapeDtypeStruct((b, t_new, LANES), I32),
                   jax.ShapeDtypeStruct((b, t_new, LANES), I32)),
        compiler_params=_cparams(("parallel", "arbitrary")),
        name="dsa_sample_scores",
    )(page_table, qi_rows, w_rows, *([kidx_t] * pgs), kk_new)

    pg = pages_per_step(8)
    n_steps = n_pages // pg
    head_of_col = jnp.arange(ATT_INNER) // ATT_HEAD_DIM
    q_heads = jnp.where(head_of_col[None, None, None, :] == jnp.arange(ATT_HEADS)[None, :, None, None],
                        q_b[:, None, :, :], jnp.zeros((), q_b.dtype))
    q_rows = q_heads.reshape(b, rows, ATT_INNER)
    rb_rows = jnp.repeat(rel_bias.T, t_new, axis=0)
    k_new, v_new = new_cols(k_b), new_cols(v_b)
    per_b = lambda s: pl.BlockSpec((1,) + s, lambda i, ph, j, pt: (i,) + (0,) * len(s))
    page_blk = (1, ATT_HEADS, ATT_HEAD_DIM, PAGE_SIZE)
    k_page = lambda r: pl.BlockSpec(
        page_blk, lambda i, ph, j, pt: (pt[i, jnp.where(ph == 0, j, n_steps - 1) * pg + r], 0, 0, 0))
    v_page = lambda r: pl.BlockSpec(
        page_blk, lambda i, ph, j, pt: (pt[i, jnp.where(ph == 1, j, 0) * pg + r], 0, 0, 0))
    return pl.pallas_call(
        functools.partial(_dsa_sample_attn_kernel, t_new, past, pg),
        grid_spec=pltpu.PrefetchScalarGridSpec(
            num_scalar_prefetch=1, grid=(b, 2, n_steps),
            in_specs=[per_b((rows, ATT_INNER)),
                      pl.BlockSpec((1, pg, t_new, LANES),
                                   lambda i, ph, j, pt: (i, jnp.where(ph == 0, j, n_steps - 1), 0, 0)),
                      per_b((t_new, LANES)), per_b((t_new, LANES)),
                      per_b((ATT_INNER, LANES)), per_b((ATT_INNER, LANES)),
                      pl.BlockSpec((rows, N_BUCKETS), lambda i, ph, j, pt: (0, 0))]
                     + [k_page(r) for r in range(pg)] + [v_page(r) for r in range(pg)],
            out_specs=per_b((t_new, ATT_INNER)),
            scratch_shapes=[pltpu.VMEM((n_pages + 1, rows, LANES), F32), pltpu.VMEM((rows, LANES), F32),
                            pltpu.VMEM((rows, LANES), F32), pltpu.VMEM((rows, ATT_INNER), F32)]),
        out_shape=jax.ShapeDtypeStruct((b, t_new, ATT_INNER), F32),
        compiler_params=_cparams(("parallel", "arbitrary", "arbitrary")),
        name="dsa_sample_attend",
    )(page_table, q_rows, keys, keys_new, tau, k_new, v_new, rb_rows, *([k_t] * pg), *([v_t] * pg))


def _matmul_kernel(x_ref, w_ref, o_ref):
    o_ref[...] = _mm(x_ref[...], w_ref[...])


def _matmul(x2d, w):
    n, kdim = x2d.shape
    tm = min(256, n)
    return pl.pallas_call(
        _matmul_kernel,
        grid=(n // tm,),
        in_specs=[pl.BlockSpec((tm, kdim), lambda i: (i, 0)), pl.BlockSpec(w.shape, lambda i: (0, 0))],
        out_specs=pl.BlockSpec((tm, w.shape[1]), lambda i: (i, 0)),
        out_shape=jax.ShapeDtypeStruct((n, w.shape[1]), F32),
        compiler_params=_cparams(("parallel",)),
        name="mem_kv_proj",
    )(x2d, w)


def _out_proj_kernel(x_ref, ssd_ref, att_ref, wout_ref, g_ref, b_ref, wcq_ref, h1_ref, qm_ref):
    m = _mm(ssd_ref[...], wout_ref[:SSD_INNER, :]) + _mm(att_ref[...], wout_ref[SSD_INNER:, :])
    h1 = _layer_norm(ALPHA * x_ref[...] + m, g_ref[...], b_ref[...])
    h1_ref[...] = h1
    qm_ref[...] = _mm(h1, wcq_ref[...])


def _out_proj(x2d, ssd2d, att2d, w_out, g, b, w_cq):
    n = x2d.shape[0]
    tm = min(256, n)
    row = lambda w: pl.BlockSpec((tm, w), lambda i: (i, 0))
    full = lambda a: pl.BlockSpec(a.shape, lambda i: (0, 0))
    return pl.pallas_call(
        _out_proj_kernel,
        grid=(n // tm,),
        in_specs=[row(D_MODEL), row(SSD_INNER), row(ATT_INNER), full(w_out), full(g), full(b), full(w_cq)],
        out_specs=(row(D_MODEL), row(MEM_INNER)),
        out_shape=(jax.ShapeDtypeStruct((n, D_MODEL), F32), jax.ShapeDtypeStruct((n, MEM_INNER), F32)),
        compiler_params=_cparams(("parallel",)),
        name="out_proj_ln1",
    )(x2d, ssd2d, att2d, w_out, g, b, w_cq)


def _mem_attend_kernel(q_ref, mk_ref, mv_ref, o_ref):
    q = q_ref[0]
    for h in range(MEM_HEADS):
        sl = slice(h * MEM_HEAD_DIM, (h + 1) * MEM_HEAD_DIM)
        logits = _mm_nt(q[:, sl], mk_ref[0, :, sl]) * (MEM_HEAD_DIM ** -0.5)
        logits = logits - jnp.max(logits, axis=-1, keepdims=True)
        p = jnp.exp(logits)
        p = p / jnp.sum(p, axis=-1, keepdims=True)
        o_ref[0, :, sl] = _mm(p, mv_ref[0, :, sl])


def _mem_attend(qm, mk, mv):
    b, length, _ = qm.shape
    tq = min(512, length)
    m_tok = mk.shape[1]
    return pl.pallas_call(
        _mem_attend_kernel,
        grid=(b, length // tq),
        in_specs=[pl.BlockSpec((1, tq, MEM_INNER), lambda i, j: (i, j, 0)),
                  pl.BlockSpec((1, m_tok, MEM_INNER), lambda i, j: (i, 0, 0)),
                  pl.BlockSpec((1, m_tok, MEM_INNER), lambda i, j: (i, 0, 0))],
        out_specs=pl.BlockSpec((1, tq, MEM_INNER), lambda i, j: (i, j, 0)),
        out_shape=jax.ShapeDtypeStruct((b, length, MEM_INNER), F32),
        compiler_params=_cparams(("parallel", "parallel")),
        name="mem_attend",
    )(qm, mk, mv)


def _top_rows(s, n_top, with_index):
    rows = s.shape[0]
    ridx = lax.broadcasted_iota(I32, s.shape, 0).astype(F32)
    vals, idxs = [], []
    for _ in range(n_top):
        m = jnp.max(s, axis=0, keepdims=True)
        first = jnp.min(jnp.where(s == m, ridx, float(rows)), axis=0, keepdims=True)
        s = jnp.where(ridx == first, -jnp.inf, s)
        vals.append(m)
        idxs.append(first)
    return jnp.concatenate(vals, axis=0), (jnp.concatenate(idxs, axis=0) if with_index else None)


_SUBLANES = 8
ROUTE_HEADS_PER_TRIP = 8


def _staircase_blocks(n):
    by_b, single_b = [], []
    for a in range(n):
        count = n // (a + 1)
        if count > 1:
            by_b += [(a, b) for b in range(0, count, _SUBLANES)]
        else:
            single_b.append(a)
    assert len(single_b) % _SUBLANES == 0 and single_b == list(range(n - len(single_b), n))
    return tuple(by_b), tuple(single_b[::_SUBLANES])


_STAIRCASE = _staircase_blocks(PEER_TOPK)


def _peer_route_kernel(h1_ref, o_ref, wco_ref, g_ref, b_ref, wq_ref, sk_ref, h2_ref, e_ref, gate_ref, h2b_sc):
    h2 = _layer_norm(ALPHA * h1_ref[...] + _mm(o_ref[...], wco_ref[...]), g_ref[...], b_ref[...])
    h2_ref[...] = h2
    h2b_sc[...] = h2.astype(MXU_DTYPE)

    def route_head(h, carry):
        tops = []
        for p in range(2):
            c = h * 2 + p
            qs = jnp.dot(h2b_sc[...], wq_ref[c], preferred_element_type=F32)
            s_t = _mm_nt(sk_ref[c], qs)
            tops.append(_top_rows(s_t, PEER_TOPK, True))
        (v0, i0), (v1, i1) = tops
        by_b, by_a = _STAIRCASE
        sub = _SUBLANES
        cand = jnp.concatenate([v0[a:a + 1] + v1[b:b + sub] for a, b in by_b]
                               + [v0[a:a + sub] + v1[0:1] for a in by_a], axis=0)
        cidx = jnp.concatenate([i0[a:a + 1] * PEER_KEYS + i1[b:b + sub] for a, b in by_b]
                               + [i0[a:a + sub] * PEER_KEYS + i1[0:1] for a in by_a], axis=0)
        pos = lax.broadcasted_iota(I32, cand.shape, 0).astype(F32)
        gs, es = [], []
        for _ in range(PEER_TOPK):
            m = jnp.max(cand, axis=0, keepdims=True)
            first = jnp.min(jnp.where(cand == m, pos, float(cand.shape[0])), axis=0, keepdims=True)
            hit = pos == first
            es.append(jnp.max(jnp.where(hit, cidx, -1.0), axis=0, keepdims=True))
            cand = jnp.where(hit, -jnp.inf, cand)
            gs.append(m)
        g_s = jnp.concatenate(gs, axis=0)
        ex = jnp.exp(g_s - g_s[0:1])
        rows = pl.ds(pl.multiple_of(h * PEER_TOPK, PEER_TOPK), PEER_TOPK)
        gate_ref[0, rows, :] = ex / jnp.sum(ex, axis=0, keepdims=True)
        e_ref[0, rows, :] = jnp.concatenate(es, axis=0).astype(I32) * SLAB_ROWS
        return carry

    def route_group(g, carry):
        for k in range(ROUTE_HEADS_PER_TRIP):
            route_head(g * ROUTE_HEADS_PER_TRIP + k, carry)
        return carry

    lax.fori_loop(0, PEER_HEADS // ROUTE_HEADS_PER_TRIP, route_group, 0)


def _peer_route(h1, o, w_co, g, b, wq, subkeys):
    n = h1.shape[0]
    tm = TOK_BLOCK
    nb = n // tm
    row = lambda w: pl.BlockSpec((tm, w), lambda i: (i, 0))
    full = lambda a: pl.BlockSpec(a.shape, lambda i: (0,) * a.ndim)
    pairs = pl.BlockSpec((1, PEER_PAIRS, tm), lambda i: (i, 0, 0))
    return pl.pallas_call(
        _peer_route_kernel,
        grid=(nb,),
        in_specs=[row(D_MODEL), row(MEM_INNER), full(w_co), full(g), full(b), full(wq), full(subkeys)],
        out_specs=(row(D_MODEL), pairs, pairs),
        out_shape=(jax.ShapeDtypeStruct((n, D_MODEL), F32),
                   jax.ShapeDtypeStruct((nb, PEER_PAIRS, tm), I32),
                   jax.ShapeDtypeStruct((nb, PEER_PAIRS, tm), F32)),
        scratch_shapes=[pltpu.VMEM((tm, D_MODEL), MXU_DTYPE)],
        compiler_params=_cparams(("parallel",)),
        name="ln2_peer_route",
    )(h1, o, w_co, g, b, wq, subkeys)


def _pack_table(t):
    tb = lax.bitcast_convert_type(t.astype(BF16), jnp.uint16).astype(jnp.uint32)
    half = t.shape[1] // 2
    words = tb[:, :half] | (tb[:, half:] << 16)
    return lax.bitcast_convert_type(words, I32).reshape(t.shape[0] * SLAB_ROWS, LANES)


def _unpack_words(w):
    lo = lax.bitcast_convert_type(w << 16, F32)
    hi = lax.bitcast_convert_type(w & jnp.int32(-65536), F32)
    return lo, hi


def _gelu_tanh(x):
    return 0.5 * x * (1.0 + jnp.tanh(math.sqrt(2.0 / math.pi) * (x + 0.044715 * (x * x * x))))


def _compact_rows(keep, shift, vals):
    n = keep.shape[0]
    shift = jnp.where(keep > 0, shift, 0)
    step = 1
    while step < n:
        up = lambda a: pltpu.roll(a, n - step, axis=0)
        moving = jnp.where((keep > 0) & ((shift & step) != 0), 1, 0)
        arrive = up(moving) > 0
        vals = [jnp.where(arrive, up(v), v) for v in vals]
        shift = jnp.where(arrive, up(shift), shift)
        keep = jnp.where(arrive, 1, keep - moving)
        step *= 2
    return vals


def _partition_kernel(e_ref, g_ref, ea_ref, ga_ref, eb_ref, gb_ref, n_ref):
    e = e_ref[0]
    n = e.shape[0]
    row = lax.broadcasted_iota(I32, e.shape, 0)
    in_b = e >= HALF_EXPERTS * SLAB_ROWS
    flag_b = jnp.where(in_b, 1.0, 0.0)
    r2 = lax.broadcasted_iota(I32, (n, n), 0)
    c2 = lax.broadcasted_iota(I32, (n, n), 1)
    b_above = _mm(jnp.where(c2 < r2, 1.0, 0.0), flag_b).astype(I32)
    n_b = jnp.sum(flag_b, axis=0, keepdims=True).astype(I32)
    n_a = n - n_b
    keep_a = jnp.where(in_b, 0, 1)
    g_bits = lax.bitcast_convert_type(g_ref[0], I32)
    ea, ga = _compact_rows(keep_a, b_above, [e, g_bits])
    eb, gb = _compact_rows(1 - keep_a, row - b_above, [e - HALF_EXPERTS * SLAB_ROWS, g_bits])
    rows_per_id = PAIR_SLAB // SLAB_ROWS
    ea_ref[0] = jnp.where(row < n_a, ea, 0) * rows_per_id
    ga_ref[0] = jnp.where(row < n_a, lax.bitcast_convert_type(ga, F32), 0.0)
    eb_ref[0] = jnp.where(row < n_b, eb, 0) * rows_per_id
    gb_ref[0] = jnp.where(row < n_b, lax.bitcast_convert_type(gb, F32), 0.0)
    n_ref[0] = jnp.concatenate([n_a, n_b] + [jnp.zeros_like(n_a)] * (_SUBLANES - 2), axis=0)


def _peer_partition(e_t, gate_t):
    nb, npairs, tm = e_t.shape
    blk = pl.BlockSpec((1, npairs, tm), lambda i: (i, 0, 0))
    cnt = pl.BlockSpec((1, _SUBLANES, tm), lambda i: (i, 0, 0))
    return pl.pallas_call(
        _partition_kernel,
        grid=(nb,),
        in_specs=[blk, blk],
        out_specs=(blk, blk, blk, blk, cnt),
        out_shape=(jax.ShapeDtypeStruct(e_t.shape, I32), jax.ShapeDtypeStruct(e_t.shape, F32),
                   jax.ShapeDtypeStruct(e_t.shape, I32), jax.ShapeDtypeStruct(e_t.shape, F32),
                   jax.ShapeDtypeStruct((nb, _SUBLANES, tm), I32)),
        compiler_params=_cparams(("parallel",)),
        name="peer_partition",
    )(e_t, gate_t)


def _pack_half_table(u_half, v_half):
    e = u_half.shape[0]
    pu = _pack_table(u_half).reshape(e, SLAB_ROWS, LANES)
    pv = _pack_table(v_half).reshape(e, SLAB_ROWS, LANES)
    return jnp.concatenate([pu, pv], axis=1).reshape(e * PAIR_SLAB, LANES)


def _gather_pairs(idx_smem, tab_ref, tile_ref, t, lo, hi):
    for p in range(lo, hi):
        row = pl.multiple_of(idx_smem[t, p], PAIR_SLAB)
        tile_ref[pl.ds(p, PAIR_SLAB, stride=TILE_STRIDE), :] = tab_ref[pl.ds(row, PAIR_SLAB), :]


def _token_out(tile_ref, x_t, gate_rows, lo, hi):
    rows = lambda j: tile_ref[j * TILE_STRIDE + lo:j * TILE_STRIDE + hi, :]
    acc = jnp.zeros((hi - lo, LANES), F32)
    for j in range(SLAB_ROWS):
        ulo, uhi = _unpack_words(rows(j))
        acc = acc + ulo * x_t[j:j + 1, :] + uhi * x_t[SLAB_ROWS + j:SLAB_ROWS + j + 1, :]
    w = gate_rows * _gelu_tanh(jnp.sum(acc, axis=1, keepdims=True))
    los, his = [], []
    for j in range(SLAB_ROWS):
        vlo, vhi = _unpack_words(rows(SLAB_ROWS + j))
        los.append(jnp.sum(vlo * w, axis=0, keepdims=True))
        his.append(jnp.sum(vhi * w, axis=0, keepdims=True))
    return jnp.concatenate(los + his, axis=0)


def _pow2_pieces(lo, hi):
    pieces = []
    while lo < hi:
        size = 1 << ((hi - lo).bit_length() - 1)
        pieces.append((lo, lo + size))
        lo += size
    return pieces


def _peer_half_kernel(finalize, half, e_hbm, n_hbm, tab_ref, h2_ref, gate_ref, part_ref, g_ref, b_ref, y_ref,
                      idx_smem, n_smem, sem, out_sc, *tiles):
    i = pl.program_id(0)
    copies = [pltpu.make_async_copy(e_hbm.at[i], idx_smem, sem.at[0]),
              pltpu.make_async_copy(n_hbm.at[i], n_smem, sem.at[1])]
    for cp in copies:
        cp.start()
    for cp in copies:
        cp.wait()
    tm = h2_ref.shape[0]
    nt = len(tiles)

    def token_out(tile_ref, t, lo, hi):
        x_t = h2_ref[t]
        out = None
        for a, b in _pow2_pieces(lo, hi):
            lane = lax.broadcasted_iota(I32, (b - a, tm), 1)
            gate = jnp.sum(jnp.where(lane == t, gate_ref[0, a:b, :], 0.0), axis=1, keepdims=True)
            piece = _token_out(tile_ref, x_t, gate, a, b)
            out = piece if out is None else out + piece
        return out

    def tok_group(g, carry):
        most = jnp.int32(0)
        for u, tile_ref in enumerate(tiles):
            t = g * nt + u
            _gather_pairs(idx_smem, tab_ref, tile_ref, t, 0, FAST_PAIRS)
            out_sc[t] = token_out(tile_ref, t, 0, FAST_PAIRS)
            most = jnp.maximum(most, n_smem[half, t])

        @pl.when(most > FAST_PAIRS)
        def _():
            for u, tile_ref in enumerate(tiles):
                t = g * nt + u
                _gather_pairs(idx_smem, tab_ref, tile_ref, t, FAST_PAIRS, PEER_PAIRS)
                out_sc[t] = out_sc[t] + token_out(tile_ref, t, FAST_PAIRS, PEER_PAIRS)
        return carry

    lax.fori_loop(0, tm // nt, tok_group, 0)
    total = part_ref[...] + out_sc[...]
    if finalize:
        x = ALPHA * h2_ref[...] + total
        mean3 = lambda a: jnp.sum(jnp.sum(a, axis=2, keepdims=True), axis=1, keepdims=True) * (1.0 / D_MODEL)
        xc = x - mean3(x)
        y_ref[...] = xc * lax.rsqrt(mean3(xc * xc) + LN_EPS) * g_ref[...] + b_ref[...]
    else:
        y_ref[...] = total


def _peer_half(finalize, half, e_rows, counts, tab, h2, gate_t, part, g, b):
    nb, tm, _ = e_rows.shape
    pairs = pl.BlockSpec((1, PEER_PAIRS, tm), lambda i: (i, 0, 0))
    row = pl.BlockSpec((tm, FEAT_CHUNKS, LANES), lambda i: (i, 0, 0))
    vec = pl.BlockSpec((FEAT_CHUNKS, LANES), lambda i: (0, 0))
    tiles = [pltpu.VMEM((PAIR_SLAB * TILE_STRIDE, LANES), I32) for _ in range(GATHER_TOKENS)]
    return pl.pallas_call(
        functools.partial(_peer_half_kernel, finalize, half),
        grid=(nb,),
        in_specs=[pl.BlockSpec(memory_space=pl.ANY), pl.BlockSpec(memory_space=pl.ANY),
                  pl.BlockSpec(memory_space=pltpu.VMEM), row, pairs, row, vec, vec],
        out_specs=row,
        out_shape=jax.ShapeDtypeStruct((nb * tm, FEAT_CHUNKS, LANES), F32),
        scratch_shapes=[pltpu.SMEM((tm, PEER_PAIRS), I32), pltpu.SMEM((_SUBLANES, tm), I32),
                        pltpu.SemaphoreType.DMA((2,)), pltpu.VMEM((tm, FEAT_CHUNKS, LANES), F32)] + tiles,
        compiler_params=_cparams(("arbitrary",)),
        name="peer_experts_half%d" % half,
    )(e_rows, counts, tab, h2, gate_t, part, g, b)


def _lane_row(vals, offset):
    return jnp.zeros((1, LANES), F32).at[0, offset:offset + vals.shape[0]].set(vals)


def _finish(x2d, ssd2d, att2d, mem_attend_fn, p):
    h1, qm = _out_proj(x2d, ssd2d, att2d, p["w_out"], p["ln1_g"], p["ln1_b"], p["w_cq"])
    o = mem_attend_fn(qm)
    h2, e_t, gate_t = _peer_route(h1, o, p["w_co"], p["ln2_g"], p["ln2_b"], p["peer_wq"], p["subkeys"])
    h2_chunks = h2.reshape(-1, FEAT_CHUNKS, LANES)
    ea, ga, eb, gb, counts = _peer_partition(e_t, gate_t)
    tok_major = lambda e: jnp.swapaxes(e, 1, 2)
    ln = (p["ln3_g"], p["ln3_b"])
    part = _peer_half(False, 0, tok_major(ea), counts, p["tab_a"], h2_chunks, ga, jnp.zeros_like(h2_chunks), *ln)
    return _peer_half(True, 1, tok_major(eb), counts, p["tab_b"], h2_chunks, gb, part, *ln).reshape(-1, D_MODEL)


def kernel(x_prompt, mem_prompt, x_sample, cache_k, cache_v, cache_kidx, cache_mem_k, cache_mem_v, state_ssm, state_conv, page_table, w_in, conv_w, conv_b, dt_bias, a_log, d_skip, ssd_norm_w, rel_bias, w_out, ln1_g, ln1_b, w_cq, w_ck, w_cv, w_co, ln2_g, ln2_b, peer_wq, peer_subkeys, peer_u, peer_v, ln3_g, ln3_b):
    bp, seq, _ = x_prompt.shape
    bs, t_s, _ = x_sample.shape
    m_tok = mem_prompt.shape[1]

    c0 = SSD_INNER + CONV_DIM
    wa = w_in[:, :c0].astype(MXU_DTYPE)
    c1 = c0 + SSD_HEADS
    wb = w_in[:, c1:c1 + 4 * ATT_INNER].astype(MXU_DTYPE)
    c2 = c1 + 4 * ATT_INNER
    w_kidx = w_in[:, c2:c2 + IDX_DIM]
    wc = jnp.concatenate([w_kidx, w_kidx], axis=1).astype(MXU_DTYPE)
    wd = jnp.concatenate([w_kidx, w_in[:, c0:c1], w_in[:, c2 + IDX_DIM:],
                          jnp.zeros((D_MODEL, LANES - IDX_DIM - SSD_HEADS - IDX_HEADS), F32)], axis=1).astype(MXU_DTYPE)
    row = lambda v: v.reshape(1, -1)
    p = dict(
        w_out=w_out.astype(MXU_DTYPE), ln1_g=row(ln1_g), ln1_b=row(ln1_b), w_cq=w_cq.astype(MXU_DTYPE),
        w_co=w_co.astype(MXU_DTYPE), ln2_g=row(ln2_g), ln2_b=row(ln2_b),
        peer_wq=peer_wq.astype(MXU_DTYPE).reshape(D_MODEL, PEER_HEADS * 2, PEER_HALF).swapaxes(0, 1),
        subkeys=peer_subkeys.reshape(PEER_HEADS * 2, PEER_KEYS, PEER_HALF).astype(MXU_DTYPE),
        tab_a=_pack_half_table(peer_u[:HALF_EXPERTS], peer_v[:HALF_EXPERTS]),
        tab_b=_pack_half_table(peer_u[HALF_EXPERTS:], peer_v[HALF_EXPERTS:]),
        ln3_g=ln3_g.reshape(FEAT_CHUNKS, LANES), ln3_b=ln3_b.reshape(FEAT_CHUNKS, LANES))
    ssd_params = (conv_w, row(conv_b), _lane_row(a_log, DT_LANE), _lane_row(dt_bias, DT_LANE),
                  row(jnp.repeat(d_skip, SSD_HEAD_DIM)), row(ssd_norm_w))
    state_t = lambda s: jnp.swapaxes(s, -1, -2)

    n_p = bp * seq
    z, xbc, q_b, k_p, v_p, k_b, v_b, qi_b, kk_b, small = _project_in(x_prompt.reshape(n_p, D_MODEL), wa, wb, wc, wd)
    r3 = lambda a, b_: a.reshape(b_, -1, a.shape[-1])
    ssd_out, ssm_t, conv_p = _ssd_mixer(
        r3(z, bp), r3(xbc, bp), r3(small, bp), jnp.zeros((bp, SSD_CONV - 1, CONV_DIM), F32),
        jnp.zeros((bp, SSD_HEADS, SSD_STATE, SSD_HEAD_DIM), F32), SSD_CHUNK, *ssd_params)
    att_out = _dsa_prompt(r3(q_b, bp), r3(qi_b, bp), r3(small, bp), r3(kk_b, bp), r3(k_b, bp), r3(v_b, bp), rel_bias)
    kidx_p = r3(small, bp)[:, :, :IDX_DIM]
    mem_kv = _matmul(mem_prompt.reshape(bp * m_tok, D_MODEL),
                     jnp.concatenate([w_ck, w_cv], axis=1).astype(MXU_DTYPE))
    memk_p = mem_kv[:, :MEM_INNER].reshape(bp, m_tok, MEM_INNER)
    memv_p = mem_kv[:, MEM_INNER:].reshape(bp, m_tok, MEM_INNER)
    y_prompt = _finish(x_prompt.reshape(n_p, D_MODEL), ssd_out.reshape(n_p, SSD_INNER),
                       att_out.reshape(n_p, ATT_INNER),
                       lambda qm: _mem_attend(qm.reshape(bp, seq, MEM_INNER), memk_p, memv_p).reshape(n_p, MEM_INNER),
                       p)

    n_s = bs * t_s
    n_pad = -(-n_s // TOK_BLOCK) * TOK_BLOCK
    xs2d = jnp.pad(x_sample.reshape(n_s, D_MODEL), ((0, n_pad - n_s), (0, 0)))
    outs = _project_in(xs2d, wa, wb, wc, wd)
    z, xbc, q_b, k_s, v_s, k_b, v_b, qi_b, kk_b, small = [r3(a[:n_s], bs) for a in outs]
    pad_chunk = lambda a: jnp.pad(a, ((0, 0), (0, SSD_CHUNK - t_s), (0, 0)))
    ssd_s, ssm_s_t, conv_s = _ssd_mixer(pad_chunk(z), pad_chunk(xbc), pad_chunk(small), state_conv,
                                        state_t(state_ssm), t_s, *ssd_params)
    att_s = _dsa_sample(q_b, qi_b, small, k_b, v_b, kk_b, cache_k, cache_v, cache_kidx, page_table, rel_bias)
    pad_rows = lambda a: jnp.pad(a.reshape(n_s, -1), ((0, n_pad - n_s), (0, 0)))
    cmk = cache_mem_k.reshape(bs, m_tok, MEM_INNER)
    cmv = cache_mem_v.reshape(bs, m_tok, MEM_INNER)
    y_s = _finish(xs2d, pad_rows(ssd_s[:, :t_s]), pad_rows(att_s),
                  lambda qm: pad_rows(_mem_attend(qm[:n_s].reshape(bs, t_s, MEM_INNER), cmk, cmv)), p)
    y_sample = y_s[:n_s].reshape(bs, t_s, D_MODEL)

    heads = lambda a, b_: a.reshape(b_, -1, ATT_HEADS, ATT_HEAD_DIM)
    mem_heads = lambda a: a.reshape(bp, m_tok, MEM_HEADS, MEM_HEAD_DIM)
    return (y_prompt.reshape(bp, seq, D_MODEL), y_sample,
            heads(k_p, bp), heads(v_p, bp), kidx_p, mem_heads(memk_p), mem_heads(memv_p),
            state_t(ssm_t), conv_p,
            heads(k_s, bs), heads(v_s, bs), small[:, :, :IDX_DIM], state_t(ssm_s_t), conv_s)
```

```python
import functools
import math

import jax
import jax.numpy as jnp
import numpy as np
from jax import lax
from jax.experimental import pallas as pl
from jax.experimental.pallas import tpu as pltpu

F32 = jnp.float32
BF16 = jnp.bfloat16
I32 = jnp.int32
MXU_DTYPE = BF16

D_MODEL = 1024
DEPTH = 1
ALPHA = (2.0 * DEPTH) ** 0.25
LN_EPS = 1e-5

SSD_HEADS = 8
SSD_HEAD_DIM = 64
SSD_INNER = 512
SSD_GROUPS = 2
SSD_STATE = 128
SSD_CONV = 4
SSD_CHUNK = 128
CONV_DIM = 1024

ATT_HEADS = 8
ATT_HEAD_DIM = 64
ATT_INNER = 512
IDX_HEADS = 8
IDX_DIM = 64
TOPK_MAX = 256
PAGE_SIZE = 128
N_BUCKETS = 32
MAX_DISTANCE = 128

MEM_HEADS = 4
MEM_HEAD_DIM = 128
MEM_INNER = 512

PEER_HEADS = 8
PEER_KEYS = 128
PEER_HALF = 128
PEER_TOPK = 16
PEER_PAIRS = PEER_HEADS * PEER_TOPK

LANES = 128
CHUNK = 2 * LANES
TOK_BLOCK = 128
DT_LANE = 64
WIDX_LANE = 72
NEG_BIG = -1e30
INT_MIN = -(2 ** 31)
VMEM_LIMIT = 56 * 1024 * 1024
SLAB_ROWS = 4
FEAT_CHUNKS = D_MODEL // LANES
TILE_STRIDE = 136
GATHER_TOKENS = 8


def _cparams(sem, vmem=VMEM_LIMIT):
    return pltpu.CompilerParams(dimension_semantics=sem, vmem_limit_bytes=vmem)


def _mm(a, b):
    return jnp.dot(a.astype(MXU_DTYPE), b.astype(MXU_DTYPE), preferred_element_type=F32)


def _mm_nt(a, b):
    return lax.dot_general(a.astype(MXU_DTYPE), b.astype(MXU_DTYPE), (((1,), (1,)), ((), ())),
                           preferred_element_type=F32)


def _mm_exact(a, b):
    return jnp.dot(a, b, preferred_element_type=F32, precision=lax.Precision.HIGHEST)


def _sigmoid(x):
    return 1.0 / (1.0 + jnp.exp(-x))


def _layer_norm(x, g, b):
    mu = jnp.mean(x, axis=-1, keepdims=True)
    xc = x - mu
    var = jnp.mean(xc * xc, axis=-1, keepdims=True)
    return xc * lax.rsqrt(var + LN_EPS) * g + b


def _bucket_thresholds():
    max_exact = N_BUCKETS // 2
    out = []
    for n in range(max_exact, MAX_DISTANCE + 1):
        v = max_exact + int(math.log(n / max_exact) / math.log(MAX_DISTANCE / max_exact) * (N_BUCKETS - max_exact))
        out.append(min(v, N_BUCKETS - 1))
    thr = []
    for bkt in range(max_exact + 1, N_BUCKETS):
        thr.append(max_exact + next(i for i, v in enumerate(out) if v >= bkt))
    return tuple(thr)


_BUCKET_THR = _bucket_thresholds()


def _rel_bucket(dist):
    n = jnp.maximum(dist, 0)
    max_exact = N_BUCKETS // 2
    large = jnp.full(n.shape, max_exact, I32)
    for t in _BUCKET_THR:
        large = large + (n >= t).astype(I32)
    return jnp.where(n < max_exact, n, large)


def _monotone_key(x):
    b = lax.bitcast_convert_type(x, I32)
    return b ^ ((b >> 31) & 0x7FFFFFFF)


def _proj_in_kernel(x_ref, wa_ref, wb_ref, wc_ref, wd_ref,
                    z_ref, xbc_ref, q_ref, k_ref, v_ref, kb_ref, vb_ref, qi_ref, kk_ref, sm_ref):
    xb = x_ref[...].astype(MXU_DTYPE)
    z_ref[...] = jnp.dot(xb, wa_ref[:, :SSD_INNER], preferred_element_type=F32)
    xbc_ref[...] = jnp.dot(xb, wa_ref[:, SSD_INNER:], preferred_element_type=F32)
    part = lambda n: wb_ref[:, n * ATT_INNER:(n + 1) * ATT_INNER]
    q = jnp.dot(xb, part(0), preferred_element_type=F32)
    q_ref[...] = (q * (ATT_HEAD_DIM ** -0.5)).astype(q_ref.dtype)
    k = jnp.dot(xb, part(1), preferred_element_type=F32)
    k_ref[...] = k
    kb_ref[...] = k.astype(kb_ref.dtype)
    v = jnp.dot(xb, part(2), preferred_element_type=F32)
    v_ref[...] = v
    vb_ref[...] = v.astype(vb_ref.dtype)
    qi = jnp.dot(xb, part(3), preferred_element_type=F32)
    qi_ref[...] = (qi * (IDX_DIM ** -0.5)).astype(qi_ref.dtype)
    kk_ref[...] = jnp.dot(xb, wc_ref[...], preferred_element_type=F32).astype(kk_ref.dtype)
    sm = jnp.dot(xb, wd_ref[...], preferred_element_type=F32)
    lane = lax.broadcasted_iota(I32, sm.shape, 1)
    is_w = (lane >= WIDX_LANE) & (lane < WIDX_LANE + IDX_HEADS)
    sm_ref[...] = jnp.where(is_w, sm * (IDX_HEADS ** -0.5), sm)


def _project_in(x2d, wa, wb, wc, wd):
    n = x2d.shape[0]
    tm = min(256, n)
    row = lambda w: pl.BlockSpec((tm, w), lambda i: (i, 0))
    full = lambda a: pl.BlockSpec(a.shape, lambda i: (0, 0))
    out_shapes = (
        jax.ShapeDtypeStruct((n, SSD_INNER), F32),
        jax.ShapeDtypeStruct((n, CONV_DIM), F32),
        jax.ShapeDtypeStruct((n, ATT_INNER), MXU_DTYPE),
        jax.ShapeDtypeStruct((n, ATT_INNER), F32),
        jax.ShapeDtypeStruct((n, ATT_INNER), F32),
        jax.ShapeDtypeStruct((n, ATT_INNER), MXU_DTYPE),
        jax.ShapeDtypeStruct((n, ATT_INNER), MXU_DTYPE),
        jax.ShapeDtypeStruct((n, ATT_INNER), MXU_DTYPE),
        jax.ShapeDtypeStruct((n, LANES), MXU_DTYPE),
        jax.ShapeDtypeStruct((n, LANES), F32),
    )
    return pl.pallas_call(
        _proj_in_kernel,
        grid=(n // tm,),
        in_specs=[row(D_MODEL), full(wa), full(wb), full(wc), full(wd)],
        out_specs=tuple(row(s.shape[1]) for s in out_shapes),
        out_shape=out_shapes,
        compiler_params=_cparams(("parallel",)),
        name="project_in",
    )(x2d, wa, wb, wc, wd)


def _ssd_kernel(n_valid, z_ref, xbc_ref, sm_ref, conv0_ref, h0_ref, convw_ref, convb_ref, alog_ref, dtb_ref,
                dskip_ref, normw_ref, y_ref, hout_ref, tail_ref, cbuf, hst, ysc):
    q_len = z_ref.shape[1]
    c = pl.program_id(1)
    pad = 8

    @pl.when(c == 0)
    def _():
        cbuf[pad - 3:pad, :] = conv0_ref[0]
        hst[...] = h0_ref[0]

    xbc = xbc_ref[0]
    cbuf[pad:pad + q_len, :] = xbc
    w = convw_ref[...]
    conv = cbuf[pad - 3:pad - 3 + q_len, :] * w[0:1]
    conv = conv + cbuf[pad - 2:pad - 2 + q_len, :] * w[1:2]
    conv = conv + cbuf[pad - 1:pad - 1 + q_len, :] * w[2:3]
    conv = conv + xbc * w[3:4]
    conv = conv + convb_ref[...]
    xc = conv * _sigmoid(conv)
    tail = cbuf[pad + n_valid - 3:pad + n_valid, :]
    cbuf[pad - 3:pad, :] = tail

    row = lax.broadcasted_iota(I32, (q_len, LANES), 0)
    lane = lax.broadcasted_iota(I32, (q_len, LANES), 1)
    is_dt = (lane >= DT_LANE) & (lane < DT_LANE + SSD_HEADS) & (row < n_valid)
    raw = sm_ref[0] + dtb_ref[...]
    dt_full = jnp.maximum(raw, 0.0) + jnp.log1p(jnp.exp(-jnp.abs(raw)))
    dtm = jnp.where(is_dt, dt_full, 0.0)
    a_full = dtm * (-jnp.exp(alog_ref[...]))
    ri = lax.broadcasted_iota(I32, (q_len, q_len), 0)
    ci = lax.broadcasted_iota(I32, (q_len, q_len), 1)
    causal = ci <= ri
    acum = _mm_exact(causal.astype(F32), a_full)
    acum_t = acum.T

    xs = xc[:, :SSD_INNER]
    for g in range(SSD_GROUPS):
        bm = xc[:, SSD_INNER + g * SSD_STATE:SSD_INNER + (g + 1) * SSD_STATE]
        cm = xc[:, SSD_INNER + (SSD_GROUPS + g) * SSD_STATE:SSD_INNER + (SSD_GROUPS + g + 1) * SSD_STATE]
        scores = _mm_nt(cm, bm)
        bm_t = bm.T
        for hh in range(SSD_HEADS // SSD_GROUPS):
            h = g * (SSD_HEADS // SSD_GROUPS) + hh
            col = acum[:, DT_LANE + h:DT_LANE + h + 1]
            rowv = acum_t[DT_LANE + h:DT_LANE + h + 1, :]
            a_last = acum[q_len - 1:q_len, DT_LANE + h:DT_LANE + h + 1]
            decay = jnp.exp(jnp.where(causal, col - rowv, -jnp.inf))
            x_h = xs[:, h * SSD_HEAD_DIM:(h + 1) * SSD_HEAD_DIM]
            xd = x_h * dtm[:, DT_LANE + h:DT_LANE + h + 1]
            h_t = hst[h]
            y = _mm(scores * decay, xd) + _mm(cm, h_t) * jnp.exp(col)
            y = y + x_h * dskip_ref[:, h * SSD_HEAD_DIM:(h + 1) * SSD_HEAD_DIM]
            ysc[:, h * SSD_HEAD_DIM:(h + 1) * SSD_HEAD_DIM] = y
            hst[h] = h_t * jnp.exp(a_last) + _mm(bm_t, xd * jnp.exp(a_last - col))

    zz = z_ref[0]
    hg = ysc[...] * (zz * _sigmoid(zz))
    hg = hg * lax.rsqrt(jnp.mean(hg * hg, axis=-1, keepdims=True) + LN_EPS)
    y_ref[0] = hg * normw_ref[...]

    @pl.when(c == pl.num_programs(1) - 1)
    def _():
        hout_ref[0] = hst[...]
        tail_ref[0] = cbuf[pad - 3:pad, :]


def _ssd_mixer(z, xbc, small, conv0, h0_t, n_valid, conv_w, conv_b, alog_lane, dtb_lane, dskip_row, norm_w):
    b, length, _ = z.shape
    q_len = SSD_CHUNK
    nc = length // q_len
    blk = lambda w: pl.BlockSpec((1, q_len, w), lambda i, c: (i, c, 0))
    per_b3 = lambda s: pl.BlockSpec((1,) + s, lambda i, c: (i,) + (0,) * len(s))
    const2 = lambda a: pl.BlockSpec(a.shape, lambda i, c: (0, 0))
    st_shape = (SSD_HEADS, SSD_STATE, SSD_HEAD_DIM)
    return pl.pallas_call(
        functools.partial(_ssd_kernel, n_valid),
        grid=(b, nc),
        in_specs=[blk(SSD_INNER), blk(CONV_DIM), blk(LANES), per_b3((SSD_CONV - 1, CONV_DIM)), per_b3(st_shape),
                  const2(conv_w), const2(conv_b), const2(alog_lane), const2(dtb_lane), const2(dskip_row),
                  const2(norm_w)],
        out_specs=(blk(SSD_INNER), per_b3(st_shape), per_b3((SSD_CONV - 1, CONV_DIM))),
        out_shape=(jax.ShapeDtypeStruct((b, length, SSD_INNER), F32),
                   jax.ShapeDtypeStruct((b,) + st_shape, F32),
                   jax.ShapeDtypeStruct((b, SSD_CONV - 1, CONV_DIM), F32)),
        scratch_shapes=[pltpu.VMEM((q_len + 8, CONV_DIM), F32), pltpu.VMEM(st_shape, F32),
                        pltpu.VMEM((q_len, SSD_INNER), F32)],
        compiler_params=_cparams(("parallel", "arbitrary")),
        name="ssd_mixer",
    )(z, xbc, small, conv0, h0_t, conv_w, conv_b, alog_lane, dtb_lane, dskip_row, norm_w)


def _kth_largest_key(count_ge, k, rows):
    def body(i, tau):
        cand = tau + jnp.left_shift(jnp.int32(1), 31 - i)
        return jnp.where(count_ge(cand) >= k, cand, tau)
    return lax.fori_loop(0, 32, body, jnp.full((rows, 1), INT_MIN, I32))


def _bias_tiles(rb_ref, bias_sc, n_tiles):
    a = lax.broadcasted_iota(I32, (LANES, LANES), 0)
    bcol = lax.broadcasted_iota(I32, (LANES, LANES), 1)
    for d in range(n_tiles):
        bucket = _rel_bucket(d * LANES + a - bcol)
        for h in range(ATT_HEADS):
            far = rb_ref[N_BUCKETS - 1, h]
            tile = jnp.zeros((LANES, LANES), F32)
            for kb in range(N_BUCKETS):
                tile = jnp.where(bucket == kb, rb_ref[kb, h] - far, tile)
            bias_sc[d, h] = tile


def _dsa_prompt_kernel(topk, q_ref, qi_ref, sm_ref, kk_ref, kb_ref, vb_ref, rb_ref, o_ref,
                       keys_sc, bias_sc, qm_sc, qim_sc, wb_sc, *head_sc):
    nh = ATT_HEADS
    cache, m_l, l_l, acc = head_sc[:nh], head_sc[nh:2 * nh], head_sc[2 * nh:3 * nh], head_sc[3 * nh:]
    i = pl.program_id(1)
    tq = q_ref.shape[1]
    n_far = bias_sc.shape[0] - 1
    n_chunks = (i + 2) // 2

    @pl.when(i == 0)
    def _():
        _bias_tiles(rb_ref, bias_sc, n_far + 1)

    lane = lax.broadcasted_iota(I32, (tq, LANES), 1)
    low_half = lane < ATT_HEAD_DIM
    lane2 = lax.broadcasted_iota(I32, (tq, CHUNK), 1)
    row2 = lax.broadcasted_iota(I32, (tq, CHUNK), 0)
    sm = sm_ref[0]

    for h in range(nh):
        keep = low_half if h % 2 == 0 else jnp.logical_not(low_half)
        pair = slice((h // 2) * LANES, (h // 2 + 1) * LANES)
        qp, qip = q_ref[0, :, pair], qi_ref[0, :, pair]
        qm_sc[h] = jnp.where(keep, qp, jnp.zeros_like(qp))
        qim_sc[h] = jnp.where(keep, qip, jnp.zeros_like(qip))
        wb_sc[h] = jnp.broadcast_to(sm[:, WIDX_LANE + h:WIDX_LANE + h + 1], (tq, LANES))
        m_l[h][...] = jnp.full((tq, LANES), NEG_BIG, F32)
        l_l[h][...] = jnp.zeros((tq, LANES), F32)
        acc[h][...] = jnp.zeros((tq, LANES), F32)

    def causal(c):
        return (c * CHUNK + lane2) <= (i * tq + row2)

    def chunk_logits(c, carry):
        start = pl.multiple_of(c * CHUNK, CHUNK)
        kk = kk_ref[0, pl.ds(start, CHUNK), :]
        sc = jnp.zeros((tq, CHUNK), F32)
        for h in range(IDX_HEADS):
            wb = wb_sc[h]
            sc = sc + jnp.maximum(_mm_nt(qim_sc[h], kk), 0.0) * jnp.concatenate([wb, wb], axis=1)
        keys_sc[c] = _monotone_key(jnp.where(causal(c), sc, -jnp.inf))
        d0 = jnp.clip(i - 2 * c, 0, n_far)
        d1 = jnp.clip(i - 2 * c - 1, 0, n_far)
        for h in range(nh):
            kblk = kb_ref[0, pl.ds(start, CHUNK), (h // 2) * LANES:(h // 2 + 1) * LANES]
            bias = jnp.concatenate([bias_sc[d0, h], bias_sc[d1, h]], axis=1)
            cache[h][c] = _mm_nt(qm_sc[h], kblk) + bias
        return carry

    lax.fori_loop(0, n_chunks, chunk_logits, 0)

    def count_ge(cand):
        def body(c, cnt):
            k = keys_sc[c]
            return cnt + jnp.where(k[:, :LANES] >= cand, 1.0, 0.0) + jnp.where(k[:, LANES:] >= cand, 1.0, 0.0)
        cnt = lax.fori_loop(0, n_chunks, body, jnp.zeros((tq, LANES), F32))
        return jnp.sum(cnt, axis=1, keepdims=True)

    tau = _kth_largest_key(count_ge, float(topk), tq)

    def chunk_max(c, carry):
        sel = (keys_sc[c] >= tau) & causal(c)
        for h in range(nh):
            s = jnp.where(sel, cache[h][c], NEG_BIG)
            cache[h][c] = s
            m_l[h][...] = jnp.maximum(m_l[h][...], jnp.maximum(s[:, :LANES], s[:, LANES:]))
        return carry

    lax.fori_loop(0, n_chunks, chunk_max, 0)
    for h in range(nh):
        m_l[h][...] = jnp.broadcast_to(jnp.max(m_l[h][...], axis=1, keepdims=True), (tq, LANES))

    def chunk_pv(c, carry):
        start = pl.multiple_of(c * CHUNK, CHUNK)
        for h in range(nh):
            mb = m_l[h][...]
            s = cache[h][c]
            p0 = jnp.exp(s[:, :LANES] - mb)
            p1 = jnp.exp(s[:, LANES:] - mb)
            l_l[h][...] = l_l[h][...] + (p0 + p1)
            vblk = vb_ref[0, pl.ds(start, CHUNK), (h // 2) * LANES:(h // 2 + 1) * LANES]
            acc[h][...] = acc[h][...] + _mm(jnp.concatenate([p0, p1], axis=1), vblk)
        return carry

    lax.fori_loop(0, n_chunks, chunk_pv, 0)

    for g in range(nh // 2):
        lo = acc[2 * g][...] / jnp.sum(l_l[2 * g][...], axis=1, keepdims=True)
        hi = acc[2 * g + 1][...] / jnp.sum(l_l[2 * g + 1][...], axis=1, keepdims=True)
        o_ref[0, :, g * LANES:(g + 1) * LANES] = jnp.where(low_half, lo, hi)


def _dsa_prompt(q_b, qi_b, small, kk_b, k_b, v_b, rel_bias):
    b, s, _ = q_b.shape
    tq = LANES
    nq = s // tq
    topk = min(TOPK_MAX, s // 4)
    n_bias_tiles = 3
    assert MAX_DISTANCE <= LANES + 1 and s % CHUNK == 0
    n_chunks = s // CHUNK
    qblk = lambda w: pl.BlockSpec((1, tq, w), lambda i, j: (i, j, 0))
    per_b = lambda w: pl.BlockSpec((1, s, w), lambda i, j: (i, 0, 0))
    per_head = lambda shape, dt: [pltpu.VMEM(shape, dt) for _ in range(ATT_HEADS)]
    return pl.pallas_call(
        functools.partial(_dsa_prompt_kernel, topk),
        grid=(b, nq),
        in_specs=[qblk(ATT_INNER), qblk(ATT_INNER), qblk(LANES), per_b(LANES), per_b(ATT_INNER), per_b(ATT_INNER),
                  pl.BlockSpec(memory_space=pltpu.SMEM)],
        out_specs=qblk(ATT_INNER),
        out_shape=jax.ShapeDtypeStruct((b, s, ATT_INNER), F32),
        scratch_shapes=[pltpu.VMEM((n_chunks, tq, CHUNK), I32),
                        pltpu.VMEM((n_bias_tiles, ATT_HEADS, LANES, LANES), F32),
                        pltpu.VMEM((ATT_HEADS, tq, LANES), q_b.dtype), pltpu.VMEM((ATT_HEADS, tq, LANES), qi_b.dtype),
                        pltpu.VMEM((ATT_HEADS, tq, LANES), F32)]
                       + per_head((n_chunks, tq, CHUNK), F32)
                       + per_head((tq, LANES), F32)
                       + per_head((tq, LANES), F32)
                       + per_head((tq, LANES), F32),
        compiler_params=_cparams(("parallel", "arbitrary")),
        name="dsa_prompt",
    )(q_b, qi_b, small, kk_b, k_b, v_b, rel_bias)


def _dsa_sample_score_kernel(topk, t_new, pt_ref, qi_ref, w_ref, *refs):
    kidx_refs, (kknew_ref, keys_ref, keysnew_ref, tau_ref, keys_sc) = refs[:-5], refs[-5:]
    pg = len(kidx_refs)
    j = pl.program_id(1)
    n_pages = pl.num_programs(1) * pg
    rows = qi_ref.shape[1]
    lane = lax.broadcasted_iota(I32, (t_new, LANES), 1)
    rowi = lax.broadcasted_iota(I32, (t_new, LANES), 0)

    def head_sum(s):
        s = jnp.maximum(s, 0.0) * w_ref[0]
        return jnp.concatenate(
            [jnp.sum(s[t * IDX_HEADS:(t + 1) * IDX_HEADS], axis=0, keepdims=True) for t in range(t_new)], axis=0)

    for r, kidx_ref in enumerate(kidx_refs):
        keys_sc[j * pg + r] = _monotone_key(head_sum(_mm(qi_ref[0], kidx_ref[0])))

    @pl.when(j == pl.num_programs(1) - 1)
    def _():
        sc = head_sum(_mm(qi_ref[0], kknew_ref[0]))
        keys_new = _monotone_key(jnp.where(lane <= rowi, sc, -jnp.inf))
        keys_sc[n_pages] = keys_new

        def count_ge(cand):
            def body(jj, cnt):
                return cnt + jnp.where(keys_sc[jj] >= cand, 1.0, 0.0)
            cnt = lax.fori_loop(0, n_pages + 1, body, jnp.zeros((t_new, LANES), F32))
            return jnp.sum(cnt, axis=1, keepdims=True)

        tau = _kth_largest_key(count_ge, float(topk), t_new)
        tau_ref[0] = jnp.broadcast_to(tau, (t_new, LANES))
        keys_ref[0] = keys_sc[0:n_pages]
        keysnew_ref[0] = keys_new


def _dsa_sample_attn_kernel(t_new, past, pg, pt_ref, q_ref, keys_ref, keysnew_ref, tau_ref, knew_ref, vnew_ref,
                            rbrow_ref, *refs):
    k_refs, v_refs = refs[:pg], refs[pg:2 * pg]
    o_ref, cache_sc, m_sc, l_sc, acc_sc = refs[2 * pg:]
    phase = pl.program_id(1)
    j = pl.program_id(2)
    n_steps = pl.num_programs(2)
    n_pages = n_steps * pg
    rows = q_ref.shape[1]
    lane = lax.broadcasted_iota(I32, (rows, LANES), 1)
    rowi = lax.broadcasted_iota(I32, (rows, LANES), 0)
    tok = lax.rem(rowi, t_new)
    q = q_ref[0]

    def rows_of(x_t):
        return jnp.concatenate([x_t] * ATT_HEADS, axis=0)

    tau = rows_of(tau_ref[0])

    def page_rows(ref):
        return ref[0].reshape(ATT_INNER, PAGE_SIZE)

    def near_bias(key_pos):
        bucket = _rel_bucket(past + tok - key_pos)
        far = rbrow_ref[:, N_BUCKETS - 1:N_BUCKETS]
        bias = jnp.zeros((rows, LANES), F32)
        for kb in range(N_BUCKETS - 1):
            bias = jnp.where(bucket == kb, rbrow_ref[:, kb:kb + 1] - far, bias)
        return bias

    @pl.when(phase == 0)
    def _():
        for r in range(pg):
            s = _mm(q, page_rows(k_refs[r]))
            cache_sc[j * pg + r] = jnp.where(rows_of(keys_ref[0, r]) >= tau, s, NEG_BIG)

        @pl.when(j == n_steps - 1)
        def _():
            last = n_pages - 1
            cache_sc[last] = cache_sc[last] + near_bias(last * PAGE_SIZE + lane)
            s = _mm(q, knew_ref[0]) + near_bias(past + lane)
            ok = (rows_of(keysnew_ref[0]) >= tau) & (lane <= tok)
            cache_sc[n_pages] = jnp.where(ok, s, NEG_BIG)
            m = lax.fori_loop(0, n_pages + 1, lambda jj, m: jnp.maximum(m, cache_sc[jj]),
                              jnp.full((rows, LANES), NEG_BIG, F32))
            m_sc[...] = jnp.broadcast_to(jnp.max(m, axis=1, keepdims=True), (rows, LANES))
            l_sc[...] = jnp.zeros(l_sc.shape, F32)
            acc_sc[...] = jnp.zeros(acc_sc.shape, F32)

    @pl.when(phase == 1)
    def _():
        mb = m_sc[...]
        acc = acc_sc[...]
        lsum = l_sc[...]
        for r in range(pg):
            p = jnp.exp(cache_sc[j * pg + r] - mb)
            lsum = lsum + p
            acc = acc + _mm_nt(p, page_rows(v_refs[r]))
        l_sc[...] = lsum
        acc_sc[...] = acc

        @pl.when(j == n_steps - 1)
        def _():
            p_new = jnp.exp(cache_sc[n_pages] - mb)
            l = jnp.sum(lsum + p_new, axis=1, keepdims=True)
            a = (acc + _mm_nt(p_new, vnew_ref[0])) / l
            col_head = lax.broadcasted_iota(I32, (t_new, ATT_INNER), 1) // ATT_HEAD_DIM
            out = jnp.zeros((t_new, ATT_INNER), F32)
            for h in range(ATT_HEADS):
                out = jnp.where(col_head == h, a[h * t_new:(h + 1) * t_new], out)
            o_ref[0] = out


def _dsa_sample(q_b, qi_b, small, k_b, v_b, kk_b, cache_k, cache_v, cache_kidx, page_table, rel_bias):
    b, t_new, _ = q_b.shape
    n_pages = page_table.shape[1]
    past = n_pages * PAGE_SIZE
    topk = min(TOPK_MAX, (past + t_new) // 4)
    rows = t_new * ATT_HEADS
    assert PAGE_SIZE >= MAX_DISTANCE

    def pages_per_step(limit):
        return max(g for g in (1, 2, 4, 8, 16, 32) if g <= limit and n_pages % g == 0)

    kidx_t = jnp.transpose(cache_kidx, (0, 2, 1))
    k_t = jnp.transpose(cache_k, (0, 2, 3, 1))
    v_t = jnp.transpose(cache_v, (0, 2, 3, 1))
    new_cols = lambda x: jnp.pad(jnp.swapaxes(x, 1, 2), ((0, 0), (0, 0), (0, PAGE_SIZE - t_new)))

    pgs = pages_per_step(32)
    qi_rows = qi_b.reshape(b, rows, IDX_DIM)
    w_rows = small[:, :, WIDX_LANE:WIDX_LANE + IDX_HEADS].reshape(b, rows, 1)
    kk_new = new_cols(kk_b[:, :, :IDX_DIM])
    per_b = lambda s: pl.BlockSpec((1,) + s, lambda i, j, pt: (i,) + (0,) * len(s))
    kidx_page = lambda r: pl.BlockSpec((1, IDX_DIM, PAGE_SIZE), lambda i, j, pt: (pt[i, j * pgs + r], 0, 0))
    keys, keys_new, tau = pl.pallas_call(
        functools.partial(_dsa_sample_score_kernel, topk, t_new),
        grid_spec=pltpu.PrefetchScalarGridSpec(
            num_scalar_prefetch=1, grid=(b, n_pages // pgs),
            in_specs=[per_b((rows, IDX_DIM)), per_b((rows, 1))] + [kidx_page(r) for r in range(pgs)]
                     + [per_b((IDX_DIM, LANES))],
            out_specs=(per_b((n_pages, t_new, LANES)), per_b((t_new, LANES)), per_b((t_new, LANES))),
            scratch_shapes=[pltpu.VMEM((n_pages + 1, t_new, LANES), I32)]),
        out_shape=(jax.ShapeDtypeStruct((b, n_pages, t_new, LANES), I32),
                   jax.ShapeDtypeStruct((b, t_new, LANES), I32),
                   jax.ShapeDtypeStruct((b, t_new, LANES), I32)),
        compiler_params=_cparams(("parallel", "arbitrary")),
        name="dsa_sample_scores",
    )(page_table, qi_rows, w_rows, *([kidx_t] * pgs), kk_new)

    pg = pages_per_step(8)
    n_steps = n_pages // pg
    head_of_col = jnp.arange(ATT_INNER) // ATT_HEAD_DIM
    q_heads = jnp.where(head_of_col[None, None, None, :] == jnp.arange(ATT_HEADS)[None, :, None, None],
                        q_b[:, None, :, :], jnp.zeros((), q_b.dtype))
    q_rows = q_heads.reshape(b, rows, ATT_INNER)
    rb_rows = jnp.repeat(rel_bias.T, t_new, axis=0)
    k_new, v_new = new_cols(k_b), new_cols(v_b)
    per_b = lambda s: pl.BlockSpec((1,) + s, lambda i, ph, j, pt: (i,) + (0,) * len(s))
    page_blk = (1, ATT_HEADS, ATT_HEAD_DIM, PAGE_SIZE)
    k_page = lambda r: pl.BlockSpec(
        page_blk, lambda i, ph, j, pt: (pt[i, jnp.where(ph == 0, j, n_steps - 1) * pg + r], 0, 0, 0))
    v_page = lambda r: pl.BlockSpec(
        page_blk, lambda i, ph, j, pt: (pt[i, jnp.where(ph == 1, j, 0) * pg + r], 0, 0, 0))
    return pl.pallas_call(
        functools.partial(_dsa_sample_attn_kernel, t_new, past, pg),
        grid_spec=pltpu.PrefetchScalarGridSpec(
            num_scalar_prefetch=1, grid=(b, 2, n_steps),
            in_specs=[per_b((rows, ATT_INNER)),
                      pl.BlockSpec((1, pg, t_new, LANES),
                                   lambda i, ph, j, pt: (i, jnp.where(ph == 0, j, n_steps - 1), 0, 0)),
                      per_b((t_new, LANES)), per_b((t_new, LANES)),
                      per_b((ATT_INNER, LANES)), per_b((ATT_INNER, LANES)),
                      pl.BlockSpec((rows, N_BUCKETS), lambda i, ph, j, pt: (0, 0))]
                     + [k_page(r) for r in range(pg)] + [v_page(r) for r in range(pg)],
            out_specs=per_b((t_new, ATT_INNER)),
            scratch_shapes=[pltpu.VMEM((n_pages + 1, rows, LANES), F32), pltpu.VMEM((rows, LANES), F32),
                            pltpu.VMEM((rows, LANES), F32), pltpu.VMEM((rows, ATT_INNER), F32)]),
        out_shape=jax.ShapeDtypeStruct((b, t_new, ATT_INNER), F32),
        compiler_params=_cparams(("parallel", "arbitrary", "arbitrary")),
        name="dsa_sample_attend",
    )(page_table, q_rows, keys, keys_new, tau, k_new, v_new, rb_rows, *([k_t] * pg), *([v_t] * pg))


def _matmul_kernel(x_ref, w_ref, o_ref):
    o_ref[...] = _mm(x_ref[...], w_ref[...])


def _matmul(x2d, w):
    n, kdim = x2d.shape
    tm = min(256, n)
    return pl.pallas_call(
        _matmul_kernel,
        grid=(n // tm,),
        in_specs=[pl.BlockSpec((tm, kdim), lambda i: (i, 0)), pl.BlockSpec(w.shape, lambda i: (0, 0))],
        out_specs=pl.BlockSpec((tm, w.shape[1]), lambda i: (i, 0)),
        out_shape=jax.ShapeDtypeStruct((n, w.shape[1]), F32),
        compiler_params=_cparams(("parallel",)),
        name="mem_kv_proj",
    )(x2d, w)


def _out_proj_kernel(x_ref, ssd_ref, att_ref, wout_ref, g_ref, b_ref, wcq_ref, h1_ref, qm_ref):
    m = _mm(ssd_ref[...], wout_ref[:SSD_INNER, :]) + _mm(att_ref[...], wout_ref[SSD_INNER:, :])
    h1 = _layer_norm(ALPHA * x_ref[...] + m, g_ref[...], b_ref[...])
    h1_ref[...] = h1
    qm_ref[...] = _mm(h1, wcq_ref[...])


def _out_proj(x2d, ssd2d, att2d, w_out, g, b, w_cq):
    n = x2d.shape[0]
    tm = min(256, n)
    row = lambda w: pl.BlockSpec((tm, w), lambda i: (i, 0))
    full = lambda a: pl.BlockSpec(a.shape, lambda i: (0, 0))
    return pl.pallas_call(
        _out_proj_kernel,
        grid=(n // tm,),
        in_specs=[row(D_MODEL), row(SSD_INNER), row(ATT_INNER), full(w_out), full(g), full(b), full(w_cq)],
        out_specs=(row(D_MODEL), row(MEM_INNER)),
        out_shape=(jax.ShapeDtypeStruct((n, D_MODEL), F32), jax.ShapeDtypeStruct((n, MEM_INNER), F32)),
        compiler_params=_cparams(("parallel",)),
        name="out_proj_ln1",
    )(x2d, ssd2d, att2d, w_out, g, b, w_cq)


def _mem_attend_kernel(q_ref, mk_ref, mv_ref, o_ref):
    q = q_ref[0]
    for h in range(MEM_HEADS):
        sl = slice(h * MEM_HEAD_DIM, (h + 1) * MEM_HEAD_DIM)
        logits = _mm_nt(q[:, sl], mk_ref[0, :, sl]) * (MEM_HEAD_DIM ** -0.5)
        logits = logits - jnp.max(logits, axis=-1, keepdims=True)
        p = jnp.exp(logits)
        p = p / jnp.sum(p, axis=-1, keepdims=True)
        o_ref[0, :, sl] = _mm(p, mv_ref[0, :, sl])


def _mem_attend(qm, mk, mv):
    b, length, _ = qm.shape
    tq = min(512, length)
    m_tok = mk.shape[1]
    return pl.pallas_call(
        _mem_attend_kernel,
        grid=(b, length // tq),
        in_specs=[pl.BlockSpec((1, tq, MEM_INNER), lambda i, j: (i, j, 0)),
                  pl.BlockSpec((1, m_tok, MEM_INNER), lambda i, j: (i, 0, 0)),
                  pl.BlockSpec((1, m_tok, MEM_INNER), lambda i, j: (i, 0, 0))],
        out_specs=pl.BlockSpec((1, tq, MEM_INNER), lambda i, j: (i, j, 0)),
        out_shape=jax.ShapeDtypeStruct((b, length, MEM_INNER), F32),
        compiler_params=_cparams(("parallel", "parallel")),
        name="mem_attend",
    )(qm, mk, mv)


def _top_rows(s, n_top, with_index):
    rows = s.shape[0]
    ridx = lax.broadcasted_iota(I32, s.shape, 0).astype(F32)
    vals, idxs = [], []
    for _ in range(n_top):
        m = jnp.max(s, axis=0, keepdims=True)
        first = jnp.min(jnp.where(s == m, ridx, float(rows)), axis=0, keepdims=True)
        s = jnp.where(ridx == first, -jnp.inf, s)
        vals.append(m)
        idxs.append(first)
    return jnp.concatenate(vals, axis=0), (jnp.concatenate(idxs, axis=0) if with_index else None)


_SUBLANES = 8
ROUTE_HEADS_PER_TRIP = 8


def _staircase_blocks(n):
    by_b, single_b = [], []
    for a in range(n):
        count = n // (a + 1)
        if count > 1:
            by_b += [(a, b) for b in range(0, count, _SUBLANES)]
        else:
            single_b.append(a)
    assert len(single_b) % _SUBLANES == 0 and single_b == list(range(n - len(single_b), n))
    return tuple(by_b), tuple(single_b[::_SUBLANES])


_STAIRCASE = _staircase_blocks(PEER_TOPK)


def _peer_route_kernel(h1_ref, o_ref, wco_ref, g_ref, b_ref, wq_ref, sk_ref, h2_ref, e_ref, gate_ref, h2b_sc):
    h2 = _layer_norm(ALPHA * h1_ref[...] + _mm(o_ref[...], wco_ref[...]), g_ref[...], b_ref[...])
    h2_ref[...] = h2
    h2b_sc[...] = h2.astype(MXU_DTYPE)

    def route_head(h, carry):
        tops = []
        for p in range(2):
            c = h * 2 + p
            qs = jnp.dot(h2b_sc[...], wq_ref[c], preferred_element_type=F32)
            s_t = _mm_nt(sk_ref[c], qs)
            tops.append(_top_rows(s_t, PEER_TOPK, True))
        (v0, i0), (v1, i1) = tops
        by_b, by_a = _STAIRCASE
        sub = _SUBLANES
        cand = jnp.concatenate([v0[a:a + 1] + v1[b:b + sub] for a, b in by_b]
                               + [v0[a:a + sub] + v1[0:1] for a in by_a], axis=0)
        cidx = jnp.concatenate([i0[a:a + 1] * PEER_KEYS + i1[b:b + sub] for a, b in by_b]
                               + [i0[a:a + sub] * PEER_KEYS + i1[0:1] for a in by_a], axis=0)
        pos = lax.broadcasted_iota(I32, cand.shape, 0).astype(F32)
        gs, es = [], []
        for _ in range(PEER_TOPK):
            m = jnp.max(cand, axis=0, keepdims=True)
            first = jnp.min(jnp.where(cand == m, pos, float(cand.shape[0])), axis=0, keepdims=True)
            hit = pos == first
            es.append(jnp.max(jnp.where(hit, cidx, -1.0), axis=0, keepdims=True))
            cand = jnp.where(hit, -jnp.inf, cand)
            gs.append(m)
        g_s = jnp.concatenate(gs, axis=0)
        ex = jnp.exp(g_s - g_s[0:1])
        rows = pl.ds(pl.multiple_of(h * PEER_TOPK, PEER_TOPK), PEER_TOPK)
        gate_ref[0, rows, :] = ex / jnp.sum(ex, axis=0, keepdims=True)
        e_ref[0, rows, :] = jnp.concatenate(es, axis=0).astype(I32) * SLAB_ROWS
        return carry

    def route_group(g, carry):
        for k in range(ROUTE_HEADS_PER_TRIP):
            route_head(g * ROUTE_HEADS_PER_TRIP + k, carry)
        return carry

    lax.fori_loop(0, PEER_HEADS // ROUTE_HEADS_PER_TRIP, route_group, 0)


def _peer_route(h1, o, w_co, g, b, wq, subkeys):
    n = h1.shape[0]
    tm = TOK_BLOCK
    nb = n // tm
    row = lambda w: pl.BlockSpec((tm, w), lambda i: (i, 0))
    full = lambda a: pl.BlockSpec(a.shape, lambda i: (0,) * a.ndim)
    pairs = pl.BlockSpec((1, PEER_PAIRS, tm), lambda i: (i, 0, 0))
    return pl.pallas_call(
        _peer_route_kernel,
        grid=(nb,),
        in_specs=[row(D_MODEL), row(MEM_INNER), full(w_co), full(g), full(b), full(wq), full(subkeys)],
        out_specs=(row(D_MODEL), pairs, pairs),
        out_shape=(jax.ShapeDtypeStruct((n, D_MODEL), F32),
                   jax.ShapeDtypeStruct((nb, PEER_PAIRS, tm), I32),
                   jax.ShapeDtypeStruct((nb, PEER_PAIRS, tm), F32)),
        scratch_shapes=[pltpu.VMEM((tm, D_MODEL), MXU_DTYPE)],
        compiler_params=_cparams(("parallel",)),
        name="ln2_peer_route",
    )(h1, o, w_co, g, b, wq, subkeys)


def _pack_table(t):
    tb = lax.bitcast_convert_type(t.astype(BF16), jnp.uint16).astype(jnp.uint32)
    half = t.shape[1] // 2
    words = tb[:, :half] | (tb[:, half:] << 16)
    return lax.bitcast_convert_type(words, I32).reshape(t.shape[0] * SLAB_ROWS, LANES)


def _unpack_words(w):
    lo = lax.bitcast_convert_type(w << 16, F32)
    hi = lax.bitcast_convert_type(w & jnp.int32(-65536), F32)
    return lo, hi


def _gelu_tanh(x):
    return 0.5 * x * (1.0 + jnp.tanh(math.sqrt(2.0 / math.pi) * (x + 0.044715 * (x * x * x))))


def _gather_token_rows(idx_smem, tab_ref, tile_ref, t):
    for p in range(PEER_PAIRS):
        row = pl.multiple_of(idx_smem[t, p], SLAB_ROWS)
        tile_ref[pl.ds(p, SLAB_ROWS, stride=TILE_STRIDE), :] = tab_ref[pl.ds(row, SLAB_ROWS), :]


def _load_indices(e_hbm, idx_smem, sem):
    cp = pltpu.make_async_copy(e_hbm.at[pl.program_id(0)], idx_smem, sem)
    cp.start()
    cp.wait()


def _trip_rows(g):
    return pl.ds(pl.multiple_of(g * GATHER_TOKENS, GATHER_TOKENS), GATHER_TOKENS)


def _peer_in_kernel(e_hbm, tab_ref, h2_ref, gate_ref, w_ref, idx_smem, sem, *tiles):
    _load_indices(e_hbm, idx_smem, sem)
    tm = h2_ref.shape[0]
    lane = lax.broadcasted_iota(I32, (PEER_PAIRS, tm), 1)

    def tok_group(g, act_t):
        x_rows = h2_ref[_trip_rows(g), :]
        cols = []
        _gather_token_rows(idx_smem, tab_ref, tiles[0], g * GATHER_TOKENS)
        for u, tile_ref in enumerate(tiles):
            if u + 1 < GATHER_TOKENS:
                _gather_token_rows(idx_smem, tab_ref, tiles[u + 1], g * GATHER_TOKENS + u + 1)
            x_chunk = lambda c: x_rows[u:u + 1, c * LANES:(c + 1) * LANES]
            acc = jnp.zeros((PEER_PAIRS, LANES), F32)
            for j in range(SLAB_ROWS):
                lo, hi = _unpack_words(tile_ref[j * TILE_STRIDE:j * TILE_STRIDE + PEER_PAIRS, :])
                acc = acc + lo * x_chunk(j) + hi * x_chunk(SLAB_ROWS + j)
            cols.append(jnp.sum(acc, axis=1, keepdims=True))
        for u, col in enumerate(cols):
            act_t = jnp.where(lane == g * GATHER_TOKENS + u, col, act_t)
        return act_t

    act_t = lax.fori_loop(0, tm // GATHER_TOKENS, tok_group, jnp.zeros((PEER_PAIRS, tm), F32))
    w_ref[0] = gate_ref[0] * _gelu_tanh(act_t)


def _peer_out_kernel(e_hbm, tab_ref, h2_ref, w_ref, g_ref, b_ref, y_ref, idx_smem, sem, out_sc, *tiles):
    _load_indices(e_hbm, idx_smem, sem)
    tm = h2_ref.shape[0]
    lane = lax.broadcasted_iota(I32, (PEER_PAIRS, tm), 1)

    def tok_group(g, carry):
        rows = []
        _gather_token_rows(idx_smem, tab_ref, tiles[0], g * GATHER_TOKENS)
        for u, tile_ref in enumerate(tiles):
            t = g * GATHER_TOKENS + u
            if u + 1 < GATHER_TOKENS:
                _gather_token_rows(idx_smem, tab_ref, tiles[u + 1], t + 1)
            w_col = jnp.sum(jnp.where(lane == t, w_ref[0], 0.0), axis=1, keepdims=True)
            los, his = [], []
            for j in range(SLAB_ROWS):
                lo, hi = _unpack_words(tile_ref[j * TILE_STRIDE:j * TILE_STRIDE + PEER_PAIRS, :])
                los.append(jnp.sum(lo * w_col, axis=0, keepdims=True))
                his.append(jnp.sum(hi * w_col, axis=0, keepdims=True))
            rows.append(jnp.concatenate(los + his, axis=1))
        out_sc[_trip_rows(g), :] = jnp.concatenate(rows, axis=0)
        return carry

    lax.fori_loop(0, tm // GATHER_TOKENS, tok_group, 0)
    y_ref[...] = _layer_norm(ALPHA * h2_ref[...] + out_sc[...], g_ref[...], b_ref[...])


def _peer_scratch(tm):
    assert GATHER_TOKENS == _SUBLANES and tm % GATHER_TOKENS == 0
    return ([pltpu.SMEM((tm, PEER_PAIRS), I32), pltpu.SemaphoreType.DMA(())]
            + [pltpu.VMEM((SLAB_ROWS * TILE_STRIDE, LANES), I32) for _ in range(GATHER_TOKENS)])


def _peer_in(e_rows, u_tab, h2, gate_t):
    nb, _, tm = gate_t.shape
    pairs = pl.BlockSpec((1, PEER_PAIRS, tm), lambda i: (i, 0, 0))
    return pl.pallas_call(
        _peer_in_kernel,
        grid=(nb,),
        in_specs=[pl.BlockSpec(memory_space=pl.ANY), pl.BlockSpec(memory_space=pltpu.VMEM),
                  pl.BlockSpec((tm, D_MODEL), lambda i: (i, 0)), pairs],
        out_specs=pairs,
        out_shape=jax.ShapeDtypeStruct((nb, PEER_PAIRS, tm), F32),
        scratch_shapes=_peer_scratch(tm),
        compiler_params=_cparams(("arbitrary",)),
        name="peer_expert_in",
    )(e_rows, u_tab, h2, gate_t)


def _peer_out(e_rows, v_tab, h2, w_t, g, b):
    nb, _, tm = w_t.shape
    pairs = pl.BlockSpec((1, PEER_PAIRS, tm), lambda i: (i, 0, 0))
    row = pl.BlockSpec((tm, D_MODEL), lambda i: (i, 0))
    vec = pl.BlockSpec((1, D_MODEL), lambda i: (0, 0))
    scratch = _peer_scratch(tm)
    return pl.pallas_call(
        _peer_out_kernel,
        grid=(nb,),
        in_specs=[pl.BlockSpec(memory_space=pl.ANY), pl.BlockSpec(memory_space=pltpu.VMEM), row, pairs, vec, vec],
        out_specs=row,
        out_shape=jax.ShapeDtypeStruct((nb * tm, D_MODEL), F32),
        scratch_shapes=scratch[:2] + [pltpu.VMEM((tm, D_MODEL), F32)] + scratch[2:],
        compiler_params=_cparams(("arbitrary",)),
        name="peer_expert_out_ln3",
    )(e_rows, v_tab, h2, w_t, g, b)


def _lane_row(vals, offset):
    return jnp.zeros((1, LANES), F32).at[0, offset:offset + vals.shape[0]].set(vals)


def _finish(x2d, ssd2d, att2d, mem_attend_fn, p):
    h1, qm = _out_proj(x2d, ssd2d, att2d, p["w_out"], p["ln1_g"], p["ln1_b"], p["w_cq"])
    o = mem_attend_fn(qm)
    h2, e_t, gate_t = _peer_route(h1, o, p["w_co"], p["ln2_g"], p["ln2_b"], p["peer_wq"], p["subkeys"])
    e_rows = jnp.swapaxes(e_t, 1, 2)
    w_t = _peer_in(e_rows, p["u_tab"], h2, gate_t)
    return _peer_out(e_rows, p["v_tab"], h2, w_t, p["ln3_g"], p["ln3_b"])


def kernel(x_prompt, mem_prompt, x_sample, cache_k, cache_v, cache_kidx, cache_mem_k, cache_mem_v, state_ssm, state_conv, page_table, w_in, conv_w, conv_b, dt_bias, a_log, d_skip, ssd_norm_w, rel_bias, w_out, ln1_g, ln1_b, w_cq, w_ck, w_cv, w_co, ln2_g, ln2_b, peer_wq, peer_subkeys, peer_u, peer_v, ln3_g, ln3_b):
    bp, seq, _ = x_prompt.shape
    bs, t_s, _ = x_sample.shape
    m_tok = mem_prompt.shape[1]

    c0 = SSD_INNER + CONV_DIM
    wa = w_in[:, :c0].astype(MXU_DTYPE)
    c1 = c0 + SSD_HEADS
    wb = w_in[:, c1:c1 + 4 * ATT_INNER].astype(MXU_DTYPE)
    c2 = c1 + 4 * ATT_INNER
    w_kidx = w_in[:, c2:c2 + IDX_DIM]
    wc = jnp.concatenate([w_kidx, w_kidx], axis=1).astype(MXU_DTYPE)
    wd = jnp.concatenate([w_kidx, w_in[:, c0:c1], w_in[:, c2 + IDX_DIM:],
                          jnp.zeros((D_MODEL, LANES - IDX_DIM - SSD_HEADS - IDX_HEADS), F32)], axis=1).astype(MXU_DTYPE)
    row = lambda v: v.reshape(1, -1)
    p = dict(
        w_out=w_out.astype(MXU_DTYPE), ln1_g=row(ln1_g), ln1_b=row(ln1_b), w_cq=w_cq.astype(MXU_DTYPE),
        w_co=w_co.astype(MXU_DTYPE), ln2_g=row(ln2_g), ln2_b=row(ln2_b),
        peer_wq=peer_wq.astype(MXU_DTYPE).reshape(D_MODEL, PEER_HEADS * 2, PEER_HALF).swapaxes(0, 1),
        subkeys=peer_subkeys.reshape(PEER_HEADS * 2, PEER_KEYS, PEER_HALF).astype(MXU_DTYPE),
        u_tab=_pack_table(peer_u), v_tab=_pack_table(peer_v), ln3_g=row(ln3_g), ln3_b=row(ln3_b))
    ssd_params = (conv_w, row(conv_b), _lane_row(a_log, DT_LANE), _lane_row(dt_bias, DT_LANE),
                  row(jnp.repeat(d_skip, SSD_HEAD_DIM)), row(ssd_norm_w))
    state_t = lambda s: jnp.swapaxes(s, -1, -2)

    n_p = bp * seq
    z, xbc, q_b, k_p, v_p, k_b, v_b, qi_b, kk_b, small = _project_in(x_prompt.reshape(n_p, D_MODEL), wa, wb, wc, wd)
    r3 = lambda a, b_: a.reshape(b_, -1, a.shape[-1])
    ssd_out, ssm_t, conv_p = _ssd_mixer(
        r3(z, bp), r3(xbc, bp), r3(small, bp), jnp.zeros((bp, SSD_CONV - 1, CONV_DIM), F32),
        jnp.zeros((bp, SSD_HEADS, SSD_STATE, SSD_HEAD_DIM), F32), SSD_CHUNK, *ssd_params)
    att_out = _dsa_prompt(r3(q_b, bp), r3(qi_b, bp), r3(small, bp), r3(kk_b, bp), r3(k_b, bp), r3(v_b, bp), rel_bias)
    kidx_p = r3(small, bp)[:, :, :IDX_DIM]
    mem_kv = _matmul(mem_prompt.reshape(bp * m_tok, D_MODEL),
                     jnp.concatenate([w_ck, w_cv], axis=1).astype(MXU_DTYPE))
    memk_p = mem_kv[:, :MEM_INNER].reshape(bp, m_tok, MEM_INNER)
    memv_p = mem_kv[:, MEM_INNER:].reshape(bp, m_tok, MEM_INNER)
    y_prompt = _finish(x_prompt.reshape(n_p, D_MODEL), ssd_out.reshape(n_p, SSD_INNER),
                       att_out.reshape(n_p, ATT_INNER),
                       lambda qm: _mem_attend(qm.reshape(bp, seq, MEM_INNER), memk_p, memv_p).reshape(n_p, MEM_INNER),
                       p)

    n_s = bs * t_s
    n_pad = -(-n_s // TOK_BLOCK) * TOK_BLOCK
    xs2d = jnp.pad(x_sample.reshape(n_s, D_MODEL), ((0, n_pad - n_s), (0, 0)))
    outs = _project_in(xs2d, wa, wb, wc, wd)
    z, xbc, q_b, k_s, v_s, k_b, v_b, qi_b, kk_b, small = [r3(a[:n_s], bs) for a in outs]
    pad_chunk = lambda a: jnp.pad(a, ((0, 0), (0, SSD_CHUNK - t_s), (0, 0)))
    ssd_s, ssm_s_t, conv_s = _ssd_mixer(pad_chunk(z), pad_chunk(xbc), pad_chunk(small), state_conv,
                                        state_t(state_ssm), t_s, *ssd_params)
    att_s = _dsa_sample(q_b, qi_b, small, k_b, v_b, kk_b, cache_k, cache_v, cache_kidx, page_table, rel_bias)
    pad_rows = lambda a: jnp.pad(a.reshape(n_s, -1), ((0, n_pad - n_s), (0, 0)))
    cmk = cache_mem_k.reshape(bs, m_tok, MEM_INNER)
    cmv = cache_mem_v.reshape(bs, m_tok, MEM_INNER)
    y_s = _finish(xs2d, pad_rows(ssd_s[:, :t_s]), pad_rows(att_s),
                  lambda qm: pad_rows(_mem_attend(qm[:n_s].reshape(bs, t_s, MEM_INNER), cmk, cmv)), p)
    y_sample = y_s[:n_s].reshape(bs, t_s, D_MODEL)

    heads = lambda a, b_: a.reshape(b_, -1, ATT_HEADS, ATT_HEAD_DIM)
    mem_heads = lambda a: a.reshape(bp, m_tok, MEM_HEADS, MEM_HEAD_DIM)
    return (y_prompt.reshape(bp, seq, D_MODEL), y_sample,
            heads(k_p, bp), heads(v_p, bp), kidx_p, mem_heads(memk_p), mem_heads(memv_p),
            state_t(ssm_t), conv_p,
            heads(k_s, bs), heads(v_s, bs), small[:, :, :IDX_DIM], state_t(ssm_s_t), conv_s)
```

```python
import functools
import math

import jax
import jax.numpy as jnp
import numpy as np
from jax import lax
from jax.experimental import pallas as pl
from jax.experimental.pallas import tpu as pltpu

F32 = jnp.float32
BF16 = jnp.bfloat16
I32 = jnp.int32
MXU_DTYPE = BF16

D_MODEL = 1024
DEPTH = 1
ALPHA = (2.0 * DEPTH) ** 0.25
LN_EPS = 1e-5

SSD_HEADS = 8
SSD_HEAD_DIM = 64
SSD_INNER = 512
SSD_GROUPS = 2
SSD_STATE = 128
SSD_CONV = 4
SSD_CHUNK = 128
CONV_DIM = 1024

ATT_HEADS = 8
ATT_HEAD_DIM = 64
ATT_INNER = 512
IDX_HEADS = 8
IDX_DIM = 64
TOPK_MAX = 256
PAGE_SIZE = 128
N_BUCKETS = 32
MAX_DISTANCE = 128

MEM_HEADS = 4
MEM_HEAD_DIM = 128
MEM_INNER = 512

PEER_HEADS = 8
PEER_KEYS = 128
PEER_HALF = 128
PEER_TOPK = 16
PEER_PAIRS = PEER_HEADS * PEER_TOPK

LANES = 128
CHUNK = 2 * LANES
TOK_BLOCK = 128
DT_LANE = 64
WIDX_LANE = 72
NEG_BIG = -1e30
INT_MIN = -(2 ** 31)
VMEM_LIMIT = 56 * 1024 * 1024
SLAB_ROWS = 4
FEAT_CHUNKS = D_MODEL // LANES
TILE_STRIDE = 136
GATHER_TOKENS = 8


def _cparams(sem, vmem=VMEM_LIMIT):
    return pltpu.CompilerParams(dimension_semantics=sem, vmem_limit_bytes=vmem)


def _mm(a, b):
    return jnp.dot(a.astype(MXU_DTYPE), b.astype(MXU_DTYPE), preferred_element_type=F32)


def _mm_nt(a, b):
    return lax.dot_general(a.astype(MXU_DTYPE), b.astype(MXU_DTYPE), (((1,), (1,)), ((), ())),
                           preferred_element_type=F32)


def _mm_exact(a, b):
    return jnp.dot(a, b, preferred_element_type=F32, precision=lax.Precision.HIGHEST)


def _sigmoid(x):
    return 1.0 / (1.0 + jnp.exp(-x))


def _layer_norm(x, g, b):
    mu = jnp.mean(x, axis=-1, keepdims=True)
    xc = x - mu
    var = jnp.mean(xc * xc, axis=-1, keepdims=True)
    return xc * lax.rsqrt(var + LN_EPS) * g + b


def _bucket_thresholds():
    max_exact = N_BUCKETS // 2
    out = []
    for n in range(max_exact, MAX_DISTANCE + 1):
        v = max_exact + int(math.log(n / max_exact) / math.log(MAX_DISTANCE / max_exact) * (N_BUCKETS - max_exact))
        out.append(min(v, N_BUCKETS - 1))
    thr = []
    for bkt in range(max_exact + 1, N_BUCKETS):
        thr.append(max_exact + next(i for i, v in enumerate(out) if v >= bkt))
    return tuple(thr)


_BUCKET_THR = _bucket_thresholds()


def _rel_bucket(dist):
    n = jnp.maximum(dist, 0)
    max_exact = N_BUCKETS // 2
    large = jnp.full(n.shape, max_exact, I32)
    for t in _BUCKET_THR:
        large = large + (n >= t).astype(I32)
    return jnp.where(n < max_exact, n, large)


def _monotone_key(x):
    b = lax.bitcast_convert_type(x, I32)
    return b ^ ((b >> 31) & 0x7FFFFFFF)


def _proj_in_kernel(x_ref, wa_ref, wb_ref, wc_ref, wd_ref,
                    z_ref, xbc_ref, q_ref, k_ref, v_ref, kb_ref, vb_ref, qi_ref, kk_ref, sm_ref):
    xb = x_ref[...].astype(MXU_DTYPE)
    z_ref[...] = jnp.dot(xb, wa_ref[:, :SSD_INNER], preferred_element_type=F32)
    xbc_ref[...] = jnp.dot(xb, wa_ref[:, SSD_INNER:], preferred_element_type=F32)
    part = lambda n: wb_ref[:, n * ATT_INNER:(n + 1) * ATT_INNER]
    q = jnp.dot(xb, part(0), preferred_element_type=F32)
    q_ref[...] = (q * (ATT_HEAD_DIM ** -0.5)).astype(q_ref.dtype)
    k = jnp.dot(xb, part(1), preferred_element_type=F32)
    k_ref[...] = k
    kb_ref[...] = k.astype(kb_ref.dtype)
    v = jnp.dot(xb, part(2), preferred_element_type=F32)
    v_ref[...] = v
    vb_ref[...] = v.astype(vb_ref.dtype)
    qi = jnp.dot(xb, part(3), preferred_element_type=F32)
    qi_ref[...] = (qi * (IDX_DIM ** -0.5)).astype(qi_ref.dtype)
    kk_ref[...] = jnp.dot(xb, wc_ref[...], preferred_element_type=F32).astype(kk_ref.dtype)
    sm = jnp.dot(xb, wd_ref[...], preferred_element_type=F32)
    lane = lax.broadcasted_iota(I32, sm.shape, 1)
    is_w = (lane >= WIDX_LANE) & (lane < WIDX_LANE + IDX_HEADS)
    sm_ref[...] = jnp.where(is_w, sm * (IDX_HEADS ** -0.5), sm)


def _project_in(x2d, wa, wb, wc, wd):
    n = x2d.shape[0]
    tm = min(256, n)
    row = lambda w: pl.BlockSpec((tm, w), lambda i: (i, 0))
    full = lambda a: pl.BlockSpec(a.shape, lambda i: (0, 0))
    out_shapes = (
        jax.ShapeDtypeStruct((n, SSD_INNER), F32),
        jax.ShapeDtypeStruct((n, CONV_DIM), F32),
        jax.ShapeDtypeStruct((n, ATT_INNER), MXU_DTYPE),
        jax.ShapeDtypeStruct((n, ATT_INNER), F32),
        jax.ShapeDtypeStruct((n, ATT_INNER), F32),
        jax.ShapeDtypeStruct((n, ATT_INNER), MXU_DTYPE),
        jax.ShapeDtypeStruct((n, ATT_INNER), MXU_DTYPE),
        jax.ShapeDtypeStruct((n, ATT_INNER), MXU_DTYPE),
        jax.ShapeDtypeStruct((n, LANES), MXU_DTYPE),
        jax.ShapeDtypeStruct((n, LANES), F32),
    )
    return pl.pallas_call(
        _proj_in_kernel,
        grid=(n // tm,),
        in_specs=[row(D_MODEL), full(wa), full(wb), full(wc), full(wd)],
        out_specs=tuple(row(s.shape[1]) for s in out_shapes),
        out_shape=out_shapes,
        compiler_params=_cparams(("parallel",)),
        name="project_in",
    )(x2d, wa, wb, wc, wd)


def _ssd_kernel(n_valid, z_ref, xbc_ref, sm_ref, conv0_ref, h0_ref, convw_ref, convb_ref, alog_ref, dtb_ref,
                dskip_ref, normw_ref, y_ref, hout_ref, tail_ref, cbuf, hst, ysc):
    q_len = z_ref.shape[1]
    c = pl.program_id(1)
    pad = 8

    @pl.when(c == 0)
    def _():
        cbuf[pad - 3:pad, :] = conv0_ref[0]
        hst[...] = h0_ref[0]

    xbc = xbc_ref[0]
    cbuf[pad:pad + q_len, :] = xbc
    w = convw_ref[...]
    conv = cbuf[pad - 3:pad - 3 + q_len, :] * w[0:1]
    conv = conv + cbuf[pad - 2:pad - 2 + q_len, :] * w[1:2]
    conv = conv + cbuf[pad - 1:pad - 1 + q_len, :] * w[2:3]
    conv = conv + xbc * w[3:4]
    conv = conv + convb_ref[...]
    xc = conv * _sigmoid(conv)
    tail = cbuf[pad + n_valid - 3:pad + n_valid, :]
    cbuf[pad - 3:pad, :] = tail

    row = lax.broadcasted_iota(I32, (q_len, LANES), 0)
    lane = lax.broadcasted_iota(I32, (q_len, LANES), 1)
    is_dt = (lane >= DT_LANE) & (lane < DT_LANE + SSD_HEADS) & (row < n_valid)
    raw = sm_ref[0] + dtb_ref[...]
    dt_full = jnp.maximum(raw, 0.0) + jnp.log1p(jnp.exp(-jnp.abs(raw)))
    dtm = jnp.where(is_dt, dt_full, 0.0)
    a_full = dtm * (-jnp.exp(alog_ref[...]))
    ri = lax.broadcasted_iota(I32, (q_len, q_len), 0)
    ci = lax.broadcasted_iota(I32, (q_len, q_len), 1)
    causal = ci <= ri
    acum = _mm_exact(causal.astype(F32), a_full)
    acum_t = acum.T

    xs = xc[:, :SSD_INNER]
    for g in range(SSD_GROUPS):
        bm = xc[:, SSD_INNER + g * SSD_STATE:SSD_INNER + (g + 1) * SSD_STATE]
        cm = xc[:, SSD_INNER + (SSD_GROUPS + g) * SSD_STATE:SSD_INNER + (SSD_GROUPS + g + 1) * SSD_STATE]
        scores = _mm_nt(cm, bm)
        bm_t = bm.T
        for hh in range(SSD_HEADS // SSD_GROUPS):
            h = g * (SSD_HEADS // SSD_GROUPS) + hh
            col = acum[:, DT_LANE + h:DT_LANE + h + 1]
            rowv = acum_t[DT_LANE + h:DT_LANE + h + 1, :]
            a_last = acum[q_len - 1:q_len, DT_LANE + h:DT_LANE + h + 1]
            decay = jnp.exp(jnp.where(causal, col - rowv, -jnp.inf))
            x_h = xs[:, h * SSD_HEAD_DIM:(h + 1) * SSD_HEAD_DIM]
            xd = x_h * dtm[:, DT_LANE + h:DT_LANE + h + 1]
            h_t = hst[h]
            y = _mm(scores * decay, xd) + _mm(cm, h_t) * jnp.exp(col)
            y = y + x_h * dskip_ref[:, h * SSD_HEAD_DIM:(h + 1) * SSD_HEAD_DIM]
            ysc[:, h * SSD_HEAD_DIM:(h + 1) * SSD_HEAD_DIM] = y
            hst[h] = h_t * jnp.exp(a_last) + _mm(bm_t, xd * jnp.exp(a_last - col))

    zz = z_ref[0]
    hg = ysc[...] * (zz * _sigmoid(zz))
    hg = hg * lax.rsqrt(jnp.mean(hg * hg, axis=-1, keepdims=True) + LN_EPS)
    y_ref[0] = hg * normw_ref[...]

    @pl.when(c == pl.num_programs(1) - 1)
    def _():
        hout_ref[0] = hst[...]
        tail_ref[0] = cbuf[pad - 3:pad, :]


def _ssd_mixer(z, xbc, small, conv0, h0_t, n_valid, conv_w, conv_b, alog_lane, dtb_lane, dskip_row, norm_w):
    b, length, _ = z.shape
    q_len = SSD_CHUNK
    nc = length // q_len
    blk = lambda w: pl.BlockSpec((1, q_len, w), lambda i, c: (i, c, 0))
    per_b3 = lambda s: pl.BlockSpec((1,) + s, lambda i, c: (i,) + (0,) * len(s))
    const2 = lambda a: pl.BlockSpec(a.shape, lambda i, c: (0, 0))
    st_shape = (SSD_HEADS, SSD_STATE, SSD_HEAD_DIM)
    return pl.pallas_call(
        functools.partial(_ssd_kernel, n_valid),
        grid=(b, nc),
        in_specs=[blk(SSD_INNER), blk(CONV_DIM), blk(LANES), per_b3((SSD_CONV - 1, CONV_DIM)), per_b3(st_shape),
                  const2(conv_w), const2(conv_b), const2(alog_lane), const2(dtb_lane), const2(dskip_row),
                  const2(norm_w)],
        out_specs=(blk(SSD_INNER), per_b3(st_shape), per_b3((SSD_CONV - 1, CONV_DIM))),
        out_shape=(jax.ShapeDtypeStruct((b, length, SSD_INNER), F32),
                   jax.ShapeDtypeStruct((b,) + st_shape, F32),
                   jax.ShapeDtypeStruct((b, SSD_CONV - 1, CONV_DIM), F32)),
        scratch_shapes=[pltpu.VMEM((q_len + 8, CONV_DIM), F32), pltpu.VMEM(st_shape, F32),
                        pltpu.VMEM((q_len, SSD_INNER), F32)],
        compiler_params=_cparams(("parallel", "arbitrary")),
        name="ssd_mixer",
    )(z, xbc, small, conv0, h0_t, conv_w, conv_b, alog_lane, dtb_lane, dskip_row, norm_w)


def _kth_largest_key(count_ge, k, rows):
    def body(i, tau):
        cand = tau + jnp.left_shift(jnp.int32(1), 31 - i)
        return jnp.where(count_ge(cand) >= k, cand, tau)
    return lax.fori_loop(0, 32, body, jnp.full((rows, 1), INT_MIN, I32))


def _bias_tiles(rb_ref, bias_sc, n_tiles):
    a = lax.broadcasted_iota(I32, (LANES, LANES), 0)
    bcol = lax.broadcasted_iota(I32, (LANES, LANES), 1)
    for d in range(n_tiles):
        bucket = _rel_bucket(d * LANES + a - bcol)
        for h in range(ATT_HEADS):
            far = rb_ref[N_BUCKETS - 1, h]
            tile = jnp.zeros((LANES, LANES), F32)
            for kb in range(N_BUCKETS):
                tile = jnp.where(bucket == kb, rb_ref[kb, h] - far, tile)
            bias_sc[d, h] = tile


def _dsa_prompt_kernel(topk, q_ref, qi_ref, sm_ref, kk_ref, kb_ref, vb_ref, rb_ref, o_ref,
                       keys_sc, bias_sc, qm_sc, qim_sc, wb_sc, *head_sc):
    nh = ATT_HEADS
    cache, m_l, l_l, acc = head_sc[:nh], head_sc[nh:2 * nh], head_sc[2 * nh:3 * nh], head_sc[3 * nh:]
    i = pl.program_id(1)
    tq = q_ref.shape[1]
    n_far = bias_sc.shape[0] - 1
    n_chunks = (i + 2) // 2

    @pl.when(i == 0)
    def _():
        _bias_tiles(rb_ref, bias_sc, n_far + 1)

    lane = lax.broadcasted_iota(I32, (tq, LANES), 1)
    low_half = lane < ATT_HEAD_DIM
    lane2 = lax.broadcasted_iota(I32, (tq, CHUNK), 1)
    row2 = lax.broadcasted_iota(I32, (tq, CHUNK), 0)
    sm = sm_ref[0]

    for h in range(nh):
        keep = low_half if h % 2 == 0 else jnp.logical_not(low_half)
        pair = slice((h // 2) * LANES, (h // 2 + 1) * LANES)
        qp, qip = q_ref[0, :, pair], qi_ref[0, :, pair]
        qm_sc[h] = jnp.where(keep, qp, jnp.zeros_like(qp))
        qim_sc[h] = jnp.where(keep, qip, jnp.zeros_like(qip))
        wb_sc[h] = jnp.broadcast_to(sm[:, WIDX_LANE + h:WIDX_LANE + h + 1], (tq, LANES))
        m_l[h][...] = jnp.full((tq, LANES), NEG_BIG, F32)
        l_l[h][...] = jnp.zeros((tq, LANES), F32)
        acc[h][...] = jnp.zeros((tq, LANES), F32)

    def causal(c):
        return (c * CHUNK + lane2) <= (i * tq + row2)

    def chunk_logits(c, carry):
        start = pl.multiple_of(c * CHUNK, CHUNK)
        kk = kk_ref[0, pl.ds(start, CHUNK), :]
        sc = jnp.zeros((tq, CHUNK), F32)
        for h in range(IDX_HEADS):
            wb = wb_sc[h]
            sc = sc + jnp.maximum(_mm_nt(qim_sc[h], kk), 0.0) * jnp.concatenate([wb, wb], axis=1)
        keys_sc[c] = _monotone_key(jnp.where(causal(c), sc, -jnp.inf))
        d0 = jnp.clip(i - 2 * c, 0, n_far)
        d1 = jnp.clip(i - 2 * c - 1, 0, n_far)
        for h in range(nh):
            kblk = kb_ref[0, pl.ds(start, CHUNK), (h // 2) * LANES:(h // 2 + 1) * LANES]
            bias = jnp.concatenate([bias_sc[d0, h], bias_sc[d1, h]], axis=1)
            cache[h][c] = _mm_nt(qm_sc[h], kblk) + bias
        return carry

    lax.fori_loop(0, n_chunks, chunk_logits, 0)

    def count_ge(cand):
        def body(c, cnt):
            k = keys_sc[c]
            return cnt + jnp.where(k[:, :LANES] >= cand, 1.0, 0.0) + jnp.where(k[:, LANES:] >= cand, 1.0, 0.0)
        cnt = lax.fori_loop(0, n_chunks, body, jnp.zeros((tq, LANES), F32))
        return jnp.sum(cnt, axis=1, keepdims=True)

    tau = _kth_largest_key(count_ge, float(topk), tq)

    def chunk_max(c, carry):
        sel = (keys_sc[c] >= tau) & causal(c)
        for h in range(nh):
            s = jnp.where(sel, cache[h][c], NEG_BIG)
            cache[h][c] = s
            m_l[h][...] = jnp.maximum(m_l[h][...], jnp.maximum(s[:, :LANES], s[:, LANES:]))
        return carry

    lax.fori_loop(0, n_chunks, chunk_max, 0)
    for h in range(nh):
        m_l[h][...] = jnp.broadcast_to(jnp.max(m_l[h][...], axis=1, keepdims=True), (tq, LANES))

    def chunk_pv(c, carry):
        start = pl.multiple_of(c * CHUNK, CHUNK)
        for h in range(nh):
            mb = m_l[h][...]
            s = cache[h][c]
            p0 = jnp.exp(s[:, :LANES] - mb)
            p1 = jnp.exp(s[:, LANES:] - mb)
            l_l[h][...] = l_l[h][...] + (p0 + p1)
            vblk = vb_ref[0, pl.ds(start, CHUNK), (h // 2) * LANES:(h // 2 + 1) * LANES]
            acc[h][...] = acc[h][...] + _mm(jnp.concatenate([p0, p1], axis=1), vblk)
        return carry

    lax.fori_loop(0, n_chunks, chunk_pv, 0)

    for g in range(nh // 2):
        lo = acc[2 * g][...] / jnp.sum(l_l[2 * g][...], axis=1, keepdims=True)
        hi = acc[2 * g + 1][...] / jnp.sum(l_l[2 * g + 1][...], axis=1, keepdims=True)
        o_ref[0, :, g * LANES:(g + 1) * LANES] = jnp.where(low_half, lo, hi)


def _dsa_prompt(q_b, qi_b, small, kk_b, k_b, v_b, rel_bias):
    b, s, _ = q_b.shape
    tq = LANES
    nq = s // tq
    topk = min(TOPK_MAX, s // 4)
    n_bias_tiles = 3
    assert MAX_DISTANCE <= LANES + 1 and s % CHUNK == 0
    n_chunks = s // CHUNK
    qblk = lambda w: pl.BlockSpec((1, tq, w), lambda i, j: (i, j, 0))
    per_b = lambda w: pl.BlockSpec((1, s, w), lambda i, j: (i, 0, 0))
    per_head = lambda shape, dt: [pltpu.VMEM(shape, dt) for _ in range(ATT_HEADS)]
    return pl.pallas_call(
        functools.partial(_dsa_prompt_kernel, topk),
        grid=(b, nq),
        in_specs=[qblk(ATT_INNER), qblk(ATT_INNER), qblk(LANES), per_b(LANES), per_b(ATT_INNER), per_b(ATT_INNER),
                  pl.BlockSpec(memory_space=pltpu.SMEM)],
        out_specs=qblk(ATT_INNER),
        out_shape=jax.ShapeDtypeStruct((b, s, ATT_INNER), F32),
        scratch_shapes=[pltpu.VMEM((n_chunks, tq, CHUNK), I32),
                        pltpu.VMEM((n_bias_tiles, ATT_HEADS, LANES, LANES), F32),
                        pltpu.VMEM((ATT_HEADS, tq, LANES), q_b.dtype), pltpu.VMEM((ATT_HEADS, tq, LANES), qi_b.dtype),
                        pltpu.VMEM((ATT_HEADS, tq, LANES), F32)]
                       + per_head((n_chunks, tq, CHUNK), F32)
                       + per_head((tq, LANES), F32)
                       + per_head((tq, LANES), F32)
                       + per_head((tq, LANES), F32),
        compiler_params=_cparams(("parallel", "arbitrary")),
        name="dsa_prompt",
    )(q_b, qi_b, small, kk_b, k_b, v_b, rel_bias)


def _dsa_sample_score_kernel(topk, t_new, pt_ref, qi_ref, w_ref, *refs):
    kidx_refs, (kknew_ref, keys_ref, keysnew_ref, tau_ref, keys_sc) = refs[:-5], refs[-5:]
    pg = len(kidx_refs)
    j = pl.program_id(1)
    n_pages = pl.num_programs(1) * pg
    rows = qi_ref.shape[1]
    lane = lax.broadcasted_iota(I32, (t_new, LANES), 1)
    rowi = lax.broadcasted_iota(I32, (t_new, LANES), 0)

    def head_sum(s):
        s = jnp.maximum(s, 0.0) * w_ref[0]
        return jnp.concatenate(
            [jnp.sum(s[t * IDX_HEADS:(t + 1) * IDX_HEADS], axis=0, keepdims=True) for t in range(t_new)], axis=0)

    for r, kidx_ref in enumerate(kidx_refs):
        keys_sc[j * pg + r] = _monotone_key(head_sum(_mm(qi_ref[0], kidx_ref[0])))

    @pl.when(j == pl.num_programs(1) - 1)
    def _():
        sc = head_sum(_mm(qi_ref[0], kknew_ref[0]))
        keys_new = _monotone_key(jnp.where(lane <= rowi, sc, -jnp.inf))
        keys_sc[n_pages] = keys_new

        def count_ge(cand):
            def body(jj, cnt):
                return cnt + jnp.where(keys_sc[jj] >= cand, 1.0, 0.0)
            cnt = lax.fori_loop(0, n_pages + 1, body, jnp.zeros((t_new, LANES), F32))
            return jnp.sum(cnt, axis=1, keepdims=True)

        tau = _kth_largest_key(count_ge, float(topk), t_new)
        tau_ref[0] = jnp.broadcast_to(tau, (t_new, LANES))
        keys_ref[0] = keys_sc[0:n_pages]
        keysnew_ref[0] = keys_new


def _dsa_sample_attn_kernel(t_new, past, pg, pt_ref, q_ref, keys_ref, keysnew_ref, tau_ref, knew_ref, vnew_ref,
                            rbrow_ref, *refs):
    k_refs, v_refs = refs[:pg], refs[pg:2 * pg]
    o_ref, cache_sc, m_sc, l_sc, acc_sc = refs[2 * pg:]
    phase = pl.program_id(1)
    j = pl.program_id(2)
    n_steps = pl.num_programs(2)
    n_pages = n_steps * pg
    rows = q_ref.shape[1]
    lane = lax.broadcasted_iota(I32, (rows, LANES), 1)
    rowi = lax.broadcasted_iota(I32, (rows, LANES), 0)
    tok = lax.rem(rowi, t_new)
    q = q_ref[0]

    def rows_of(x_t):
        return jnp.concatenate([x_t] * ATT_HEADS, axis=0)

    tau = rows_of(tau_ref[0])

    def page_rows(ref):
        return ref[0].reshape(ATT_INNER, PAGE_SIZE)

    def near_bias(key_pos):
        bucket = _rel_bucket(past + tok - key_pos)
        far = rbrow_ref[:, N_BUCKETS - 1:N_BUCKETS]
        bias = jnp.zeros((rows, LANES), F32)
        for kb in range(N_BUCKETS - 1):
            bias = jnp.where(bucket == kb, rbrow_ref[:, kb:kb + 1] - far, bias)
        return bias

    @pl.when(phase == 0)
    def _():
        for r in range(pg):
            s = _mm(q, page_rows(k_refs[r]))
            cache_sc[j * pg + r] = jnp.where(rows_of(keys_ref[0, r]) >= tau, s, NEG_BIG)

        @pl.when(j == n_steps - 1)
        def _():
            last = n_pages - 1
            cache_sc[last] = cache_sc[last] + near_bias(last * PAGE_SIZE + lane)
            s = _mm(q, knew_ref[0]) + near_bias(past + lane)
            ok = (rows_of(keysnew_ref[0]) >= tau) & (lane <= tok)
            cache_sc[n_pages] = jnp.where(ok, s, NEG_BIG)
            m = lax.fori_loop(0, n_pages + 1, lambda jj, m: jnp.maximum(m, cache_sc[jj]),
                              jnp.full((rows, LANES), NEG_BIG, F32))
            m_sc[...] = jnp.broadcast_to(jnp.max(m, axis=1, keepdims=True), (rows, LANES))
            l_sc[...] = jnp.zeros(l_sc.shape, F32)
            acc_sc[...] = jnp.zeros(acc_sc.shape, F32)

    @pl.when(phase == 1)
    def _():
        mb = m_sc[...]
        acc = acc_sc[...]
        lsum = l_sc[...]
        for r in range(pg):
            p = jnp.exp(cache_sc[j * pg + r] - mb)
            lsum = lsum + p
            acc = acc + _mm_nt(p, page_rows(v_refs[r]))
        l_sc[...] = lsum
        acc_sc[...] = acc

        @pl.when(j == n_steps - 1)
        def _():
            p_new = jnp.exp(cache_sc[n_pages] - mb)
            l = jnp.sum(lsum + p_new, axis=1, keepdims=True)
            a = (acc + _mm_nt(p_new, vnew_ref[0])) / l
            col_head = lax.broadcasted_iota(I32, (t_new, ATT_INNER), 1) // ATT_HEAD_DIM
            out = jnp.zeros((t_new, ATT_INNER), F32)
            for h in range(ATT_HEADS):
                out = jnp.where(col_head == h, a[h * t_new:(h + 1) * t_new], out)
            o_ref[0] = out


def _dsa_sample(q_b, qi_b, small, k_b, v_b, kk_b, cache_k, cache_v, cache_kidx, page_table, rel_bias):
    b, t_new, _ = q_b.shape
    n_pages = page_table.shape[1]
    past = n_pages * PAGE_SIZE
    topk = min(TOPK_MAX, (past + t_new) // 4)
    rows = t_new * ATT_HEADS
    assert PAGE_SIZE >= MAX_DISTANCE

    def pages_per_step(limit):
        return max(g for g in (1, 2, 4, 8, 16, 32) if g <= limit and n_pages % g == 0)

    kidx_t = jnp.transpose(cache_kidx, (0, 2, 1))
    k_t = jnp.transpose(cache_k, (0, 2, 3, 1))
    v_t = jnp.transpose(cache_v, (0, 2, 3, 1))
    new_cols = lambda x: jnp.pad(jnp.swapaxes(x, 1, 2), ((0, 0), (0, 0), (0, PAGE_SIZE - t_new)))

    pgs = pages_per_step(32)
    qi_rows = qi_b.reshape(b, rows, IDX_DIM)
    w_rows = small[:, :, WIDX_LANE:WIDX_LANE + IDX_HEADS].reshape(b, rows, 1)
    kk_new = new_cols(kk_b[:, :, :IDX_DIM])
    per_b = lambda s: pl.BlockSpec((1,) + s, lambda i, j, pt: (i,) + (0,) * len(s))
    kidx_page = lambda r: pl.BlockSpec((1, IDX_DIM, PAGE_SIZE), lambda i, j, pt: (pt[i, j * pgs + r], 0, 0))
    keys, keys_new, tau = pl.pallas_call(
        functools.partial(_dsa_sample_score_kernel, topk, t_new),
        grid_spec=pltpu.PrefetchScalarGridSpec(
            num_scalar_prefetch=1, grid=(b, n_pages // pgs),
            in_specs=[per_b((rows, IDX_DIM)), per_b((rows, 1))] + [kidx_page(r) for r in range(pgs)]
                     + [per_b((IDX_DIM, LANES))],
            out_specs=(per_b((n_pages, t_new, LANES)), per_b((t_new, LANES)), per_b((t_new, LANES))),
            scratch_shapes=[pltpu.VMEM((n_pages + 1, t_new, LANES), I32)]),
        out_shape=(jax.ShapeDtypeStruct((b, n_pages, t_new, LANES), I32),
                   jax.ShapeDtypeStruct((b, t_new, LANES), I32),
                   jax.ShapeDtypeStruct((b, t_new, LANES), I32)),
        compiler_params=_cparams(("parallel", "arbitrary")),
        name="dsa_sample_scores",
    )(page_table, qi_rows, w_rows, *([kidx_t] * pgs), kk_new)

    pg = pages_per_step(16)
    n_steps = n_pages // pg
    head_of_col = jnp.arange(ATT_INNER) // ATT_HEAD_DIM
    q_heads = jnp.where(head_of_col[None, None, None, :] == jnp.arange(ATT_HEADS)[None, :, None, None],
                        q_b[:, None, :, :], jnp.zeros((), q_b.dtype))
    q_rows = q_heads.reshape(b, rows, ATT_INNER)
    rb_rows = jnp.repeat(rel_bias.T, t_new, axis=0)
    k_new, v_new = new_cols(k_b), new_cols(v_b)
    per_b = lambda s: pl.BlockSpec((1,) + s, lambda i, ph, j, pt: (i,) + (0,) * len(s))
    page_blk = (1, ATT_HEADS, ATT_HEAD_DIM, PAGE_SIZE)
    k_page = lambda r: pl.BlockSpec(
        page_blk, lambda i, ph, j, pt: (pt[i, jnp.where(ph == 0, j, n_steps - 1) * pg + r], 0, 0, 0))
    v_page = lambda r: pl.BlockSpec(
        page_blk, lambda i, ph, j, pt: (pt[i, jnp.where(ph == 1, j, 0) * pg + r], 0, 0, 0))
    return pl.pallas_call(
        functools.partial(_dsa_sample_attn_kernel, t_new, past, pg),
        grid_spec=pltpu.PrefetchScalarGridSpec(
            num_scalar_prefetch=1, grid=(b, 2, n_steps),
            in_specs=[per_b((rows, ATT_INNER)),
                      pl.BlockSpec((1, pg, t_new, LANES),
                                   lambda i, ph, j, pt: (i, jnp.where(ph == 0, j, n_steps - 1), 0, 0)),
                      per_b((t_new, LANES)), per_b((t_new, LANES)),
                      per_b((ATT_INNER, LANES)), per_b((ATT_INNER, LANES)),
                      pl.BlockSpec((rows, N_BUCKETS), lambda i, ph, j, pt: (0, 0))]
                     + [k_page(r) for r in range(pg)] + [v_page(r) for r in range(pg)],
            out_specs=per_b((t_new, ATT_INNER)),
            scratch_shapes=[pltpu.VMEM((n_pages + 1, rows, LANES), F32), pltpu.VMEM((rows, LANES), F32),
                            pltpu.VMEM((rows, LANES), F32), pltpu.VMEM((rows, ATT_INNER), F32)]),
        out_shape=jax.ShapeDtypeStruct((b, t_new, ATT_INNER), F32),
        compiler_params=_cparams(("parallel", "arbitrary", "arbitrary")),
        name="dsa_sample_attend",
    )(page_table, q_rows, keys, keys_new, tau, k_new, v_new, rb_rows, *([k_t] * pg), *([v_t] * pg))


def _matmul_kernel(x_ref, w_ref, o_ref):
    o_ref[...] = _mm(x_ref[...], w_ref[...])


def _matmul(x2d, w):
    n, kdim = x2d.shape
    tm = min(256, n)
    return pl.pallas_call(
        _matmul_kernel,
        grid=(n // tm,),
        in_specs=[pl.BlockSpec((tm, kdim), lambda i: (i, 0)), pl.BlockSpec(w.shape, lambda i: (0, 0))],
        out_specs=pl.BlockSpec((tm, w.shape[1]), lambda i: (i, 0)),
        out_shape=jax.ShapeDtypeStruct((n, w.shape[1]), F32),
        compiler_params=_cparams(("parallel",)),
        name="mem_kv_proj",
    )(x2d, w)


def _out_proj_kernel(x_ref, ssd_ref, att_ref, wout_ref, g_ref, b_ref, wcq_ref, h1_ref, qm_ref):
    m = _mm(ssd_ref[...], wout_ref[:SSD_INNER, :]) + _mm(att_ref[...], wout_ref[SSD_INNER:, :])
    h1 = _layer_norm(ALPHA * x_ref[...] + m, g_ref[...], b_ref[...])
    h1_ref[...] = h1
    qm_ref[...] = _mm(h1, wcq_ref[...])


def _out_proj(x2d, ssd2d, att2d, w_out, g, b, w_cq):
    n = x2d.shape[0]
    tm = min(256, n)
    row = lambda w: pl.BlockSpec((tm, w), lambda i: (i, 0))
    full = lambda a: pl.BlockSpec(a.shape, lambda i: (0, 0))
    return pl.pallas_call(
        _out_proj_kernel,
        grid=(n // tm,),
        in_specs=[row(D_MODEL), row(SSD_INNER), row(ATT_INNER), full(w_out), full(g), full(b), full(w_cq)],
        out_specs=(row(D_MODEL), row(MEM_INNER)),
        out_shape=(jax.ShapeDtypeStruct((n, D_MODEL), F32), jax.ShapeDtypeStruct((n, MEM_INNER), F32)),
        compiler_params=_cparams(("parallel",)),
        name="out_proj_ln1",
    )(x2d, ssd2d, att2d, w_out, g, b, w_cq)


def _mem_attend_kernel(q_ref, mk_ref, mv_ref, o_ref):
    q = q_ref[0]
    for h in range(MEM_HEADS):
        sl = slice(h * MEM_HEAD_DIM, (h + 1) * MEM_HEAD_DIM)
        logits = _mm_nt(q[:, sl], mk_ref[0, :, sl]) * (MEM_HEAD_DIM ** -0.5)
        logits = logits - jnp.max(logits, axis=-1, keepdims=True)
        p = jnp.exp(logits)
        p = p / jnp.sum(p, axis=-1, keepdims=True)
        o_ref[0, :, sl] = _mm(p, mv_ref[0, :, sl])


def _mem_attend(qm, mk, mv):
    b, length, _ = qm.shape
    tq = min(512, length)
    m_tok = mk.shape[1]
    return pl.pallas_call(
        _mem_attend_kernel,
        grid=(b, length // tq),
        in_specs=[pl.BlockSpec((1, tq, MEM_INNER), lambda i, j: (i, j, 0)),
                  pl.BlockSpec((1, m_tok, MEM_INNER), lambda i, j: (i, 0, 0)),
                  pl.BlockSpec((1, m_tok, MEM_INNER), lambda i, j: (i, 0, 0))],
        out_specs=pl.BlockSpec((1, tq, MEM_INNER), lambda i, j: (i, j, 0)),
        out_shape=jax.ShapeDtypeStruct((b, length, MEM_INNER), F32),
        compiler_params=_cparams(("parallel", "parallel")),
        name="mem_attend",
    )(qm, mk, mv)


def _top_rows(s, n_top, with_index):
    rows = s.shape[0]
    ridx = lax.broadcasted_iota(I32, s.shape, 0).astype(F32)
    vals, idxs = [], []
    for _ in range(n_top):
        m = jnp.max(s, axis=0, keepdims=True)
        first = jnp.min(jnp.where(s == m, ridx, float(rows)), axis=0, keepdims=True)
        s = jnp.where(ridx == first, -jnp.inf, s)
        vals.append(m)
        idxs.append(first)
    return jnp.concatenate(vals, axis=0), (jnp.concatenate(idxs, axis=0) if with_index else None)


_SUBLANES = 8
ROUTE_HEADS_PER_TRIP = 8


def _staircase_blocks(n):
    by_b, single_b = [], []
    for a in range(n):
        count = n // (a + 1)
        if count > 1:
            by_b += [(a, b) for b in range(0, count, _SUBLANES)]
        else:
            single_b.append(a)
    assert len(single_b) % _SUBLANES == 0 and single_b == list(range(n - len(single_b), n))
    return tuple(by_b), tuple(single_b[::_SUBLANES])


_STAIRCASE = _staircase_blocks(PEER_TOPK)


def _peer_route_kernel(h1_ref, o_ref, wco_ref, g_ref, b_ref, wq_ref, sk_ref, h2_ref, e_ref, gate_ref, h2b_sc):
    h2 = _layer_norm(ALPHA * h1_ref[...] + _mm(o_ref[...], wco_ref[...]), g_ref[...], b_ref[...])
    h2_ref[...] = h2
    h2b_sc[...] = h2.astype(MXU_DTYPE)

    def route_head(h, carry):
        tops = []
        for p in range(2):
            c = h * 2 + p
            qs = jnp.dot(h2b_sc[...], wq_ref[c], preferred_element_type=F32)
            s_t = _mm_nt(sk_ref[c], qs)
            tops.append(_top_rows(s_t, PEER_TOPK, True))
        (v0, i0), (v1, i1) = tops
        by_b, by_a = _STAIRCASE
        sub = _SUBLANES
        cand = jnp.concatenate([v0[a:a + 1] + v1[b:b + sub] for a, b in by_b]
                               + [v0[a:a + sub] + v1[0:1] for a in by_a], axis=0)
        cidx = jnp.concatenate([i0[a:a + 1] * PEER_KEYS + i1[b:b + sub] for a, b in by_b]
                               + [i0[a:a + sub] * PEER_KEYS + i1[0:1] for a in by_a], axis=0)
        pos = lax.broadcasted_iota(I32, cand.shape, 0).astype(F32)
        gs, es = [], []
        for _ in range(PEER_TOPK):
            m = jnp.max(cand, axis=0, keepdims=True)
            first = jnp.min(jnp.where(cand == m, pos, float(cand.shape[0])), axis=0, keepdims=True)
            hit = pos == first
            es.append(jnp.max(jnp.where(hit, cidx, -1.0), axis=0, keepdims=True))
            cand = jnp.where(hit, -jnp.inf, cand)
            gs.append(m)
        g_s = jnp.concatenate(gs, axis=0)
        ex = jnp.exp(g_s - g_s[0:1])
        rows = pl.ds(pl.multiple_of(h * PEER_TOPK, PEER_TOPK), PEER_TOPK)
        gate_ref[0, rows, :] = ex / jnp.sum(ex, axis=0, keepdims=True)
        e_ref[0, rows, :] = jnp.concatenate(es, axis=0).astype(I32) * SLAB_ROWS
        return carry

    def route_group(g, carry):
        for k in range(ROUTE_HEADS_PER_TRIP):
            route_head(g * ROUTE_HEADS_PER_TRIP + k, carry)
        return carry

    lax.fori_loop(0, PEER_HEADS // ROUTE_HEADS_PER_TRIP, route_group, 0)


def _peer_route(h1, o, w_co, g, b, wq, subkeys):
    n = h1.shape[0]
    tm = TOK_BLOCK
    nb = n // tm
    row = lambda w: pl.BlockSpec((tm, w), lambda i: (i, 0))
    full = lambda a: pl.BlockSpec(a.shape, lambda i: (0,) * a.ndim)
    pairs = pl.BlockSpec((1, PEER_PAIRS, tm), lambda i: (i, 0, 0))
    return pl.pallas_call(
        _peer_route_kernel,
        grid=(nb,),
        in_specs=[row(D_MODEL), row(MEM_INNER), full(w_co), full(g), full(b), full(wq), full(subkeys)],
        out_specs=(row(D_MODEL), pairs, pairs),
        out_shape=(jax.ShapeDtypeStruct((n, D_MODEL), F32),
                   jax.ShapeDtypeStruct((nb, PEER_PAIRS, tm), I32),
                   jax.ShapeDtypeStruct((nb, PEER_PAIRS, tm), F32)),
        scratch_shapes=[pltpu.VMEM((tm, D_MODEL), MXU_DTYPE)],
        compiler_params=_cparams(("parallel",)),
        name="ln2_peer_route",
    )(h1, o, w_co, g, b, wq, subkeys)


def _pack_table(t):
    tb = lax.bitcast_convert_type(t.astype(BF16), jnp.uint16).astype(jnp.uint32)
    half = t.shape[1] // 2
    words = tb[:, :half] | (tb[:, half:] << 16)
    return lax.bitcast_convert_type(words, I32).reshape(t.shape[0] * SLAB_ROWS, LANES)


def _unpack_words(w):
    lo = lax.bitcast_convert_type(w << 16, F32)
    hi = lax.bitcast_convert_type(w & jnp.int32(-65536), F32)
    return lo, hi


def _gelu_tanh(x):
    return 0.5 * x * (1.0 + jnp.tanh(math.sqrt(2.0 / math.pi) * (x + 0.044715 * (x * x * x))))


def _gather_token_rows(idx_smem, slot, tab_ref, tile_ref, t):
    for p in range(PEER_PAIRS):
        row = pl.multiple_of(idx_smem[slot, t, p], SLAB_ROWS)
        tile_ref[pl.ds(p, SLAB_ROWS, stride=TILE_STRIDE), :] = tab_ref[pl.ds(row, SLAB_ROWS), :]


def _load_indices(e_hbm, idx_smem, sem):
    i = pl.program_id(0)
    slot = lax.rem(i, 2)
    copy = lambda step, s: pltpu.make_async_copy(e_hbm.at[step], idx_smem.at[s], sem.at[s])

    @pl.when(i == 0)
    def _():
        copy(0, 0).start()

    @pl.when(i + 1 < pl.num_programs(0))
    def _():
        copy(i + 1, 1 - slot).start()

    copy(i, slot).wait()
    return slot


def _trip_rows(g):
    return pl.ds(pl.multiple_of(g * GATHER_TOKENS, GATHER_TOKENS), GATHER_TOKENS)


def _peer_in_kernel(e_hbm, tab_ref, h2_ref, gate_ref, w_ref, idx_smem, sem, *tiles):
    slot = _load_indices(e_hbm, idx_smem, sem)
    tm = h2_ref.shape[0]
    lane = lax.broadcasted_iota(I32, (PEER_PAIRS, tm), 1)

    def tok_group(g, act_t):
        x_rows = h2_ref[_trip_rows(g), :]
        cols = []
        _gather_token_rows(idx_smem, slot, tab_ref, tiles[0], g * GATHER_TOKENS)
        for u, tile_ref in enumerate(tiles):
            if u + 1 < GATHER_TOKENS:
                _gather_token_rows(idx_smem, slot, tab_ref, tiles[u + 1], g * GATHER_TOKENS + u + 1)
            x_chunk = lambda c: x_rows[u:u + 1, c * LANES:(c + 1) * LANES]
            acc = jnp.zeros((PEER_PAIRS, LANES), F32)
            for j in range(SLAB_ROWS):
                lo, hi = _unpack_words(tile_ref[j * TILE_STRIDE:j * TILE_STRIDE + PEER_PAIRS, :])
                acc = acc + lo * x_chunk(j) + hi * x_chunk(SLAB_ROWS + j)
            cols.append(jnp.sum(acc, axis=1, keepdims=True))
        for u, col in enumerate(cols):
            act_t = jnp.where(lane == g * GATHER_TOKENS + u, col, act_t)
        return act_t

    act_t = lax.fori_loop(0, tm // GATHER_TOKENS, tok_group, jnp.zeros((PEER_PAIRS, tm), F32))
    w_ref[0] = gate_ref[0] * _gelu_tanh(act_t)


def _peer_out_kernel(e_hbm, tab_ref, h2_ref, w_ref, g_ref, b_ref, y_ref, idx_smem, sem, out_sc, *tiles):
    slot = _load_indices(e_hbm, idx_smem, sem)
    tm = h2_ref.shape[0]
    lane = lax.broadcasted_iota(I32, (PEER_PAIRS, tm), 1)

    def tok_group(g, carry):
        rows = []
        _gather_token_rows(idx_smem, slot, tab_ref, tiles[0], g * GATHER_TOKENS)
        for u, tile_ref in enumerate(tiles):
            t = g * GATHER_TOKENS + u
            if u + 1 < GATHER_TOKENS:
                _gather_token_rows(idx_smem, slot, tab_ref, tiles[u + 1], t + 1)
            w_col = jnp.sum(jnp.where(lane == t, w_ref[0], 0.0), axis=1, keepdims=True)
            los, his = [], []
            for j in range(SLAB_ROWS):
                lo, hi = _unpack_words(tile_ref[j * TILE_STRIDE:j * TILE_STRIDE + PEER_PAIRS, :])
                los.append(jnp.sum(lo * w_col, axis=0, keepdims=True))
                his.append(jnp.sum(hi * w_col, axis=0, keepdims=True))
            rows.append(jnp.concatenate(los + his, axis=1))
        out_sc[_trip_rows(g), :] = jnp.concatenate(rows, axis=0)
        return carry

    lax.fori_loop(0, tm // GATHER_TOKENS, tok_group, 0)
    y_ref[...] = _layer_norm(ALPHA * h2_ref[...] + out_sc[...], g_ref[...], b_ref[...])


def _peer_scratch(tm):
    assert GATHER_TOKENS == _SUBLANES and tm % GATHER_TOKENS == 0
    return ([pltpu.SMEM((2, tm, PEER_PAIRS), I32), pltpu.SemaphoreType.DMA((2,))]
            + [pltpu.VMEM((SLAB_ROWS * TILE_STRIDE, LANES), I32) for _ in range(GATHER_TOKENS)])


def _peer_in(e_rows, u_tab, h2, gate_t):
    nb, _, tm = gate_t.shape
    pairs = pl.BlockSpec((1, PEER_PAIRS, tm), lambda i: (i, 0, 0))
    return pl.pallas_call(
        _peer_in_kernel,
        grid=(nb,),
        in_specs=[pl.BlockSpec(memory_space=pl.ANY), pl.BlockSpec(memory_space=pltpu.VMEM),
                  pl.BlockSpec((tm, D_MODEL), lambda i: (i, 0)), pairs],
        out_specs=pairs,
        out_shape=jax.ShapeDtypeStruct((nb, PEER_PAIRS, tm), F32),
        scratch_shapes=_peer_scratch(tm),
        compiler_params=_cparams(("arbitrary",)),
        name="peer_expert_in",
    )(e_rows, u_tab, h2, gate_t)


def _peer_out(e_rows, v_tab, h2, w_t, g, b):
    nb, _, tm = w_t.shape
    pairs = pl.BlockSpec((1, PEER_PAIRS, tm), lambda i: (i, 0, 0))
    row = pl.BlockSpec((tm, D_MODEL), lambda i: (i, 0))
    vec = pl.BlockSpec((1, D_MODEL), lambda i: (0, 0))
    scratch = _peer_scratch(tm)
    return pl.pallas_call(
        _peer_out_kernel,
        grid=(nb,),
        in_specs=[pl.BlockSpec(memory_space=pl.ANY), pl.BlockSpec(memory_space=pltpu.VMEM), row, pairs, vec, vec],
        out_specs=row,
        out_shape=jax.ShapeDtypeStruct((nb * tm, D_MODEL), F32),
        scratch_shapes=scratch[:2] + [pltpu.VMEM((tm, D_MODEL), F32)] + scratch[2:],
        compiler_params=_cparams(("arbitrary",)),
        name="peer_expert_out_ln3",
    )(e_rows, v_tab, h2, w_t, g, b)


def _lane_row(vals, offset):
    return jnp.zeros((1, LANES), F32).at[0, offset:offset + vals.shape[0]].set(vals)


def _finish(x2d, ssd2d, att2d, mem_attend_fn, p):
    h1, qm = _out_proj(x2d, ssd2d, att2d, p["w_out"], p["ln1_g"], p["ln1_b"], p["w_cq"])
    o = mem_attend_fn(qm)
    h2, e_t, gate_t = _peer_route(h1, o, p["w_co"], p["ln2_g"], p["ln2_b"], p["peer_wq"], p["subkeys"])
    e_rows = jnp.swapaxes(e_t, 1, 2)
    w_t = _peer_in(e_rows, p["u_tab"], h2, gate_t)
    return _peer_out(e_rows, p["v_tab"], h2, w_t, p["ln3_g"], p["ln3_b"])


def kernel(x_prompt, mem_prompt, x_sample, cache_k, cache_v, cache_kidx, cache_mem_k, cache_mem_v, state_ssm, state_conv, page_table, w_in, conv_w, conv_b, dt_bias, a_log, d_skip, ssd_norm_w, rel_bias, w_out, ln1_g, ln1_b, w_cq, w_ck, w_cv, w_co, ln2_g, ln2_b, peer_wq, peer_subkeys, peer_u, peer_v, ln3_g, ln3_b):
    bp, seq, _ = x_prompt.shape
    bs, t_s, _ = x_sample.shape
    m_tok = mem_prompt.shape[1]

    c0 = SSD_INNER + CONV_DIM
    wa = w_in[:, :c0].astype(MXU_DTYPE)
    c1 = c0 + SSD_HEADS
    wb = w_in[:, c1:c1 + 4 * ATT_INNER].astype(MXU_DTYPE)
    c2 = c1 + 4 * ATT_INNER
    w_kidx = w_in[:, c2:c2 + IDX_DIM]
    wc = jnp.concatenate([w_kidx, w_kidx], axis=1).astype(MXU_DTYPE)
    wd = jnp.concatenate([w_kidx, w_in[:, c0:c1], w_in[:, c2 + IDX_DIM:],
                          jnp.zeros((D_MODEL, LANES - IDX_DIM - SSD_HEADS - IDX_HEADS), F32)], axis=1).astype(MXU_DTYPE)
    row = lambda v: v.reshape(1, -1)
    p = dict(
        w_out=w_out.astype(MXU_DTYPE), ln1_g=row(ln1_g), ln1_b=row(ln1_b), w_cq=w_cq.astype(MXU_DTYPE),
        w_co=w_co.astype(MXU_DTYPE), ln2_g=row(ln2_g), ln2_b=row(ln2_b),
        peer_wq=peer_wq.astype(MXU_DTYPE).reshape(D_MODEL, PEER_HEADS * 2, PEER_HALF).swapaxes(0, 1),
        subkeys=peer_subkeys.reshape(PEER_HEADS * 2, PEER_KEYS, PEER_HALF).astype(MXU_DTYPE),
        u_tab=_pack_table(peer_u), v_tab=_pack_table(peer_v), ln3_g=row(ln3_g), ln3_b=row(ln3_b))
    ssd_params = (conv_w, row(conv_b), _lane_row(a_log, DT_LANE), _lane_row(dt_bias, DT_LANE),
                  row(jnp.repeat(d_skip, SSD_HEAD_DIM)), row(ssd_norm_w))
    state_t = lambda s: jnp.swapaxes(s, -1, -2)

    n_p = bp * seq
    z, xbc, q_b, k_p, v_p, k_b, v_b, qi_b, kk_b, small = _project_in(x_prompt.reshape(n_p, D_MODEL), wa, wb, wc, wd)
    r3 = lambda a, b_: a.reshape(b_, -1, a.shape[-1])
    ssd_out, ssm_t, conv_p = _ssd_mixer(
        r3(z, bp), r3(xbc, bp), r3(small, bp), jnp.zeros((bp, SSD_CONV - 1, CONV_DIM), F32),
        jnp.zeros((bp, SSD_HEADS, SSD_STATE, SSD_HEAD_DIM), F32), SSD_CHUNK, *ssd_params)
    att_out = _dsa_prompt(r3(q_b, bp), r3(qi_b, bp), r3(small, bp), r3(kk_b, bp), r3(k_b, bp), r3(v_b, bp), rel_bias)
    kidx_p = r3(small, bp)[:, :, :IDX_DIM]
    mem_kv = _matmul(mem_prompt.reshape(bp * m_tok, D_MODEL),
                     jnp.concatenate([w_ck, w_cv], axis=1).astype(MXU_DTYPE))
    memk_p = mem_kv[:, :MEM_INNER].reshape(bp, m_tok, MEM_INNER)
    memv_p = mem_kv[:, MEM_INNER:].reshape(bp, m_tok, MEM_INNER)
    y_prompt = _finish(x_prompt.reshape(n_p, D_MODEL), ssd_out.reshape(n_p, SSD_INNER),
                       att_out.reshape(n_p, ATT_INNER),
                       lambda qm: _mem_attend(qm.reshape(bp, seq, MEM_INNER), memk_p, memv_p).reshape(n_p, MEM_INNER),
                       p)

    n_s = bs * t_s
    n_pad = -(-n_s // TOK_BLOCK) * TOK_BLOCK
    xs2d = jnp.pad(x_sample.reshape(n_s, D_MODEL), ((0, n_pad - n_s), (0, 0)))
    outs = _project_in(xs2d, wa, wb, wc, wd)
    z, xbc, q_b, k_s, v_s, k_b, v_b, qi_b, kk_b, small = [r3(a[:n_s], bs) for a in outs]
    pad_chunk = lambda a: jnp.pad(a, ((0, 0), (0, SSD_CHUNK - t_s), (0, 0)))
    ssd_s, ssm_s_t, conv_s = _ssd_mixer(pad_chunk(z), pad_chunk(xbc), pad_chunk(small), state_conv,
                                        state_t(state_ssm), t_s, *ssd_params)
    att_s = _dsa_sample(q_b, qi_b, small, k_b, v_b, kk_b, cache_k, cache_v, cache_kidx, page_table, rel_bias)
    pad_rows = lambda a: jnp.pad(a.reshape(n_s, -1), ((0, n_pad - n_s), (0, 0)))
    cmk = cache_mem_k.reshape(bs, m_tok, MEM_INNER)
    cmv = cache_mem_v.reshape(bs, m_tok, MEM_INNER)
    y_s = _finish(xs2d, pad_rows(ssd_s[:, :t_s]), pad_rows(att_s),
                  lambda qm: pad_rows(_mem_attend(qm[:n_s].reshape(bs, t_s, MEM_INNER), cmk, cmv)), p)
    y_sample = y_s[:n_s].reshape(bs, t_s, D_MODEL)

    heads = lambda a, b_: a.reshape(b_, -1, ATT_HEADS, ATT_HEAD_DIM)
    mem_heads = lambda a: a.reshape(bp, m_tok, MEM_HEADS, MEM_HEAD_DIM)
    return (y_prompt.reshape(bp, seq, D_MODEL), y_sample,
            heads(k_p, bp), heads(v_p, bp), kidx_p, mem_heads(memk_p), mem_heads(memv_p),
            state_t(ssm_t), conv_p,
            heads(k_s, bs), heads(v_s, bs), small[:, :, :IDX_DIM], state_t(ssm_s_t), conv_s)
```

```python
import functools
import math

import jax
import jax.numpy as jnp
import numpy as np
from jax import lax
from jax.experimental import pallas as pl
from jax.experimental.pallas import tpu as pltpu

F32 = jnp.float32
BF16 = jnp.bfloat16
I32 = jnp.int32
MXU_DTYPE = BF16

D_MODEL = 1024
DEPTH = 1
ALPHA = (2.0 * DEPTH) ** 0.25
LN_EPS = 1e-5

SSD_HEADS = 8
SSD_HEAD_DIM = 64
SSD_INNER = 512
SSD_GROUPS = 2
SSD_STATE = 128
SSD_CONV = 4
SSD_CHUNK = 128
CONV_DIM = 1024

ATT_HEADS = 8
ATT_HEAD_DIM = 64
ATT_INNER = 512
IDX_HEADS = 8
IDX_DIM = 64
TOPK_MAX = 256
PAGE_SIZE = 128
N_BUCKETS = 32
MAX_DISTANCE = 128

MEM_HEADS = 4
MEM_HEAD_DIM = 128
MEM_INNER = 512

PEER_HEADS = 8
PEER_KEYS = 128
PEER_HALF = 128
PEER_TOPK = 16
PEER_PAIRS = PEER_HEADS * PEER_TOPK

LANES = 128
CHUNK = 2 * LANES
TOK_BLOCK = 128
DT_LANE = 64
WIDX_LANE = 72
NEG_BIG = -1e30
INT_MIN = -(2 ** 31)
VMEM_LIMIT = 56 * 1024 * 1024
SLAB_ROWS = 4
FEAT_CHUNKS = D_MODEL // LANES
TILE_STRIDE = 136
GATHER_TOKENS = 8


def _cparams(sem, vmem=VMEM_LIMIT):
    return pltpu.CompilerParams(dimension_semantics=sem, vmem_limit_bytes=vmem)


def _mm(a, b):
    return jnp.dot(a.astype(MXU_DTYPE), b.astype(MXU_DTYPE), preferred_element_type=F32)


def _mm_nt(a, b):
    return lax.dot_general(a.astype(MXU_DTYPE), b.astype(MXU_DTYPE), (((1,), (1,)), ((), ())),
                           preferred_element_type=F32)


def _mm_exact(a, b):
    return jnp.dot(a, b, preferred_element_type=F32, precision=lax.Precision.HIGHEST)


def _sigmoid(x):
    return 1.0 / (1.0 + jnp.exp(-x))


def _layer_norm(x, g, b):
    mu = jnp.mean(x, axis=-1, keepdims=True)
    xc = x - mu
    var = jnp.mean(xc * xc, axis=-1, keepdims=True)
    return xc * lax.rsqrt(var + LN_EPS) * g + b


def _bucket_thresholds():
    max_exact = N_BUCKETS // 2
    out = []
    for n in range(max_exact, MAX_DISTANCE + 1):
        v = max_exact + int(math.log(n / max_exact) / math.log(MAX_DISTANCE / max_exact) * (N_BUCKETS - max_exact))
        out.append(min(v, N_BUCKETS - 1))
    thr = []
    for bkt in range(max_exact + 1, N_BUCKETS):
        thr.append(max_exact + next(i for i, v in enumerate(out) if v >= bkt))
    return tuple(thr)


_BUCKET_THR = _bucket_thresholds()


def _rel_bucket(dist):
    n = jnp.maximum(dist, 0)
    max_exact = N_BUCKETS // 2
    large = jnp.full(n.shape, max_exact, I32)
    for t in _BUCKET_THR:
        large = large + (n >= t).astype(I32)
    return jnp.where(n < max_exact, n, large)


def _monotone_key(x):
    b = lax.bitcast_convert_type(x, I32)
    return b ^ ((b >> 31) & 0x7FFFFFFF)


def _proj_in_kernel(x_ref, wa_ref, wb_ref, wc_ref, wd_ref,
                    z_ref, xbc_ref, q_ref, k_ref, v_ref, kb_ref, vb_ref, qi_ref, kk_ref, sm_ref):
    xb = x_ref[...].astype(MXU_DTYPE)
    z_ref[...] = jnp.dot(xb, wa_ref[:, :SSD_INNER], preferred_element_type=F32)
    xbc_ref[...] = jnp.dot(xb, wa_ref[:, SSD_INNER:], preferred_element_type=F32)
    part = lambda n: wb_ref[:, n * ATT_INNER:(n + 1) * ATT_INNER]
    q = jnp.dot(xb, part(0), preferred_element_type=F32)
    q_ref[...] = (q * (ATT_HEAD_DIM ** -0.5)).astype(q_ref.dtype)
    k = jnp.dot(xb, part(1), preferred_element_type=F32)
    k_ref[...] = k
    kb_ref[...] = k.astype(kb_ref.dtype)
    v = jnp.dot(xb, part(2), preferred_element_type=F32)
    v_ref[...] = v
    vb_ref[...] = v.astype(vb_ref.dtype)
    qi = jnp.dot(xb, part(3), preferred_element_type=F32)
    qi_ref[...] = (qi * (IDX_DIM ** -0.5)).astype(qi_ref.dtype)
    kk_ref[...] = jnp.dot(xb, wc_ref[...], preferred_element_type=F32).astype(kk_ref.dtype)
    sm = jnp.dot(xb, wd_ref[...], preferred_element_type=F32)
    lane = lax.broadcasted_iota(I32, sm.shape, 1)
    is_w = (lane >= WIDX_LANE) & (lane < WIDX_LANE + IDX_HEADS)
    sm_ref[...] = jnp.where(is_w, sm * (IDX_HEADS ** -0.5), sm)


def _project_in(x2d, wa, wb, wc, wd):
    n = x2d.shape[0]
    tm = min(256, n)
    row = lambda w: pl.BlockSpec((tm, w), lambda i: (i, 0))
    full = lambda a: pl.BlockSpec(a.shape, lambda i: (0, 0))
    out_shapes = (
        jax.ShapeDtypeStruct((n, SSD_INNER), F32),
        jax.ShapeDtypeStruct((n, CONV_DIM), F32),
        jax.ShapeDtypeStruct((n, ATT_INNER), MXU_DTYPE),
        jax.ShapeDtypeStruct((n, ATT_INNER), F32),
        jax.ShapeDtypeStruct((n, ATT_INNER), F32),
        jax.ShapeDtypeStruct((n, ATT_INNER), MXU_DTYPE),
        jax.ShapeDtypeStruct((n, ATT_INNER), MXU_DTYPE),
        jax.ShapeDtypeStruct((n, ATT_INNER), MXU_DTYPE),
        jax.ShapeDtypeStruct((n, LANES), MXU_DTYPE),
        jax.ShapeDtypeStruct((n, LANES), F32),
    )
    return pl.pallas_call(
        _proj_in_kernel,
        grid=(n // tm,),
        in_specs=[row(D_MODEL), full(wa), full(wb), full(wc), full(wd)],
        out_specs=tuple(row(s.shape[1]) for s in out_shapes),
        out_shape=out_shapes,
        compiler_params=_cparams(("parallel",)),
        name="project_in",
    )(x2d, wa, wb, wc, wd)


def _ssd_kernel(n_valid, z_ref, xbc_ref, sm_ref, conv0_ref, h0_ref, convw_ref, convb_ref, alog_ref, dtb_ref,
                dskip_ref, normw_ref, y_ref, hout_ref, tail_ref, cbuf, hst, ysc):
    q_len = z_ref.shape[1]
    c = pl.program_id(1)
    pad = 8

    @pl.when(c == 0)
    def _():
        cbuf[pad - 3:pad, :] = conv0_ref[0]
        hst[...] = h0_ref[0]

    xbc = xbc_ref[0]
    cbuf[pad:pad + q_len, :] = xbc
    w = convw_ref[...]
    conv = cbuf[pad - 3:pad - 3 + q_len, :] * w[0:1]
    conv = conv + cbuf[pad - 2:pad - 2 + q_len, :] * w[1:2]
    conv = conv + cbuf[pad - 1:pad - 1 + q_len, :] * w[2:3]
    conv = conv + xbc * w[3:4]
    conv = conv + convb_ref[...]
    xc = conv * _sigmoid(conv)
    tail = cbuf[pad + n_valid - 3:pad + n_valid, :]
    cbuf[pad - 3:pad, :] = tail

    row = lax.broadcasted_iota(I32, (q_len, LANES), 0)
    lane = lax.broadcasted_iota(I32, (q_len, LANES), 1)
    is_dt = (lane >= DT_LANE) & (lane < DT_LANE + SSD_HEADS) & (row < n_valid)
    raw = sm_ref[0] + dtb_ref[...]
    dt_full = jnp.maximum(raw, 0.0) + jnp.log1p(jnp.exp(-jnp.abs(raw)))
    dtm = jnp.where(is_dt, dt_full, 0.0)
    a_full = dtm * (-jnp.exp(alog_ref[...]))
    ri = lax.broadcasted_iota(I32, (q_len, q_len), 0)
    ci = lax.broadcasted_iota(I32, (q_len, q_len), 1)
    causal = ci <= ri
    acum = _mm_exact(causal.astype(F32), a_full)
    acum_t = acum.T

    xs = xc[:, :SSD_INNER]
    for g in range(SSD_GROUPS):
        bm = xc[:, SSD_INNER + g * SSD_STATE:SSD_INNER + (g + 1) * SSD_STATE]
        cm = xc[:, SSD_INNER + (SSD_GROUPS + g) * SSD_STATE:SSD_INNER + (SSD_GROUPS + g + 1) * SSD_STATE]
        scores = _mm_nt(cm, bm)
        bm_t = bm.T
        for hh in range(SSD_HEADS // SSD_GROUPS):
            h = g * (SSD_HEADS // SSD_GROUPS) + hh
            col = acum[:, DT_LANE + h:DT_LANE + h + 1]
            rowv = acum_t[DT_LANE + h:DT_LANE + h + 1, :]
            a_last = acum[q_len - 1:q_len, DT_LANE + h:DT_LANE + h + 1]
            decay = jnp.exp(jnp.where(causal, col - rowv, -jnp.inf))
            x_h = xs[:, h * SSD_HEAD_DIM:(h + 1) * SSD_HEAD_DIM]
            xd = x_h * dtm[:, DT_LANE + h:DT_LANE + h + 1]
            h_t = hst[h]
            y = _mm(scores * decay, xd) + _mm(cm, h_t) * jnp.exp(col)
            y = y + x_h * dskip_ref[:, h * SSD_HEAD_DIM:(h + 1) * SSD_HEAD_DIM]
            ysc[:, h * SSD_HEAD_DIM:(h + 1) * SSD_HEAD_DIM] = y
            hst[h] = h_t * jnp.exp(a_last) + _mm(bm_t, xd * jnp.exp(a_last - col))

    zz = z_ref[0]
    hg = ysc[...] * (zz * _sigmoid(zz))
    hg = hg * lax.rsqrt(jnp.mean(hg * hg, axis=-1, keepdims=True) + LN_EPS)
    y_ref[0] = hg * normw_ref[...]

    @pl.when(c == pl.num_programs(1) - 1)
    def _():
        hout_ref[0] = hst[...]
        tail_ref[0] = cbuf[pad - 3:pad, :]


def _ssd_mixer(z, xbc, small, conv0, h0_t, n_valid, conv_w, conv_b, alog_lane, dtb_lane, dskip_row, norm_w):
    b, length, _ = z.shape
    q_len = SSD_CHUNK
    nc = length // q_len
    blk = lambda w: pl.BlockSpec((1, q_len, w), lambda i, c: (i, c, 0))
    per_b3 = lambda s: pl.BlockSpec((1,) + s, lambda i, c: (i,) + (0,) * len(s))
    const2 = lambda a: pl.BlockSpec(a.shape, lambda i, c: (0, 0))
    st_shape = (SSD_HEADS, SSD_STATE, SSD_HEAD_DIM)
    return pl.pallas_call(
        functools.partial(_ssd_kernel, n_valid),
        grid=(b, nc),
        in_specs=[blk(SSD_INNER), blk(CONV_DIM), blk(LANES), per_b3((SSD_CONV - 1, CONV_DIM)), per_b3(st_shape),
                  const2(conv_w), const2(conv_b), const2(alog_lane), const2(dtb_lane), const2(dskip_row),
                  const2(norm_w)],
        out_specs=(blk(SSD_INNER), per_b3(st_shape), per_b3((SSD_CONV - 1, CONV_DIM))),
        out_shape=(jax.ShapeDtypeStruct((b, length, SSD_INNER), F32),
                   jax.ShapeDtypeStruct((b,) + st_shape, F32),
                   jax.ShapeDtypeStruct((b, SSD_CONV - 1, CONV_DIM), F32)),
        scratch_shapes=[pltpu.VMEM((q_len + 8, CONV_DIM), F32), pltpu.VMEM(st_shape, F32),
                        pltpu.VMEM((q_len, SSD_INNER), F32)],
        compiler_params=_cparams(("parallel", "arbitrary")),
        name="ssd_mixer",
    )(z, xbc, small, conv0, h0_t, conv_w, conv_b, alog_lane, dtb_lane, dskip_row, norm_w)


def _kth_largest_key(count_ge, k, rows):
    def body(i, tau):
        cand = tau + jnp.left_shift(jnp.int32(1), 31 - i)
        return jnp.where(count_ge(cand) >= k, cand, tau)
    return lax.fori_loop(0, 32, body, jnp.full((rows, 1), INT_MIN, I32))


def _bias_tiles(rb_ref, bias_sc, n_tiles):
    a = lax.broadcasted_iota(I32, (LANES, LANES), 0)
    bcol = lax.broadcasted_iota(I32, (LANES, LANES), 1)
    for d in range(n_tiles):
        bucket = _rel_bucket(d * LANES + a - bcol)
        for h in range(ATT_HEADS):
            far = rb_ref[N_BUCKETS - 1, h]
            tile = jnp.zeros((LANES, LANES), F32)
            for kb in range(N_BUCKETS):
                tile = jnp.where(bucket == kb, rb_ref[kb, h] - far, tile)
            bias_sc[d, h] = tile


def _dsa_prompt_kernel(topk, q_ref, qi_ref, sm_ref, kk_ref, kb_ref, vb_ref, rb_ref, o_ref,
                       keys_sc, bias_sc, qm_sc, qim_sc, wb_sc, *head_sc):
    nh = ATT_HEADS
    cache, m_l, l_l, acc = head_sc[:nh], head_sc[nh:2 * nh], head_sc[2 * nh:3 * nh], head_sc[3 * nh:]
    i = pl.program_id(1)
    tq = q_ref.shape[1]
    n_far = bias_sc.shape[0] - 1
    n_chunks = (i + 2) // 2

    @pl.when(i == 0)
    def _():
        _bias_tiles(rb_ref, bias_sc, n_far + 1)

    lane = lax.broadcasted_iota(I32, (tq, LANES), 1)
    low_half = lane < ATT_HEAD_DIM
    lane2 = lax.broadcasted_iota(I32, (tq, CHUNK), 1)
    row2 = lax.broadcasted_iota(I32, (tq, CHUNK), 0)
    sm = sm_ref[0]

    for h in range(nh):
        keep = low_half if h % 2 == 0 else jnp.logical_not(low_half)
        pair = slice((h // 2) * LANES, (h // 2 + 1) * LANES)
        qp, qip = q_ref[0, :, pair], qi_ref[0, :, pair]
        qm_sc[h] = jnp.where(keep, qp, jnp.zeros_like(qp))
        qim_sc[h] = jnp.where(keep, qip, jnp.zeros_like(qip))
        wb_sc[h] = jnp.broadcast_to(sm[:, WIDX_LANE + h:WIDX_LANE + h + 1], (tq, LANES))
        m_l[h][...] = jnp.full((tq, LANES), NEG_BIG, F32)
        l_l[h][...] = jnp.zeros((tq, LANES), F32)
        acc[h][...] = jnp.zeros((tq, LANES), F32)

    def causal(c):
        return (c * CHUNK + lane2) <= (i * tq + row2)

    def chunk_logits(c, carry):
        start = pl.multiple_of(c * CHUNK, CHUNK)
        kk = kk_ref[0, pl.ds(start, CHUNK), :]
        sc = jnp.zeros((tq, CHUNK), F32)
        for h in range(IDX_HEADS):
            wb = wb_sc[h]
            sc = sc + jnp.maximum(_mm_nt(qim_sc[h], kk), 0.0) * jnp.concatenate([wb, wb], axis=1)
        keys_sc[c] = _monotone_key(jnp.where(causal(c), sc, -jnp.inf))
        d0 = jnp.clip(i - 2 * c, 0, n_far)
        d1 = jnp.clip(i - 2 * c - 1, 0, n_far)
        for h in range(nh):
            kblk = kb_ref[0, pl.ds(start, CHUNK), (h // 2) * LANES:(h // 2 + 1) * LANES]
            bias = jnp.concatenate([bias_sc[d0, h], bias_sc[d1, h]], axis=1)
            cache[h][c] = _mm_nt(qm_sc[h], kblk) + bias
        return carry

    lax.fori_loop(0, n_chunks, chunk_logits, 0)

    def count_ge(cand):
        def body(c, cnt):
            k = keys_sc[c]
            return cnt + jnp.where(k[:, :LANES] >= cand, 1.0, 0.0) + jnp.where(k[:, LANES:] >= cand, 1.0, 0.0)
        cnt = lax.fori_loop(0, n_chunks, body, jnp.zeros((tq, LANES), F32))
        return jnp.sum(cnt, axis=1, keepdims=True)

    tau = _kth_largest_key(count_ge, float(topk), tq)
    slots = jnp.broadcast_to(float(topk) - count_ge(tau + 1), (tq, CHUNK))
    r2 = lax.broadcasted_iota(I32, (CHUNK, CHUNK), 0)
    c2 = lax.broadcasted_iota(I32, (CHUNK, CHUNK), 1)
    upto = jnp.where(r2 <= c2, 1.0, 0.0).astype(MXU_DTYPE)

    def chunk_max(c, tied_before):
        k = keys_sc[c]
        tied = k == tau
        rank = tied_before + _mm(jnp.where(tied, 1.0, 0.0), upto)
        sel = ((k > tau) | (tied & (rank <= slots))) & causal(c)
        for h in range(nh):
            s = jnp.where(sel, cache[h][c], NEG_BIG)
            cache[h][c] = s
            m_l[h][...] = jnp.maximum(m_l[h][...], jnp.maximum(s[:, :LANES], s[:, LANES:]))
        return jnp.broadcast_to(rank[:, CHUNK - 1:CHUNK], (tq, CHUNK))

    lax.fori_loop(0, n_chunks, chunk_max, jnp.zeros((tq, CHUNK), F32))
    for h in range(nh):
        m_l[h][...] = jnp.broadcast_to(jnp.max(m_l[h][...], axis=1, keepdims=True), (tq, LANES))

    def chunk_pv(c, carry):
        start = pl.multiple_of(c * CHUNK, CHUNK)
        for h in range(nh):
            mb = m_l[h][...]
            s = cache[h][c]
            p0 = jnp.exp(s[:, :LANES] - mb)
            p1 = jnp.exp(s[:, LANES:] - mb)
            l_l[h][...] = l_l[h][...] + (p0 + p1)
            vblk = vb_ref[0, pl.ds(start, CHUNK), (h // 2) * LANES:(h // 2 + 1) * LANES]
            acc[h][...] = acc[h][...] + _mm(jnp.concatenate([p0, p1], axis=1), vblk)
        return carry

    lax.fori_loop(0, n_chunks, chunk_pv, 0)

    for g in range(nh // 2):
        lo = acc[2 * g][...] / jnp.sum(l_l[2 * g][...], axis=1, keepdims=True)
        hi = acc[2 * g + 1][...] / jnp.sum(l_l[2 * g + 1][...], axis=1, keepdims=True)
        o_ref[0, :, g * LANES:(g + 1) * LANES] = jnp.where(low_half, lo, hi)


def _dsa_prompt(q_b, qi_b, small, kk_b, k_b, v_b, rel_bias):
    b, s, _ = q_b.shape
    tq = LANES
    nq = s // tq
    topk = min(TOPK_MAX, s // 4)
    n_bias_tiles = 3
    assert MAX_DISTANCE <= LANES + 1 and s % CHUNK == 0
    n_chunks = s // CHUNK
    qblk = lambda w: pl.BlockSpec((1, tq, w), lambda i, j: (i, j, 0))
    per_b = lambda w: pl.BlockSpec((1, s, w), lambda i, j: (i, 0, 0))
    per_head = lambda shape, dt: [pltpu.VMEM(shape, dt) for _ in range(ATT_HEADS)]
    return pl.pallas_call(
        functools.partial(_dsa_prompt_kernel, topk),
        grid=(b, nq),
        in_specs=[qblk(ATT_INNER), qblk(ATT_INNER), qblk(LANES), per_b(LANES), per_b(ATT_INNER), per_b(ATT_INNER),
                  pl.BlockSpec(memory_space=pltpu.SMEM)],
        out_specs=qblk(ATT_INNER),
        out_shape=jax.ShapeDtypeStruct((b, s, ATT_INNER), F32),
        scratch_shapes=[pltpu.VMEM((n_chunks, tq, CHUNK), I32),
                        pltpu.VMEM((n_bias_tiles, ATT_HEADS, LANES, LANES), F32),
                        pltpu.VMEM((ATT_HEADS, tq, LANES), q_b.dtype), pltpu.VMEM((ATT_HEADS, tq, LANES), qi_b.dtype),
                        pltpu.VMEM((ATT_HEADS, tq, LANES), F32)]
                       + per_head((n_chunks, tq, CHUNK), F32)
                       + per_head((tq, LANES), F32)
                       + per_head((tq, LANES), F32)
                       + per_head((tq, LANES), F32),
        compiler_params=_cparams(("parallel", "arbitrary")),
        name="dsa_prompt",
    )(q_b, qi_b, small, kk_b, k_b, v_b, rel_bias)


def _dsa_sample_score_kernel(topk, t_new, pt_ref, qi_ref, w_ref, *refs):
    kidx_refs, (kknew_ref, keys_ref, keysnew_ref, tau_ref, slots_ref, keys_sc) = refs[:-6], refs[-6:]
    pg = len(kidx_refs)
    j = pl.program_id(1)
    n_pages = pl.num_programs(1) * pg
    rows = qi_ref.shape[1]
    lane = lax.broadcasted_iota(I32, (t_new, LANES), 1)
    rowi = lax.broadcasted_iota(I32, (t_new, LANES), 0)

    def head_sum(s):
        s = jnp.maximum(s, 0.0) * w_ref[0]
        return jnp.concatenate(
            [jnp.sum(s[t * IDX_HEADS:(t + 1) * IDX_HEADS], axis=0, keepdims=True) for t in range(t_new)], axis=0)

    for r, kidx_ref in enumerate(kidx_refs):
        keys_sc[j * pg + r] = _monotone_key(head_sum(_mm(qi_ref[0], kidx_ref[0])))

    @pl.when(j == pl.num_programs(1) - 1)
    def _():
        sc = head_sum(_mm(qi_ref[0], kknew_ref[0]))
        keys_new = _monotone_key(jnp.where(lane <= rowi, sc, -jnp.inf))
        keys_sc[n_pages] = keys_new

        def count_ge(cand):
            def body(jj, cnt):
                return cnt + jnp.where(keys_sc[jj] >= cand, 1.0, 0.0)
            cnt = lax.fori_loop(0, n_pages + 1, body, jnp.zeros((t_new, LANES), F32))
            return jnp.sum(cnt, axis=1, keepdims=True)

        tau = _kth_largest_key(count_ge, float(topk), t_new)
        tau_ref[0] = jnp.broadcast_to(tau, (t_new, LANES))
        slots_ref[0] = jnp.broadcast_to(float(topk) - count_ge(tau + 1), (t_new, LANES))
        keys_ref[0] = keys_sc[0:n_pages]
        keysnew_ref[0] = keys_new


def _dsa_sample_attn_kernel(t_new, past, pg, pt_ref, q_ref, keys_ref, keysnew_ref, tau_ref, slots_ref, knew_ref,
                            vnew_ref, rbrow_ref, *refs):
    k_refs, v_refs = refs[:pg], refs[pg:2 * pg]
    o_ref, cache_sc, m_sc, l_sc, acc_sc, tied_sc = refs[2 * pg:]
    phase = pl.program_id(1)
    j = pl.program_id(2)
    n_steps = pl.num_programs(2)
    n_pages = n_steps * pg
    rows = q_ref.shape[1]
    lane = lax.broadcasted_iota(I32, (rows, LANES), 1)
    rowi = lax.broadcasted_iota(I32, (rows, LANES), 0)
    tok = lax.rem(rowi, t_new)
    q = q_ref[0]

    def rows_of(x_t):
        return jnp.concatenate([x_t] * ATT_HEADS, axis=0)

    tau = rows_of(tau_ref[0])

    def page_rows(ref):
        return ref[0].reshape(ATT_INNER, PAGE_SIZE)

    def near_bias(key_pos):
        bucket = _rel_bucket(past + tok - key_pos)
        far = rbrow_ref[:, N_BUCKETS - 1:N_BUCKETS]
        bias = jnp.zeros((rows, LANES), F32)
        for kb in range(N_BUCKETS - 1):
            bias = jnp.where(bucket == kb, rbrow_ref[:, kb:kb + 1] - far, bias)
        return bias

    def selected(keys_t):
        k = rows_of(keys_t)
        tied = k == tau
        upto = jnp.where(lax.broadcasted_iota(I32, (LANES, LANES), 0) <= lax.broadcasted_iota(I32, (LANES, LANES), 1),
                         1.0, 0.0)
        rank = tied_sc[...] + _mm(jnp.where(tied, 1.0, 0.0), upto)
        tied_sc[...] = jnp.broadcast_to(rank[:, LANES - 1:LANES], (rows, LANES))
        return (k > tau) | (tied & (rank <= rows_of(slots_ref[0])))

    @pl.when(phase == 0)
    def _():
        @pl.when(j == 0)
        def _():
            tied_sc[...] = jnp.zeros(tied_sc.shape, F32)

        for r in range(pg):
            s = _mm(q, page_rows(k_refs[r]))
            cache_sc[j * pg + r] = jnp.where(selected(keys_ref[0, r]), s, NEG_BIG)

        @pl.when(j == n_steps - 1)
        def _():
            last = n_pages - 1
            cache_sc[last] = cache_sc[last] + near_bias(last * PAGE_SIZE + lane)
            s = _mm(q, knew_ref[0]) + near_bias(past + lane)
            ok = selected(keysnew_ref[0]) & (lane <= tok)
            cache_sc[n_pages] = jnp.where(ok, s, NEG_BIG)
            m = lax.fori_loop(0, n_pages + 1, lambda jj, m: jnp.maximum(m, cache_sc[jj]),
                              jnp.full((rows, LANES), NEG_BIG, F32))
            m_sc[...] = jnp.broadcast_to(jnp.max(m, axis=1, keepdims=True), (rows, LANES))
            l_sc[...] = jnp.zeros(l_sc.shape, F32)
            acc_sc[...] = jnp.zeros(acc_sc.shape, F32)

    @pl.when(phase == 1)
    def _():
        mb = m_sc[...]
        acc = acc_sc[...]
        lsum = l_sc[...]
        for r in range(pg):
            p = jnp.exp(cache_sc[j * pg + r] - mb)
            lsum = lsum + p
            acc = acc + _mm_nt(p, page_rows(v_refs[r]))
        l_sc[...] = lsum
        acc_sc[...] = acc

        @pl.when(j == n_steps - 1)
        def _():
            p_new = jnp.exp(cache_sc[n_pages] - mb)
            l = jnp.sum(lsum + p_new, axis=1, keepdims=True)
            a = (acc + _mm_nt(p_new, vnew_ref[0])) / l
            col_head = lax.broadcasted_iota(I32, (t_new, ATT_INNER), 1) // ATT_HEAD_DIM
            out = jnp.zeros((t_new, ATT_INNER), F32)
            for h in range(ATT_HEADS):
                out = jnp.where(col_head == h, a[h * t_new:(h + 1) * t_new], out)
            o_ref[0] = out


def _dsa_sample(q_b, qi_b, small, k_b, v_b, kk_b, cache_k, cache_v, cache_kidx, page_table, rel_bias):
    b, t_new, _ = q_b.shape
    n_pages = page_table.shape[1]
    past = n_pages * PAGE_SIZE
    topk = min(TOPK_MAX, (past + t_new) // 4)
    rows = t_new * ATT_HEADS
    assert PAGE_SIZE >= MAX_DISTANCE

    def pages_per_step(limit):
        return max(g for g in (1, 2, 4, 8, 16, 32) if g <= limit and n_pages % g == 0)

    kidx_t = jnp.transpose(cache_kidx, (0, 2, 1))
    k_t = jnp.transpose(cache_k, (0, 2, 3, 1))
    v_t = jnp.transpose(cache_v, (0, 2, 3, 1))
    new_cols = lambda x: jnp.pad(jnp.swapaxes(x, 1, 2), ((0, 0), (0, 0), (0, PAGE_SIZE - t_new)))

    pgs = pages_per_step(32)
    qi_rows = qi_b.reshape(b, rows, IDX_DIM)
    w_rows = small[:, :, WIDX_LANE:WIDX_LANE + IDX_HEADS].reshape(b, rows, 1)
    kk_new = new_cols(kk_b[:, :, :IDX_DIM])
    per_b = lambda s: pl.BlockSpec((1,) + s, lambda i, j, pt: (i,) + (0,) * len(s))
    kidx_page = lambda r: pl.BlockSpec((1, IDX_DIM, PAGE_SIZE), lambda i, j, pt: (pt[i, j * pgs + r], 0, 0))
    keys, keys_new, tau, slots = pl.pallas_call(
        functools.partial(_dsa_sample_score_kernel, topk, t_new),
        grid_spec=pltpu.PrefetchScalarGridSpec(
            num_scalar_prefetch=1, grid=(b, n_pages // pgs),
            in_specs=[per_b((rows, IDX_DIM)), per_b((rows, 1))] + [kidx_page(r) for r in range(pgs)]
                     + [per_b((IDX_DIM, LANES))],
            out_specs=(per_b((n_pages, t_new, LANES)), per_b((t_new, LANES)), per_b((t_new, LANES)),
                       per_b((t_new, LANES))),
            scratch_shapes=[pltpu.VMEM((n_pages + 1, t_new, LANES), I32)]),
        out_shape=(jax.ShapeDtypeStruct((b, n_pages, t_new, LANES), I32),
                   jax.ShapeDtypeStruct((b, t_new, LANES), I32),
                   jax.ShapeDtypeStruct((b, t_new, LANES), I32),
                   jax.ShapeDtypeStruct((b, t_new, LANES), F32)),
        compiler_params=_cparams(("parallel", "arbitrary")),
        name="dsa_sample_scores",
    )(page_table, qi_rows, w_rows, *([kidx_t] * pgs), kk_new)

    pg = pages_per_step(16)
    n_steps = n_pages // pg
    head_of_col = jnp.arange(ATT_INNER) // ATT_HEAD_DIM
    q_heads = jnp.where(head_of_col[None, None, None, :] == jnp.arange(ATT_HEADS)[None, :, None, None],
                        q_b[:, None, :, :], jnp.zeros((), q_b.dtype))
    q_rows = q_heads.reshape(b, rows, ATT_INNER)
    rb_rows = jnp.repeat(rel_bias.T, t_new, axis=0)
    k_new, v_new = new_cols(k_b), new_cols(v_b)
    per_b = lambda s: pl.BlockSpec((1,) + s, lambda i, ph, j, pt: (i,) + (0,) * len(s))
    page_blk = (1, ATT_HEADS, ATT_HEAD_DIM, PAGE_SIZE)
    k_page = lambda r: pl.BlockSpec(
        page_blk, lambda i, ph, j, pt: (pt[i, jnp.where(ph == 0, j, n_steps - 1) * pg + r], 0, 0, 0))
    v_page = lambda r: pl.BlockSpec(
        page_blk, lambda i, ph, j, pt: (pt[i, jnp.where(ph == 1, j, 0) * pg + r], 0, 0, 0))
    return pl.pallas_call(
        functools.partial(_dsa_sample_attn_kernel, t_new, past, pg),
        grid_spec=pltpu.PrefetchScalarGridSpec(
            num_scalar_prefetch=1, grid=(b, 2, n_steps),
            in_specs=[per_b((rows, ATT_INNER)),
                      pl.BlockSpec((1, pg, t_new, LANES),
                                   lambda i, ph, j, pt: (i, jnp.where(ph == 0, j, n_steps - 1), 0, 0)),
                      per_b((t_new, LANES)), per_b((t_new, LANES)), per_b((t_new, LANES)),
                      per_b((ATT_INNER, LANES)), per_b((ATT_INNER, LANES)),
                      pl.BlockSpec((rows, N_BUCKETS), lambda i, ph, j, pt: (0, 0))]
                     + [k_page(r) for r in range(pg)] + [v_page(r) for r in range(pg)],
            out_specs=per_b((t_new, ATT_INNER)),
            scratch_shapes=[pltpu.VMEM((n_pages + 1, rows, LANES), F32), pltpu.VMEM((rows, LANES), F32),
                            pltpu.VMEM((rows, LANES), F32), pltpu.VMEM((rows, ATT_INNER), F32),
                            pltpu.VMEM((rows, LANES), F32)]),
        out_shape=jax.ShapeDtypeStruct((b, t_new, ATT_INNER), F32),
        compiler_params=_cparams(("parallel", "arbitrary", "arbitrary")),
        name="dsa_sample_attend",
    )(page_table, q_rows, keys, keys_new, tau, slots, k_new, v_new, rb_rows, *([k_t] * pg), *([v_t] * pg))


def _matmul_kernel(x_ref, w_ref, o_ref):
    o_ref[...] = _mm(x_ref[...], w_ref[...])


def _matmul(x2d, w):
    n, kdim = x2d.shape
    tm = min(256, n)
    return pl.pallas_call(
        _matmul_kernel,
        grid=(n // tm,),
        in_specs=[pl.BlockSpec((tm, kdim), lambda i: (i, 0)), pl.BlockSpec(w.shape, lambda i: (0, 0))],
        out_specs=pl.BlockSpec((tm, w.shape[1]), lambda i: (i, 0)),
        out_shape=jax.ShapeDtypeStruct((n, w.shape[1]), F32),
        compiler_params=_cparams(("parallel",)),
        name="mem_kv_proj",
    )(x2d, w)


def _out_proj_kernel(x_ref, ssd_ref, att_ref, wout_ref, g_ref, b_ref, wcq_ref, h1_ref, qm_ref):
    m = _mm(ssd_ref[...], wout_ref[:SSD_INNER, :]) + _mm(att_ref[...], wout_ref[SSD_INNER:, :])
    h1 = _layer_norm(ALPHA * x_ref[...] + m, g_ref[...], b_ref[...])
    h1_ref[...] = h1
    qm_ref[...] = _mm(h1, wcq_ref[...])


def _out_proj(x2d, ssd2d, att2d, w_out, g, b, w_cq):
    n = x2d.shape[0]
    tm = min(256, n)
    row = lambda w: pl.BlockSpec((tm, w), lambda i: (i, 0))
    full = lambda a: pl.BlockSpec(a.shape, lambda i: (0, 0))
    return pl.pallas_call(
        _out_proj_kernel,
        grid=(n // tm,),
        in_specs=[row(D_MODEL), row(SSD_INNER), row(ATT_INNER), full(w_out), full(g), full(b), full(w_cq)],
        out_specs=(row(D_MODEL), row(MEM_INNER)),
        out_shape=(jax.ShapeDtypeStruct((n, D_MODEL), F32), jax.ShapeDtypeStruct((n, MEM_INNER), F32)),
        compiler_params=_cparams(("parallel",)),
        name="out_proj_ln1",
    )(x2d, ssd2d, att2d, w_out, g, b, w_cq)


def _mem_attend_kernel(q_ref, mk_ref, mv_ref, o_ref):
    q = q_ref[0]
    for h in range(MEM_HEADS):
        sl = slice(h * MEM_HEAD_DIM, (h + 1) * MEM_HEAD_DIM)
        logits = _mm_nt(q[:, sl], mk_ref[0, :, sl]) * (MEM_HEAD_DIM ** -0.5)
        logits = logits - jnp.max(logits, axis=-1, keepdims=True)
        p = jnp.exp(logits)
        p = p / jnp.sum(p, axis=-1, keepdims=True)
        o_ref[0, :, sl] = _mm(p, mv_ref[0, :, sl])


def _mem_attend(qm, mk, mv):
    b, length, _ = qm.shape
    tq = min(512, length)
    m_tok = mk.shape[1]
    return pl.pallas_call(
        _mem_attend_kernel,
        grid=(b, length // tq),
        in_specs=[pl.BlockSpec((1, tq, MEM_INNER), lambda i, j: (i, j, 0)),
                  pl.BlockSpec((1, m_tok, MEM_INNER), lambda i, j: (i, 0, 0)),
                  pl.BlockSpec((1, m_tok, MEM_INNER), lambda i, j: (i, 0, 0))],
        out_specs=pl.BlockSpec((1, tq, MEM_INNER), lambda i, j: (i, j, 0)),
        out_shape=jax.ShapeDtypeStruct((b, length, MEM_INNER), F32),
        compiler_params=_cparams(("parallel", "parallel")),
        name="mem_attend",
    )(qm, mk, mv)


def _top_rows(s, n_top, with_index):
    rows = s.shape[0]
    ridx = lax.broadcasted_iota(I32, s.shape, 0).astype(F32)
    vals, idxs = [], []
    for _ in range(n_top):
        m = jnp.max(s, axis=0, keepdims=True)
        first = jnp.min(jnp.where(s == m, ridx, float(rows)), axis=0, keepdims=True)
        s = jnp.where(ridx == first, -jnp.inf, s)
        vals.append(m)
        idxs.append(first)
    return jnp.concatenate(vals, axis=0), (jnp.concatenate(idxs, axis=0) if with_index else None)


_SUBLANES = 8
ROUTE_HEADS_PER_TRIP = 8


def _staircase_blocks(n):
    by_b, single_b = [], []
    for a in range(n):
        count = n // (a + 1)
        if count > 1:
            by_b += [(a, b) for b in range(0, count, _SUBLANES)]
        else:
            single_b.append(a)
    assert len(single_b) % _SUBLANES == 0 and single_b == list(range(n - len(single_b), n))
    return tuple(by_b), tuple(single_b[::_SUBLANES])


_STAIRCASE = _staircase_blocks(PEER_TOPK)


def _peer_route_kernel(h1_ref, o_ref, wco_ref, g_ref, b_ref, wq_ref, sk_ref, h2_ref, e_ref, gate_ref, h2b_sc):
    h2 = _layer_norm(ALPHA * h1_ref[...] + _mm(o_ref[...], wco_ref[...]), g_ref[...], b_ref[...])
    h2_ref[...] = h2
    h2b_sc[...] = h2.astype(MXU_DTYPE)

    def route_head(h, carry):
        tops = []
        for p in range(2):
            c = h * 2 + p
            qs = jnp.dot(h2b_sc[...], wq_ref[c], preferred_element_type=F32)
            s_t = _mm_nt(sk_ref[c], qs)
            tops.append(_top_rows(s_t, PEER_TOPK, True))
        (v0, i0), (v1, i1) = tops
        by_b, by_a = _STAIRCASE
        sub = _SUBLANES
        cand = jnp.concatenate([v0[a:a + 1] + v1[b:b + sub] for a, b in by_b]
                               + [v0[a:a + sub] + v1[0:1] for a in by_a], axis=0)
        cidx = jnp.concatenate([i0[a:a + 1] * PEER_KEYS + i1[b:b + sub] for a, b in by_b]
                               + [i0[a:a + sub] * PEER_KEYS + i1[0:1] for a in by_a], axis=0)
        pos = lax.broadcasted_iota(I32, cand.shape, 0).astype(F32)
        gs, es = [], []
        for _ in range(PEER_TOPK):
            m = jnp.max(cand, axis=0, keepdims=True)
            first = jnp.min(jnp.where(cand == m, pos, float(cand.shape[0])), axis=0, keepdims=True)
            hit = pos == first
            es.append(jnp.max(jnp.where(hit, cidx, -1.0), axis=0, keepdims=True))
            cand = jnp.where(hit, -jnp.inf, cand)
            gs.append(m)
        g_s = jnp.concatenate(gs, axis=0)
        ex = jnp.exp(g_s - g_s[0:1])
        rows = pl.ds(pl.multiple_of(h * PEER_TOPK, PEER_TOPK), PEER_TOPK)
        gate_ref[0, rows, :] = ex / jnp.sum(ex, axis=0, keepdims=True)
        e_ref[0, rows, :] = jnp.concatenate(es, axis=0).astype(I32) * SLAB_ROWS
        return carry

    def route_group(g, carry):
        for k in range(ROUTE_HEADS_PER_TRIP):
            route_head(g * ROUTE_HEADS_PER_TRIP + k, carry)
        return carry

    lax.fori_loop(0, PEER_HEADS // ROUTE_HEADS_PER_TRIP, route_group, 0)


def _peer_route(h1, o, w_co, g, b, wq, subkeys):
    n = h1.shape[0]
    tm = TOK_BLOCK
    nb = n // tm
    row = lambda w: pl.BlockSpec((tm, w), lambda i: (i, 0))
    full = lambda a: pl.BlockSpec(a.shape, lambda i: (0,) * a.ndim)
    pairs = pl.BlockSpec((1, PEER_PAIRS, tm), lambda i: (i, 0, 0))
    return pl.pallas_call(
        _peer_route_kernel,
        grid=(nb,),
        in_specs=[row(D_MODEL), row(MEM_INNER), full(w_co), full(g), full(b), full(wq), full(subkeys)],
        out_specs=(row(D_MODEL), pairs, pairs),
        out_shape=(jax.ShapeDtypeStruct((n, D_MODEL), F32),
                   jax.ShapeDtypeStruct((nb, PEER_PAIRS, tm), I32),
                   jax.ShapeDtypeStruct((nb, PEER_PAIRS, tm), F32)),
        scratch_shapes=[pltpu.VMEM((tm, D_MODEL), MXU_DTYPE)],
        compiler_params=_cparams(("parallel",)),
        name="ln2_peer_route",
    )(h1, o, w_co, g, b, wq, subkeys)


def _pack_table(t):
    tb = lax.bitcast_convert_type(t.astype(BF16), jnp.uint16).astype(jnp.uint32)
    half = t.shape[1] // 2
    words = tb[:, :half] | (tb[:, half:] << 16)
    return lax.bitcast_convert_type(words, I32).reshape(t.shape[0] * SLAB_ROWS, LANES)


def _unpack_words(w):
    lo = lax.bitcast_convert_type(w << 16, F32)
    hi = lax.bitcast_convert_type(w & jnp.int32(-65536), F32)
    return lo, hi


def _gelu_tanh(x):
    return 0.5 * x * (1.0 + jnp.tanh(math.sqrt(2.0 / math.pi) * (x + 0.044715 * (x * x * x))))


def _gather_token_rows(idx_smem, slot, tab_ref, tile_ref, t):
    for p in range(PEER_PAIRS):
        row = pl.multiple_of(idx_smem[slot, t, p], SLAB_ROWS)
        tile_ref[pl.ds(p, SLAB_ROWS, stride=TILE_STRIDE), :] = tab_ref[pl.ds(row, SLAB_ROWS), :]


def _load_indices(e_hbm, idx_smem, sem):
    i = pl.program_id(0)
    slot = lax.rem(i, 2)
    copy = lambda step, s: pltpu.make_async_copy(e_hbm.at[step], idx_smem.at[s], sem.at[s])

    @pl.when(i == 0)
    def _():
        copy(0, 0).start()

    @pl.when(i + 1 < pl.num_programs(0))
    def _():
        copy(i + 1, 1 - slot).start()

    copy(i, slot).wait()
    return slot


def _trip_rows(g):
    return pl.ds(pl.multiple_of(g * GATHER_TOKENS, GATHER_TOKENS), GATHER_TOKENS)


def _peer_in_kernel(e_hbm, tab_ref, h2_ref, gate_ref, w_ref, idx_smem, sem, *tiles):
    slot = _load_indices(e_hbm, idx_smem, sem)
    tm = h2_ref.shape[0]
    lane = lax.broadcasted_iota(I32, (PEER_PAIRS, tm), 1)

    def tok_group(g, act_t):
        x_rows = h2_ref[_trip_rows(g), :]
        cols = []
        _gather_token_rows(idx_smem, slot, tab_ref, tiles[0], g * GATHER_TOKENS)
        for u, tile_ref in enumerate(tiles):
            if u + 1 < GATHER_TOKENS:
                _gather_token_rows(idx_smem, slot, tab_ref, tiles[u + 1], g * GATHER_TOKENS + u + 1)
            x_chunk = lambda c: x_rows[u:u + 1, c * LANES:(c + 1) * LANES]
            acc = jnp.zeros((PEER_PAIRS, LANES), F32)
            for j in range(SLAB_ROWS):
                lo, hi = _unpack_words(tile_ref[j * TILE_STRIDE:j * TILE_STRIDE + PEER_PAIRS, :])
                acc = acc + lo * x_chunk(j) + hi * x_chunk(SLAB_ROWS + j)
            cols.append(jnp.sum(acc, axis=1, keepdims=True))
        for u, col in enumerate(cols):
            act_t = jnp.where(lane == g * GATHER_TOKENS + u, col, act_t)
        return act_t

    act_t = lax.fori_loop(0, tm // GATHER_TOKENS, tok_group, jnp.zeros((PEER_PAIRS, tm), F32))
    w_ref[0] = gate_ref[0] * _gelu_tanh(act_t)


def _peer_out_kernel(e_hbm, tab_ref, h2_ref, w_ref, g_ref, b_ref, y_ref, idx_smem, sem, out_sc, *tiles):
    slot = _load_indices(e_hbm, idx_smem, sem)
    tm = h2_ref.shape[0]
    lane = lax.broadcasted_iota(I32, (PEER_PAIRS, tm), 1)

    def tok_group(g, carry):
        rows = []
        _gather_token_rows(idx_smem, slot, tab_ref, tiles[0], g * GATHER_TOKENS)
        for u, tile_ref in enumerate(tiles):
            t = g * GATHER_TOKENS + u
            if u + 1 < GATHER_TOKENS:
                _gather_token_rows(idx_smem, slot, tab_ref, tiles[u + 1], t + 1)
            w_col = jnp.sum(jnp.where(lane == t, w_ref[0], 0.0), axis=1, keepdims=True)
            los, his = [], []
            for j in range(SLAB_ROWS):
                lo, hi = _unpack_words(tile_ref[j * TILE_STRIDE:j * TILE_STRIDE + PEER_PAIRS, :])
                los.append(jnp.sum(lo * w_col, axis=0, keepdims=True))
                his.append(jnp.sum(hi * w_col, axis=0, keepdims=True))
            rows.append(jnp.concatenate(los + his, axis=1))
        out_sc[_trip_rows(g), :] = jnp.concatenate(rows, axis=0)
        return carry

    lax.fori_loop(0, tm // GATHER_TOKENS, tok_group, 0)
    y_ref[...] = _layer_norm(ALPHA * h2_ref[...] + out_sc[...], g_ref[...], b_ref[...])


def _peer_scratch(tm):
    assert GATHER_TOKENS == _SUBLANES and tm % GATHER_TOKENS == 0
    return ([pltpu.SMEM((2, tm, PEER_PAIRS), I32), pltpu.SemaphoreType.DMA((2,))]
            + [pltpu.VMEM((SLAB_ROWS * TILE_STRIDE, LANES), I32) for _ in range(GATHER_TOKENS)])


def _peer_in(e_rows, u_tab, h2, gate_t):
    nb, _, tm = gate_t.shape
    pairs = pl.BlockSpec((1, PEER_PAIRS, tm), lambda i: (i, 0, 0))
    return pl.pallas_call(
        _peer_in_kernel,
        grid=(nb,),
        in_specs=[pl.BlockSpec(memory_space=pl.ANY), pl.BlockSpec(memory_space=pltpu.VMEM),
                  pl.BlockSpec((tm, D_MODEL), lambda i: (i, 0)), pairs],
        out_specs=pairs,
        out_shape=jax.ShapeDtypeStruct((nb, PEER_PAIRS, tm), F32),
        scratch_shapes=_peer_scratch(tm),
        compiler_params=_cparams(("arbitrary",)),
        name="peer_expert_in",
    )(e_rows, u_tab, h2, gate_t)


def _peer_out(e_rows, v_tab, h2, w_t, g, b):
    nb, _, tm = w_t.shape
    pairs = pl.BlockSpec((1, PEER_PAIRS, tm), lambda i: (i, 0, 0))
    row = pl.BlockSpec((tm, D_MODEL), lambda i: (i, 0))
    vec = pl.BlockSpec((1, D_MODEL), lambda i: (0, 0))
    scratch = _peer_scratch(tm)
    return pl.pallas_call(
        _peer_out_kernel,
        grid=(nb,),
        in_specs=[pl.BlockSpec(memory_space=pl.ANY), pl.BlockSpec(memory_space=pltpu.VMEM), row, pairs, vec, vec],
        out_specs=row,
        out_shape=jax.ShapeDtypeStruct((nb * tm, D_MODEL), F32),
        scratch_shapes=scratch[:2] + [pltpu.VMEM((tm, D_MODEL), F32)] + scratch[2:],
        compiler_params=_cparams(("arbitrary",)),
        name="peer_expert_out_ln3",
    )(e_rows, v_tab, h2, w_t, g, b)


def _lane_row(vals, offset):
    return jnp.zeros((1, LANES), F32).at[0, offset:offset + vals.shape[0]].set(vals)


def _finish(x2d, ssd2d, att2d, mem_attend_fn, p):
    h1, qm = _out_proj(x2d, ssd2d, att2d, p["w_out"], p["ln1_g"], p["ln1_b"], p["w_cq"])
    o = mem_attend_fn(qm)
    h2, e_t, gate_t = _peer_route(h1, o, p["w_co"], p["ln2_g"], p["ln2_b"], p["peer_wq"], p["subkeys"])
    e_rows = jnp.swapaxes(e_t, 1, 2)
    w_t = _peer_in(e_rows, p["u_tab"], h2, gate_t)
    return _peer_out(e_rows, p["v_tab"], h2, w_t, p["ln3_g"], p["ln3_b"])


def kernel(x_prompt, mem_prompt, x_sample, cache_k, cache_v, cache_kidx, cache_mem_k, cache_mem_v, state_ssm, state_conv, page_table, w_in, conv_w, conv_b, dt_bias, a_log, d_skip, ssd_norm_w, rel_bias, w_out, ln1_g, ln1_b, w_cq, w_ck, w_cv, w_co, ln2_g, ln2_b, peer_wq, peer_subkeys, peer_u, peer_v, ln3_g, ln3_b):
    bp, seq, _ = x_prompt.shape
    bs, t_s, _ = x_sample.shape
    m_tok = mem_prompt.shape[1]

    c0 = SSD_INNER + CONV_DIM
    wa = w_in[:, :c0].astype(MXU_DTYPE)
    c1 = c0 + SSD_HEADS
    wb = w_in[:, c1:c1 + 4 * ATT_INNER].astype(MXU_DTYPE)
    c2 = c1 + 4 * ATT_INNER
    w_kidx = w_in[:, c2:c2 + IDX_DIM]
    wc = jnp.concatenate([w_kidx, w_kidx], axis=1).astype(MXU_DTYPE)
    wd = jnp.concatenate([w_kidx, w_in[:, c0:c1], w_in[:, c2 + IDX_DIM:],
                          jnp.zeros((D_MODEL, LANES - IDX_DIM - SSD_HEADS - IDX_HEADS), F32)], axis=1).astype(MXU_DTYPE)
    row = lambda v: v.reshape(1, -1)
    p = dict(
        w_out=w_out.astype(MXU_DTYPE), ln1_g=row(ln1_g), ln1_b=row(ln1_b), w_cq=w_cq.astype(MXU_DTYPE),
        w_co=w_co.astype(MXU_DTYPE), ln2_g=row(ln2_g), ln2_b=row(ln2_b),
        peer_wq=peer_wq.astype(MXU_DTYPE).reshape(D_MODEL, PEER_HEADS * 2, PEER_HALF).swapaxes(0, 1),
        subkeys=peer_subkeys.reshape(PEER_HEADS * 2, PEER_KEYS, PEER_HALF).astype(MXU_DTYPE),
        u_tab=_pack_table(peer_u), v_tab=_pack_table(peer_v), ln3_g=row(ln3_g), ln3_b=row(ln3_b))
    ssd_params = (conv_w, row(conv_b), _lane_row(a_log, DT_LANE), _lane_row(dt_bias, DT_LANE),
                  row(jnp.repeat(d_skip, SSD_HEAD_DIM)), row(ssd_norm_w))
    state_t = lambda s: jnp.swapaxes(s, -1, -2)

    n_p = bp * seq
    z, xbc, q_b, k_p, v_p, k_b, v_b, qi_b, kk_b, small = _project_in(x_prompt.reshape(n_p, D_MODEL), wa, wb, wc, wd)
    r3 = lambda a, b_: a.reshape(b_, -1, a.shape[-1])
    ssd_out, ssm_t, conv_p = _ssd_mixer(
        r3(z, bp), r3(xbc, bp), r3(small, bp), jnp.zeros((bp, SSD_CONV - 1, CONV_DIM), F32),
        jnp.zeros((bp, SSD_HEADS, SSD_STATE, SSD_HEAD_DIM), F32), SSD_CHUNK, *ssd_params)
    att_out = _dsa_prompt(r3(q_b, bp), r3(qi_b, bp), r3(small, bp), r3(kk_b, bp), r3(k_b, bp), r3(v_b, bp), rel_bias)
    kidx_p = r3(small, bp)[:, :, :IDX_DIM]
    mem_kv = _matmul(mem_prompt.reshape(bp * m_tok, D_MODEL),
                     jnp.concatenate([w_ck, w_cv], axis=1).astype(MXU_DTYPE))
    memk_p = mem_kv[:, :MEM_INNER].reshape(bp, m_tok, MEM_INNER)
    memv_p = mem_kv[:, MEM_INNER:].reshape(bp, m_tok, MEM_INNER)
    y_prompt = _finish(x_prompt.reshape(n_p, D_MODEL), ssd_out.reshape(n_p, SSD_INNER),
                       att_out.reshape(n_p, ATT_INNER),
                       lambda qm: _mem_attend(qm.reshape(bp, seq, MEM_INNER), memk_p, memv_p).reshape(n_p, MEM_INNER),
                       p)

    n_s = bs * t_s
    n_pad = -(-n_s // TOK_BLOCK) * TOK_BLOCK
    xs2d = jnp.pad(x_sample.reshape(n_s, D_MODEL), ((0, n_pad - n_s), (0, 0)))
    outs = _project_in(xs2d, wa, wb, wc, wd)
    z, xbc, q_b, k_s, v_s, k_b, v_b, qi_b, kk_b, small = [r3(a[:n_s], bs) for a in outs]
    pad_chunk = lambda a: jnp.pad(a, ((0, 0), (0, SSD_CHUNK - t_s), (0, 0)))
    ssd_s, ssm_s_t, conv_s = _ssd_mixer(pad_chunk(z), pad_chunk(xbc), pad_chunk(small), state_conv,
                                        state_t(state_ssm), t_s, *ssd_params)
    att_s = _dsa_sample(q_b, qi_b, small, k_b, v_b, kk_b, cache_k, cache_v, cache_kidx, page_table, rel_bias)
    pad_rows = lambda a: jnp.pad(a.reshape(n_s, -1), ((0, n_pad - n_s), (0, 0)))
    cmk = cache_mem_k.reshape(bs, m_tok, MEM_INNER)
    cmv = cache_mem_v.reshape(bs, m_tok, MEM_INNER)
    y_s = _finish(xs2d, pad_rows(ssd_s[:, :t_s]), pad_rows(att_s),
                  lambda qm: pad_rows(_mem_attend(qm[:n_s].reshape(bs, t_s, MEM_INNER), cmk, cmv)), p)
    y_sample = y_s[:n_s].reshape(bs, t_s, D_MODEL)

    heads = lambda a, b_: a.reshape(b_, -1, ATT_HEADS, ATT_HEAD_DIM)
    mem_heads = lambda a: a.reshape(bp, m_tok, MEM_HEADS, MEM_HEAD_DIM)
    return (y_prompt.reshape(bp, seq, D_MODEL), y_sample,
            heads(k_p, bp), heads(v_p, bp), kidx_p, mem_heads(memk_p), mem_heads(memv_p),
            state_t(ssm_t), conv_p,
            heads(k_s, bs), heads(v_s, bs), small[:, :, :IDX_DIM], state_t(ssm_s_t), conv_s)
```
